```python
import jax, jax.numpy as jnp
from jax import lax
import numpy as np

D_MODEL = 1024
BATCH = 8
SEQ = 2048
DEPTH = 4

CHUNK = 64
SB_BLOCK = 128
NORM_EPS = 1e-6
MASK_NEG = -1e30
LB_FLOOR = 1e-30
SB_HEADS = 8
SB_HEAD_DIM = 64
SB_WIDTH = SB_HEADS * SB_HEAD_DIM
HG_HEADS = 4
HG_HEAD_DIM = 128
HG_WIDTH = HG_HEADS * HG_HEAD_DIM
ML_HEADS = 4
ML_HEAD_DIM = 128
ML_WIDTH = ML_HEADS * ML_HEAD_DIM
ML_CONV = 4
N_BRANCH = 3
BRANCH_WIDTH = 512
IN_SPLITS = (SB_WIDTH, SB_WIDTH, SB_WIDTH,
             HG_WIDTH, HG_WIDTH, HG_WIDTH, HG_WIDTH,
             2 * ML_WIDTH, ML_WIDTH, ML_WIDTH, 2 * ML_HEADS,
             N_BRANCH * D_MODEL)
IN_WIDTH = 3 * SB_WIDTH + 4 * HG_WIDTH + 4 * ML_WIDTH + 2 * ML_HEADS + N_BRANCH * D_MODEL
PK_HEADS = 8
PK_NKEYS = 128
PK_NEXPERTS = PK_NKEYS * PK_NKEYS
PK_DKEY = 128
PK_TOPK = 16
PK_TOKEN_GROUP = 128

kernel_name = "hybrid_sb_hgrn2_mlstm_peer_adaln"


def rmsnorm(x, g):
    xf = x.astype(jnp.float32)
    y = xf * lax.rsqrt(jnp.mean(jnp.square(xf), axis=-1, keepdims=True) + NORM_EPS)
    return (y * g.astype(jnp.float32)).astype(x.dtype)


def _to_chunks(a):
    B, S, H, d = a.shape
    return a.reshape(B, S // CHUNK, CHUNK, H, d).transpose(1, 0, 3, 2, 4)


def _from_chunks(a):
    nC, B, H, L, d = a.shape
    return a.transpose(1, 0, 3, 2, 4).reshape(B, nC * L, H, d)


def stick_breaking_attention(q, k, v):
    B, S, H, dh = q.shape
    f32 = jnp.float32
    scale = dh ** -0.5
    outs = []
    for blk in range(S // SB_BLOCK):
        t0 = blk * SB_BLOCK
        kv_len = t0 + SB_BLOCK
        qb = q[:, t0:kv_len].astype(f32)
        kb = k[:, :kv_len].astype(f32)
        vb = v[:, :kv_len].astype(f32)
        z = jnp.einsum('bthd,bshd->bhts', qb, kb) * scale
        t_idx = t0 + jnp.arange(SB_BLOCK)[:, None]
        s_idx = jnp.arange(kv_len)[None, :]
        past = s_idx < t_idx
        log_keep = jnp.where(past, jax.nn.log_sigmoid(-z), 0.0)
        later = lax.cumsum(log_keep, axis=3, reverse=True) - log_keep
        w = jnp.where(past, jnp.exp(jax.nn.log_sigmoid(z) + later), 0.0)
        outs.append(jnp.einsum('bhts,bshd->bthd', w, vb))
    return jnp.concatenate(outs, axis=1)


def hgrn2_chunkwise(q, log_f, k, i):
    B, S, H, dk = q.shape
    dv = i.shape[-1]
    f32 = jnp.float32
    causal = jnp.tril(jnp.ones((CHUNK, CHUNK), dtype=bool))
    mask = causal[None, None, :, :, None]

    def step(state, xs):
        qb, fb, kb, ib = xs
        b = jnp.cumsum(fb, axis=2)
        diff = b[:, :, :, None, :] - b[:, :, None, :, :]
        decay = jnp.where(mask, jnp.exp(jnp.where(mask, diff, 0.0)), 0.0)
        scores = jnp.einsum('bhtsk,bhsk->bhts', decay * qb[:, :, :, None, :], kb)
        o = (jnp.einsum('bhts,bhsv->bhtv', scores, ib)
             + jnp.einsum('bhtk,bhkv->bhtv', qb * jnp.exp(b), state))
        b_last = b[:, :, -1, :]
        new_state = (jnp.exp(b_last)[..., None] * state
                     + jnp.einsum('bhsk,bhsv->bhkv', kb * jnp.exp(b_last[:, :, None, :] - b), ib))
        return new_state, o

    xs = tuple(_to_chunks(a.astype(f32)) for a in (q, log_f, k, i))
    s0 = jnp.zeros((B, H, dk, dv), f32)
    _, o = lax.scan(step, s0, xs)
    return _from_chunks(o)


def mlstm_chunkwise(q, k, v, log_i, log_f):
    B, S, H, d = q.shape
    f32 = jnp.float32
    k = k.astype(f32) * (d ** -0.5)
    causal = jnp.tril(jnp.ones((CHUNK, CHUNK), dtype=bool))

    def step(carry, xs):
        C, n, m = carry
        qb, kb, vb, ib, fb = xs
        b = jnp.cumsum(fb, axis=-1)
        log_d = jnp.where(causal, b[..., :, None] - b[..., None, :] + ib[..., None, :], MASK_NEG)
        log_inter = b + m[..., None]
        m_t = jnp.maximum(log_inter, jnp.max(log_d, axis=-1))
        dmat = jnp.where(causal, jnp.exp(log_d - m_t[..., None]), 0.0)
        inter = jnp.exp(log_inter - m_t)
        qk = jnp.einsum('bhtd,bhsd->bhts', qb, kb) * dmat
        num = (jnp.einsum('bhts,bhsv->bhtv', qk, vb)
               + inter[..., None] * jnp.einsum('bhvk,bhtk->bhtv', C, qb))
        den = jnp.sum(qk, axis=-1) + inter * jnp.einsum('bhk,bhtk->bht', n, qb)
        h = num / jnp.maximum(jnp.abs(den), jnp.exp(-m_t))[..., None]
        b_last = b[..., -1]
        log_w = b_last[..., None] - b + ib
        m_new = jnp.maximum(b_last + m, jnp.max(log_w, axis=-1))
        w = jnp.exp(log_w - m_new[..., None])
        dec = jnp.exp(b_last + m - m_new)
        C_new = dec[..., None, None] * C + jnp.einsum('bhsv,bhsk->bhvk', vb * w[..., None], kb)
        n_new = dec[..., None] * n + jnp.einsum('bhs,bhsk->bhk', w, kb)
        return (C_new, n_new, m_new), h

    qc, kc, vc = (_to_chunks(a.astype(f32)) for a in (q, k, v))
    ic = _to_chunks(log_i.astype(f32)[..., None])[..., 0]
    fc = _to_chunks(log_f.astype(f32)[..., None])[..., 0]
    carry0 = (jnp.zeros((B, H, d, d), f32), jnp.zeros((B, H, d), f32), jnp.zeros((B, H), f32))
    _, h = lax.scan(step, carry0, (qc, kc, vc, ic, fc))
    return _from_chunks(h)


def causal_short_conv(x, w, b):
    K, C = w.shape
    y = lax.conv_general_dilated(x, w[:, None, :].astype(x.dtype), window_strides=(1,),
                                 padding=[(K - 1, 0)], dimension_numbers=('NWC', 'WIO', 'NWC'),
                                 feature_group_count=C)
    return y + b


def hybrid_mixer(h, w_in, conv_w, conv_b, gate_b, lower_bound, hg_norm_g, ml_norm_g, w_branch, w_out):
    B, S, _ = h.shape
    proj = h @ w_in
    idx = np.cumsum(IN_SPLITS)[:-1].tolist()
    (sb_q, sb_k, sb_v, hg_q, hg_f, hg_i, hg_g,
     ml_qk, ml_v, ml_o, ml_if, br_g) = jnp.split(proj, idx, axis=-1)

    sb_shape = (B, S, SB_HEADS, SB_HEAD_DIM)
    y_a = stick_breaking_attention(sb_q.reshape(sb_shape), sb_k.reshape(sb_shape),
                                   sb_v.reshape(sb_shape)).reshape(B, S, SB_WIDTH)

    hg_shape = (B, S, HG_HEADS, HG_HEAD_DIM)
    lb = lower_bound.reshape(HG_HEADS, HG_HEAD_DIM)
    z = hg_f.astype(jnp.float32).reshape(hg_shape)
    log_lb = jnp.log(jnp.maximum(lb, LB_FLOOR))
    log_f = jnp.logaddexp(log_lb, jnp.log1p(-lb) + jax.nn.log_sigmoid(z))
    k_in = (1.0 - lb) * jax.nn.sigmoid(-z)
    o_b = hgrn2_chunkwise(hg_q.reshape(hg_shape), log_f, k_in, jax.nn.silu(hg_i).reshape(hg_shape))
    y_b = (rmsnorm(o_b, hg_norm_g.reshape(HG_HEADS, HG_HEAD_DIM)).reshape(B, S, HG_WIDTH)
           * jax.nn.silu(hg_g.astype(jnp.float32)))

    ml_shape = (B, S, ML_HEADS, ML_HEAD_DIM)
    qk = jax.nn.silu(causal_short_conv(ml_qk, conv_w, conv_b))
    ml_q, ml_k = jnp.split(qk, 2, axis=-1)
    gates = ml_if.astype(jnp.float32) + gate_b
    log_i = gates[..., :ML_HEADS]
    log_fc = jax.nn.log_sigmoid(gates[..., ML_HEADS:])
    h_c = mlstm_chunkwise(ml_q.reshape(ml_shape), ml_k.reshape(ml_shape), ml_v.reshape(ml_shape), log_i, log_fc)
    y_c = (rmsnorm(h_c, ml_norm_g.reshape(ML_HEADS, ML_HEAD_DIM)).reshape(B, S, ML_WIDTH)
           * jax.nn.sigmoid(ml_o.astype(jnp.float32)))

    branches = jnp.stack([y_a, y_b, y_c], axis=2)
    up = jnp.einsum('bsgm,gmd->bsgd', branches, w_branch.astype(jnp.float32))
    gate = jax.nn.sigmoid(br_g.astype(jnp.float32).reshape(B, S, N_BRANCH, D_MODEL))
    merged = jnp.sum(gate * up, axis=2)
    return (merged @ w_out.astype(jnp.float32)).astype(h.dtype)


def peer_ffn(h, w_q, keys, u, v):
    B, S, D = h.shape
    T = B * S
    tokens = h.reshape(T // PK_TOKEN_GROUP, PK_TOKEN_GROUP, D)

    def group(xb):
        G = xb.shape[0]
        q = (xb @ w_q).reshape(G, PK_HEADS, 2, PK_DKEY // 2)
        s = jnp.einsum('thpd,hpnd->thpn', q, keys, preferred_element_type=jnp.float32)
        sv, si = lax.top_k(s, PK_TOPK)
        cand = (sv[:, :, 0, :, None] + sv[:, :, 1, None, :]).reshape(G, PK_HEADS, PK_TOPK * PK_TOPK)
        cand_idx = (si[:, :, 0, :, None] * PK_NKEYS + si[:, :, 1, None, :]).reshape(G, PK_HEADS, PK_TOPK * PK_TOPK)
        top_s, pos = lax.top_k(cand, PK_TOPK)
        idx = jnp.take_along_axis(cand_idx, pos, axis=-1)
        g = jax.nn.softmax(top_s, axis=-1)
        u_sel = jnp.take(u, idx, axis=0)
        v_sel = jnp.take(v, idx, axis=0)
        a = jax.nn.gelu(jnp.einsum('td,thkd->thk', xb, u_sel, preferred_element_type=jnp.float32))
        return jnp.einsum('thk,thkd->td', g * a, v_sel.astype(jnp.float32)).astype(h.dtype)

    return lax.map(group, tokens).reshape(B, S, D)


def setup_inputs(seed: int = 0) -> dict:
    key = jax.random.key(seed)
    ks = jax.random.split(key, 24)
    f32 = jnp.float32

    def nrm(k, shape, s):
        return jax.random.normal(k, shape, f32) * s

    f_bias = jnp.broadcast_to(jnp.linspace(3.0, 6.0, ML_HEADS, dtype=f32), (DEPTH, ML_HEADS))
    ml_gate_b = jnp.concatenate([nrm(ks[9], (DEPTH, ML_HEADS), 0.1),
                                 f_bias + nrm(ks[10], (DEPTH, ML_HEADS), 0.1)], axis=-1)
    return {
        "x": nrm(ks[0], (BATCH, SEQ, D_MODEL), 1.0),
        "c": nrm(ks[1], (BATCH, D_MODEL), 1.0),
        "mod_w": nrm(ks[2], (DEPTH, D_MODEL, 6 * D_MODEL), 0.5 * D_MODEL ** -0.5),
        "mod_b": nrm(ks[3], (DEPTH, 6 * D_MODEL), 0.02),
        "norm_mix_g": 1.0 + nrm(ks[4], (DEPTH, D_MODEL), 0.02),
        "norm_ffn_g": 1.0 + nrm(ks[5], (DEPTH, D_MODEL), 0.02),
        "w_in": nrm(ks[6], (DEPTH, D_MODEL, IN_WIDTH), D_MODEL ** -0.5),
        "ml_conv_w": nrm(ks[7], (DEPTH, ML_CONV, 2 * ML_WIDTH), ML_CONV ** -0.5),
        "ml_conv_b": nrm(ks[8], (DEPTH, 2 * ML_WIDTH), 0.02),
        "ml_gate_b": ml_gate_b,
        "hg_lb_logits": nrm(ks[11], (DEPTH, HG_WIDTH), 0.1),
        "hg_norm_g": 1.0 + nrm(ks[12], (DEPTH, HG_WIDTH), 0.02),
        "ml_norm_g": 1.0 + nrm(ks[13], (DEPTH, ML_WIDTH), 0.02),
        "w_branch": nrm(ks[14], (DEPTH, N_BRANCH, BRANCH_WIDTH, D_MODEL), BRANCH_WIDTH ** -0.5),
        "w_out": nrm(ks[15], (DEPTH, D_MODEL, D_MODEL), D_MODEL ** -0.5),
        "pk_wq": nrm(ks[16], (DEPTH, D_MODEL, PK_HEADS * PK_DKEY), D_MODEL ** -0.5),
        "pk_keys": nrm(ks[17], (DEPTH, PK_HEADS, 2, PK_NKEYS, PK_DKEY // 2), (PK_DKEY // 2) ** -0.5),
        "pk_u": nrm(ks[18], (DEPTH, PK_NEXPERTS, D_MODEL), D_MODEL ** -0.5),
        "pk_v": nrm(ks[19], (DEPTH, PK_NEXPERTS, D_MODEL), PK_HEADS ** -0.5),
        "final_g": 1.0 + nrm(ks[20], (D_MODEL,), 0.02),
    }


def reference(x, c, mod_w, mod_b, norm_mix_g, norm_ffn_g, w_in, ml_conv_w, ml_conv_b, ml_gate_b,
              hg_lb_logits, hg_norm_g, ml_norm_g, w_branch, w_out, pk_wq, pk_keys, pk_u, pk_v, final_g):
    cond = jax.nn.silu(c.astype(jnp.float32))
    lb_soft = jax.nn.softmax(hg_lb_logits.astype(jnp.float32), axis=0)
    lower_bounds = jnp.cumsum(lb_soft, axis=0) - lb_soft[0]
    for l in range(DEPTH):
        mod = (cond @ mod_w[l].astype(jnp.float32) + mod_b[l]).astype(x.dtype)[:, None, :]
        shift1, scale1, gate1, shift2, scale2, gate2 = jnp.split(mod, 6, axis=-1)
        h = rmsnorm(x, norm_mix_g[l]) * (1 + scale1) + shift1
        y = hybrid_mixer(h, w_in[l], ml_conv_w[l], ml_conv_b[l], ml_gate_b[l], lower_bounds[l],
                         hg_norm_g[l], ml_norm_g[l], w_branch[l], w_out[l])
        x = x + gate1 * y
        h = rmsnorm(x, norm_ffn_g[l]) * (1 + scale2) + shift2
        x = x + gate2 * peer_ffn(h, pk_wq[l], pk_keys[l], pk_u[l], pk_v[l])
    return rmsnorm(x, final_g)
```

```python
import functools

import numpy as np
import jax
import jax.numpy as jnp
from jax import lax
from jax.experimental import pallas as pl
from jax.experimental.pallas import tpu as pltpu

F32 = jnp.float32
BF16 = jnp.bfloat16

D_MODEL = 1024
DEPTH = 4
NORM_EPS = 1e-6
MASK_NEG = -1e30
LB_FLOOR = 1e-30
SB_HEAD_DIM = 64
HEADS4 = 4
HEAD_DIM = 128
BRANCH_WIDTH = 512
ML_CONV = 4
PK_HEADS = 8
PK_NKEYS = 128
PK_HALF = 64
PK_TOPK = 16

PROJ_WIDTH = 8704
COL_BRG = 0
COL_MLQ = 3072
COL_MLK = 3584
COL_SBQ = 4096
COL_SBK = 4608
COL_SBV = 5120
COL_HGQ = 5632
COL_HGF = 6144
COL_HGI = 6656
COL_HGG = 7168
COL_MLV = 7680
COL_MLO = 8192

CHUNK = 128
VMEM_LIMIT = 56 * 1024 * 1024


def _dot(a, b):
    return jnp.dot(a, b, preferred_element_type=F32)


def _dot_nt(a, b):
    return lax.dot_general(a, b, (((1,), (1,)), ((), ())), preferred_element_type=F32)


def _dot_tn(a, b):
    return lax.dot_general(a, b, (((0,), (0,)), ((), ())), preferred_element_type=F32)


def _split3(x):
    hi = x.astype(BF16)
    r = x - hi.astype(F32)
    mid = r.astype(BF16)
    lo = (r - mid.astype(F32)).astype(BF16)
    return hi, mid, lo


def _sel_dot(m01, x):
    hi, mid, lo = _split3(x)
    return _dot(m01, hi) + _dot(m01, mid) + _dot(m01, lo)


def _dot_sel(x, m01):
    hi, mid, lo = _split3(x)
    return _dot(hi, m01) + _dot(mid, m01) + _dot(lo, m01)


def _sigmoid(x):
    return 1.0 / (1.0 + jnp.exp(-x))


def _silu(x):
    return x * _sigmoid(x)


def _log_sigmoid(x):
    return jnp.minimum(x, 0.0) - jnp.log(1.0 + jnp.exp(-jnp.abs(x)))


def _params(*sem):
    return pltpu.CompilerParams(dimension_semantics=sem, vmem_limit_bytes=VMEM_LIMIT)


def _mod_kernel(c_ref, w_ref, b_ref, o_ref):
    cond = _silu(c_ref[...])
    o_ref[...] = jnp.dot(cond, w_ref[...], preferred_element_type=F32,
                         precision=lax.Precision.HIGHEST) + b_ref[...]


def _modulation(c, mod_w, mod_b):
    depth, d, n = mod_w.shape
    bsz = c.shape[0]
    tn = d
    return pl.pallas_call(
        _mod_kernel,
        grid=(depth, n // tn),
        in_specs=[
            pl.BlockSpec((bsz, d), lambda l, j: (0, 0)),
            pl.BlockSpec((None, d, tn), lambda l, j: (l, 0, j)),
            pl.BlockSpec((None, 1, tn), lambda l, j: (l, 0, j)),
        ],
        out_specs=pl.BlockSpec((None, bsz, tn), lambda l, j: (l, 0, j)),
        out_shape=jax.ShapeDtypeStruct((depth, bsz, n), F32),
        compiler_params=_params("parallel", "parallel"),
        name="modulation",
    )(c, mod_w, mod_b.reshape(depth, 1, n))


def _rms_mod(x, g, scale, shift):
    ms = jnp.mean(x * x, axis=-1, keepdims=True)
    y = x * lax.rsqrt(ms + NORM_EPS) * g
    return y * (1.0 + scale) + shift


def _norm_proj_kernel(x_ref, g_ref, sc_ref, sh_ref, w_ref, wg_ref, proj_ref, gates_ref, h_ref):
    @pl.when(pl.program_id(1) == 0)
    def _():
        hb = _rms_mod(x_ref[...], g_ref[...], sc_ref[...], sh_ref[...]).astype(BF16)
        h_ref[...] = hb
        gates_ref[...] = _dot(hb, wg_ref[...])

    proj_ref[...] = _dot(h_ref[...], w_ref[...]).astype(BF16)


def _norm_proj(x2, g, scale, shift, w_main, w_gates, seq):
    t, d = x2.shape
    tm = min(1024, seq)
    tn = 512
    per_b = seq // tm
    n = w_main.shape[1]
    return pl.pallas_call(
        _norm_proj_kernel,
        grid=(t // tm, n // tn),
        in_specs=[
            pl.BlockSpec((tm, d), lambda i, j: (i, 0)),
            pl.BlockSpec((1, d), lambda i, j: (0, 0)),
            pl.BlockSpec((None, 1, d), lambda i, j: (i // per_b, 0, 0)),
            pl.BlockSpec((None, 1, d), lambda i, j: (i // per_b, 0, 0)),
            pl.BlockSpec((d, tn), lambda i, j: (0, j)),
            pl.BlockSpec((d, 128), lambda i, j: (0, 0)),
        ],
        out_specs=[
            pl.BlockSpec((tm, tn), lambda i, j: (i, j)),
            pl.BlockSpec((tm, 128), lambda i, j: (i, 0)),
        ],
        out_shape=[jax.ShapeDtypeStruct((t, n), BF16), jax.ShapeDtypeStruct((t, 128), F32)],
        scratch_shapes=[pltpu.VMEM((tm, d), BF16)],
        compiler_params=_params("parallel", "arbitrary"),
        name="norm_proj",
    )(x2, g.reshape(1, d), scale, shift, w_main, w_gates)


SB_TQ = 256
SB_TK = 128


def _sb_kernel(q_ref, k_ref, v_ref, o_ref):
    qi = pl.program_id(2)
    lane = lax.broadcasted_iota(jnp.int32, (1, 128), 1)
    first = lane < SB_HEAD_DIM
    q2 = q_ref[...] * (SB_HEAD_DIM ** -0.5)
    zero = jnp.zeros_like(q2)
    qh = (jnp.where(first, q2, zero), jnp.where(first, zero, q2))
    t_idx = qi * SB_TQ + lax.broadcasted_iota(jnp.int32, (SB_TQ, 1), 0)
    rs = lax.broadcasted_iota(jnp.int32, (SB_TK, SB_TK), 0)
    cs = lax.broadcasted_iota(jnp.int32, (SB_TK, SB_TK), 1)
    tri = jnp.where(rs > cs, 1.0, 0.0).astype(BF16)
    nkb = (qi + 1) * (SB_TQ // SB_TK)

    def body(it, carry):
        c0, c1, acc = carry
        kb = nkb - 1 - it
        r0 = pl.multiple_of(kb * SB_TK, SB_TK)
        ks = k_ref[pl.ds(r0, SB_TK), :]
        vs = v_ref[pl.ds(r0, SB_TK), :]
        vzero = jnp.zeros_like(vs)
        vh = (jnp.where(first, vs, vzero), jnp.where(first, vzero, vs))
        s_idx = kb * SB_TK + lane
        past = s_idx < t_idx
        cc = [c0, c1]
        for h in range(2):
            z = _dot_nt(qh[h], ks)
            ls = _log_sigmoid(z)
            lk = jnp.where(past, ls - z, 0.0)
            lk_hi = lk.astype(BF16)
            lk_lo = (lk - lk_hi.astype(F32)).astype(BF16)
            later = _dot(lk_hi, tri) + _dot(lk_lo, tri) + cc[h]
            w = jnp.where(past, jnp.exp(ls + later), 0.0)
            acc = acc + _dot(w.astype(BF16), vh[h])
            cc[h] = cc[h] + jnp.sum(lk, axis=-1, keepdims=True)
        return cc[0], cc[1], acc

    init = (jnp.zeros((SB_TQ, 1), F32), jnp.zeros((SB_TQ, 1), F32), jnp.zeros((SB_TQ, 128), F32))
    _, _, acc = lax.fori_loop(0, nkb, body, init)
    o_ref[...] = acc.astype(o_ref.dtype)


def _sb_attention(proj, bsz, seq):
    t = proj.shape[0]
    nq = seq // SB_TQ
    qb, kb, vb = COL_SBQ // 128, COL_SBK // 128, COL_SBV // 128
    return pl.pallas_call(
        _sb_kernel,
        grid=(bsz, 4, nq),
        in_specs=[
            pl.BlockSpec((SB_TQ, 128), lambda b, p, i: (b * nq + i, qb + p)),
            pl.BlockSpec((seq, 128), lambda b, p, i: (b, kb + p)),
            pl.BlockSpec((seq, 128), lambda b, p, i: (b, vb + p)),
        ],
        out_specs=pl.BlockSpec((SB_TQ, 128), lambda b, p, i: (b * nq + i, p)),
        out_shape=jax.ShapeDtypeStruct((t, BRANCH_WIDTH), BF16),
        compiler_params=_params("parallel", "parallel", "arbitrary"),
        name="sb_attention",
    )(proj, proj, proj)


def _hgrn_consts():
    n_l = CHUNK
    t = np.arange(n_l)[:, None]
    u = np.arange(n_l)[None, :]
    dq, dk, masks = [], [], []
    n = n_l
    while n >= 2:
        half = n // 2
        a_t = (t // n) * n
        mid = a_t + half - 1
        up_t = (t - a_t) >= half
        up_u = (u - (u // n) * n) >= half
        dq.append(up_t & (u > mid) & (u <= t))
        dk.append((~up_t) & (u > t) & (u <= mid))
        masks.append(((t // n) == (u // n)) & up_t & (~up_u))
        n //= 2
    masks.append(t == u)
    cm = np.concatenate(dq + dk + [u <= t, u > t], axis=0).astype(np.float32)
    return cm, np.stack(masks).astype(np.float32)


HG_LEVELS = 7


def _hgrn_kernel(lbl_ref, q_ref, f_ref, i_ref, g_ref, cm_ref, mask_ref, ng_ref, o_ref, st_ref,
                 *, layer, nchunk):
    @pl.when(pl.program_id(1) == 0)
    def _():
        st_ref[...] = jnp.zeros_like(st_ref)

    lg = lbl_ref[...]
    e = jnp.exp(lg - jnp.max(lg, axis=0, keepdims=True))
    sm = e / jnp.sum(e, axis=0, keepdims=True)
    lb = jnp.sum(sm[0:layer + 1], axis=0, keepdims=True) - sm[0:1]
    log_lb = jnp.log(jnp.maximum(lb, LB_FLOOR))
    log_1m = jnp.log(1.0 - lb)
    cm = cm_ref[...]
    nl = HG_LEVELS * CHUNK

    def chunk_body(c, carry):
        r0 = pl.multiple_of(c * CHUNK, CHUNK)
        rows = pl.ds(r0, CHUNK)
        for h in range(HEADS4):
            hs = slice(h * HEAD_DIM, (h + 1) * HEAD_DIM)
            z = f_ref[rows, hs].astype(F32)
            q = q_ref[rows, hs].astype(F32)
            iv = i_ref[rows, hs].astype(F32)
            gv = g_ref[rows, hs].astype(F32)
            a = log_lb[:, hs]
            bb = log_1m[:, hs] + _log_sigmoid(z)
            lf = jnp.maximum(a, bb) + jnp.log(1.0 + jnp.exp(-jnp.abs(a - bb)))
            kin = (1.0 - lb[:, hs]) * _sigmoid(-z)
            i_act = _silu(iv).astype(BF16)
            ee = jnp.exp(_sel_dot(cm, lf))
            scores = mask_ref[HG_LEVELS] * _dot_nt(q.astype(BF16), kin.astype(BF16))
            for l in range(HG_LEVELS):
                ql = (q * ee[l * CHUNK:(l + 1) * CHUNK]).astype(BF16)
                kl = (kin * ee[nl + l * CHUNK:nl + (l + 1) * CHUNK]).astype(BF16)
                scores = scores + mask_ref[l] * _dot_nt(ql, kl)
            eb = ee[2 * nl:2 * nl + CHUNK]
            esuf = ee[2 * nl + CHUNK:2 * nl + 2 * CHUNK]
            st = st_ref[h]
            o = _dot(scores.astype(BF16), i_act) + _dot_nt((q * eb).astype(BF16), st.astype(BF16))
            st_ref[h] = st * eb[CHUNK - 1:CHUNK, :] + _dot_tn(i_act, (kin * esuf).astype(BF16))
            ms = jnp.mean(o * o, axis=-1, keepdims=True)
            y = o * lax.rsqrt(ms + NORM_EPS) * ng_ref[:, hs] * _silu(gv)
            o_ref[rows, hs] = y.astype(o_ref.dtype)
        return carry

    lax.fori_loop(0, nchunk, chunk_body, 0)


def _hgrn(proj, lb_logits, norm_g, layer, bsz, seq):
    t = proj.shape[0]
    ts = min(512, seq)
    nt = seq // ts
    cm, masks = _hgrn_consts()
    w = BRANCH_WIDTH
    col = lambda c: (lambda b, i: (b * nt + i, c // w))
    return pl.pallas_call(
        functools.partial(_hgrn_kernel, layer=layer, nchunk=ts // CHUNK),
        grid=(bsz, nt),
        in_specs=[
            pl.BlockSpec(lb_logits.shape, lambda b, i: (0, 0)),
            pl.BlockSpec((ts, w), col(COL_HGQ)),
            pl.BlockSpec((ts, w), col(COL_HGF)),
            pl.BlockSpec((ts, w), col(COL_HGI)),
            pl.BlockSpec((ts, w), col(COL_HGG)),
            pl.BlockSpec(cm.shape, lambda b, i: (0, 0)),
            pl.BlockSpec(masks.shape, lambda b, i: (0, 0, 0)),
            pl.BlockSpec((1, w), lambda b, i: (0, 0)),
        ],
        out_specs=pl.BlockSpec((ts, w), lambda b, i: (b * nt + i, 0)),
        out_shape=jax.ShapeDtypeStruct((t, w), BF16),
        scratch_shapes=[pltpu.VMEM((HEADS4, HEAD_DIM, HEAD_DIM), F32)],
        compiler_params=_params("parallel", "arbitrary"),
        name="hgrn2",
    )(lb_logits, proj, proj, proj, proj, jnp.asarray(cm, BF16), jnp.asarray(masks, F32),
      norm_g.reshape(1, w))


def _mlstm_kernel(q_ref, k_ref, v_ref, og_ref, gc_ref, gr_ref, cw_ref, cb_ref, brow_ref, bcol_ref,
                  ng_ref, o_ref, qbuf, kbuf, qact, kact, ct_ref, m_ref, *, ts):
    nchunk = ts // CHUNK
    w = BRANCH_WIDTH

    @pl.when(pl.program_id(1) == 0)
    def _():
        qbuf[0:8, :] = jnp.zeros((8, w), F32)
        kbuf[0:8, :] = jnp.zeros((8, w), F32)
        ct_ref[...] = jnp.zeros_like(ct_ref)
        m_ref[...] = jnp.zeros_like(m_ref)

    qbuf[8:8 + ts, :] = q_ref[...].astype(F32)
    kbuf[8:8 + ts, :] = k_ref[...].astype(F32)

    rs = lax.broadcasted_iota(jnp.int32, (CHUNK, CHUNK), 0)
    cs = lax.broadcasted_iota(jnp.int32, (CHUNK, CHUNK), 1)
    causal = cs <= rs
    tri_c = jnp.where(causal, 1.0, 0.0).astype(BF16)
    tri_r = jnp.where(rs <= cs, 1.0, 0.0).astype(BF16)
    ones_v = jnp.ones((CHUNK, HEAD_DIM), F32)
    kscale = HEAD_DIM ** -0.5

    for cc in range(nchunk):
        yq = jnp.zeros((CHUNK, w), F32) + cb_ref[:, 0:w]
        yk = jnp.zeros((CHUNK, w), F32) + cb_ref[:, w:2 * w]
        for j in range(ML_CONV):
            lo = cc * CHUNK + 8 - (ML_CONV - 1 - j)
            yq = yq + cw_ref[j:j + 1, 0:w] * qbuf[lo:lo + CHUNK, :]
            yk = yk + cw_ref[j:j + 1, w:2 * w] * kbuf[lo:lo + CHUNK, :]
        qact[cc * CHUNK:(cc + 1) * CHUNK, :] = _silu(yq).astype(BF16)
        kact[cc * CHUNK:(cc + 1) * CHUNK, :] = (_silu(yk) * kscale).astype(BF16)

    def chunk_body(c, carry):
        r0 = pl.multiple_of(c * CHUNK, CHUNK)
        rows = pl.ds(r0, CHUNK)
        pre_c = gc_ref[rows, :] + brow_ref[...]
        pre_r = gr_ref[:, rows] + bcol_ref[...]
        b_c = _sel_dot(tri_c, _log_sigmoid(pre_c))
        b_r = _dot_sel(_log_sigmoid(pre_r), tri_r)
        for h in range(HEADS4):
            hs = slice(h * HEAD_DIM, (h + 1) * HEAD_DIM)
            qh = qact[rows, hs]
            kh = kact[rows, hs]
            vx = jnp.concatenate([v_ref[rows, hs].astype(F32), ones_v], axis=1)
            bcol = b_c[:, HEADS4 + h:HEADS4 + h + 1]
            icol = pre_c[:, h:h + 1]
            brow = b_r[HEADS4 + h:HEADS4 + h + 1, :]
            irow = pre_r[h:h + 1, :]
            m = m_ref[h:h + 1, 0:1]
            log_d = jnp.where(causal, bcol - brow + irow, MASK_NEG)
            log_inter = bcol + m
            m_t = jnp.maximum(log_inter, jnp.max(log_d, axis=-1, keepdims=True))
            dmat = jnp.where(causal, jnp.exp(log_d - m_t), 0.0)
            inter = jnp.exp(log_inter - m_t)
            qk = _dot_nt(qh, kh) * dmat
            ct = ct_ref[h]
            hx = _dot(qk.astype(BF16), vx.astype(BF16)) + inter * _dot(qh, ct.astype(BF16))
            hout = hx[:, 0:HEAD_DIM] / jnp.maximum(jnp.abs(hx[:, HEAD_DIM:]), jnp.exp(-m_t))
            b_last = bcol[CHUNK - 1:CHUNK, :]
            m_new = jnp.maximum(b_last + m, jnp.max(b_last - brow + irow, axis=-1, keepdims=True))
            w_col = jnp.exp(b_last - bcol + icol - m_new)
            dec = jnp.exp(b_last + m - m_new)
            ct_ref[h] = dec * ct + _dot_tn(kh, (vx * w_col).astype(BF16))
            m_ref[h:h + 1, :] = jnp.broadcast_to(m_new, (1, 128))
            ms = jnp.mean(hout * hout, axis=-1, keepdims=True)
            y = hout * lax.rsqrt(ms + NORM_EPS) * ng_ref[:, hs] * _sigmoid(og_ref[rows, hs].astype(F32))
            o_ref[rows, hs] = y.astype(o_ref.dtype)
        return carry

    lax.fori_loop(0, nchunk, chunk_body, 0)
    qbuf[0:8, :] = qbuf[ts:ts + 8, :]
    kbuf[0:8, :] = kbuf[ts:ts + 8, :]


def _mlstm(proj, gates, gates_t, conv_w, conv_b, gate_b, norm_g, bsz, seq):
    t = proj.shape[0]
    ts = min(512, seq)
    nt = seq // ts
    w = BRANCH_WIDTH
    col = lambda c: (lambda b, i: (b * nt + i, c // w))
    brow = jnp.zeros((1, 128), F32).at[0, 0:8].set(gate_b)
    bcol = gate_b.reshape(8, 1)
    return pl.pallas_call(
        functools.partial(_mlstm_kernel, ts=ts),
        grid=(bsz, nt),
        in_specs=[
            pl.BlockSpec((ts, w), col(COL_MLQ)),
            pl.BlockSpec((ts, w), col(COL_MLK)),
            pl.BlockSpec((ts, w), col(COL_MLV)),
            pl.BlockSpec((ts, w), col(COL_MLO)),
            pl.BlockSpec((ts, 128), lambda b, i: (b * nt + i, 0)),
            pl.BlockSpec((8, ts), lambda b, i: (0, b * nt + i)),
            pl.BlockSpec((ML_CONV, 2 * w), lambda b, i: (0, 0)),
            pl.BlockSpec((1, 2 * w), lambda b, i: (0, 0)),
            pl.BlockSpec((1, 128), lambda b, i: (0, 0)),
            pl.BlockSpec((8, 1), lambda b, i: (0, 0)),
            pl.BlockSpec((1, w), lambda b, i: (0, 0)),
        ],
        out_specs=pl.BlockSpec((ts, w), lambda b, i: (b * nt + i, 0)),
        out_shape=jax.ShapeDtypeStruct((t, w), BF16),
        scratch_shapes=[
            pltpu.VMEM((ts + 8, w), F32),
            pltpu.VMEM((ts + 8, w), F32),
            pltpu.VMEM((ts, w), BF16),
            pltpu.VMEM((ts, w), BF16),
            pltpu.VMEM((HEADS4, HEAD_DIM, 2 * HEAD_DIM), F32),
            pltpu.VMEM((8, 128), F32),
        ],
        compiler_params=_params("parallel", "arbitrary"),
        name="mlstm",
    )(proj, proj, proj, proj, gates, gates_t, conv_w, conv_b.reshape(1, 2 * w), brow, bcol,
      norm_g.reshape(1, w))


def _merge_kernel(x_ref, ya_ref, yb_ref, yc_ref, bg_ref, wb_ref, wo_ref, g1_ref, o_ref):
    d = D_MODEL
    merged = jnp.zeros(x_ref.shape, F32)
    for g, y_ref in enumerate((ya_ref, yb_ref, yc_ref)):
        up = _dot(y_ref[...], wb_ref[g])
        merged = merged + _sigmoid(bg_ref[:, g * d:(g + 1) * d].astype(F32)) * up
    y = _dot(merged.astype(BF16), wo_ref[...])
    o_ref[...] = x_ref[...] + g1_ref[...] * y


def _merge(x2, ya, yb, yc, proj, w_branch, w_out, gate1, seq):
    t, d = x2.shape
    tm = min(512, seq)
    per_b = seq // tm
    w = BRANCH_WIDTH
    row = lambda i: (i, 0)
    return pl.pallas_call(
        _merge_kernel,
        grid=(t // tm,),
        in_specs=[
            pl.BlockSpec((tm, d), row),
            pl.BlockSpec((tm, w), row),
            pl.BlockSpec((tm, w), row),
            pl.BlockSpec((tm, w), row),
            pl.BlockSpec((tm, 3 * d), lambda i: (i, COL_BRG // (3 * d))),
            pl.BlockSpec((3, w, d), lambda i: (0, 0, 0)),
            pl.BlockSpec((d, d), lambda i: (0, 0)),
            pl.BlockSpec((None, 1, d), lambda i: (i // per_b, 0, 0)),
        ],
        out_specs=pl.BlockSpec((tm, d), row),
        out_shape=jax.ShapeDtypeStruct((t, d), F32),
        compiler_params=_params("parallel"),
        name="merge",
    )(x2, ya, yb, yc, proj, w_branch, w_out, gate1)


def _fold_keys_kernel(keys_ref, wq_ref, o_ref):
    o_ref[...] = lax.dot_general(keys_ref[...], wq_ref[...], (((1,), (1,)), ((), ())),
                                 preferred_element_type=F32, precision=lax.Precision.HIGHEST)


def _fold_keys(pk_keys, pk_wq):
    d = pk_wq.shape[0]
    sets = 2 * PK_HEADS
    keys = pk_keys.reshape(sets, PK_NKEYS, PK_HALF)
    wq = pk_wq.reshape(d, sets, PK_HALF).transpose(1, 0, 2)
    out = pl.pallas_call(
        _fold_keys_kernel,
        grid=(sets,),
        in_specs=[
            pl.BlockSpec((None, PK_NKEYS, PK_HALF), lambda s: (s, 0, 0)),
            pl.BlockSpec((None, d, PK_HALF), lambda s: (s, 0, 0)),
        ],
        out_specs=pl.BlockSpec((None, PK_NKEYS, d), lambda s: (s, 0, 0)),
        out_shape=jax.ShapeDtypeStruct((sets, PK_NKEYS, d), F32),
        compiler_params=_params("parallel"),
        name="fold_keys",
    )(keys, wq)
    return out.reshape(sets * PK_NKEYS, d)


def _peer_scores_kernel(x_ref, g_ref, sc_ref, sh_ref, wk_ref, ht_ref, st_ref):
    h = _rms_mod(x_ref[...], g_ref[...], sc_ref[...], sh_ref[...])
    ht = h.T.astype(BF16)
    ht_ref[...] = ht
    st_ref[...] = _dot(wk_ref[...], ht)


def _peer_scores(x2, g, scale, shift, wk_t, seq):
    t, d = x2.shape
    tb = min(512, seq)
    per_b = seq // tb
    nk = wk_t.shape[0]
    return pl.pallas_call(
        _peer_scores_kernel,
        grid=(t // tb,),
        in_specs=[
            pl.BlockSpec((tb, d), lambda i: (i, 0)),
            pl.BlockSpec((1, d), lambda i: (0, 0)),
            pl.BlockSpec((None, 1, d), lambda i: (i // per_b, 0, 0)),
            pl.BlockSpec((None, 1, d), lambda i: (i // per_b, 0, 0)),
            pl.BlockSpec((nk, d), lambda i: (0, 0)),
        ],
        out_specs=[
            pl.BlockSpec((d, tb), lambda i: (0, i)),
            pl.BlockSpec((nk, tb), lambda i: (0, i)),
        ],
        out_shape=[jax.ShapeDtypeStruct((d, t), BF16), jax.ShapeDtypeStruct((nk, t), F32)],
        compiler_params=_params("parallel"),
        name="peer_scores",
    )(x2, g.reshape(1, d), scale, shift, wk_t)


NEG_INF = float("-inf")
PK_ROUNDS = PK_TOPK + 1


def _top_rows(x, rounds):
    vals = []
    for r in range(rounds):
        m = jnp.max(x, axis=0, keepdims=True)
        vals.append(m)
        if r + 1 < rounds:
            x = jnp.where(x == m, NEG_INF, x)
    return vals


def _peer_stats_kernel(st_ref, c1_ref, e1_ref, e2_ref):
    tb = st_ref.shape[-1]
    for h in range(PK_HEADS):
        s1 = st_ref[h, 0]
        s2 = st_ref[h, 1]
        v1 = _top_rows(s1, PK_ROUNDS)
        v2 = _top_rows(s2, PK_ROUNDS)
        v2_16 = jnp.concatenate(v2[0:16], axis=0)
        blocks = [v1[0] + v2_16]
        for a in range(1, 8):
            blocks.append(v1[a] + v2_16[0:8])
        blocks.append(jnp.concatenate(v1[8:16], axis=0) + v2[0])
        blocks.append(jnp.concatenate(
            [v1[0] + v2[16], v1[16] + v2[0], jnp.full((6, tb), NEG_INF, F32)], axis=0))
        cand = jnp.concatenate(blocks, axis=0)
        top = _top_rows(cand, PK_ROUNDS)
        thr = 0.5 * (top[PK_TOPK - 1] + top[PK_TOPK])
        zsum = jnp.zeros((1, tb), F32)
        for r in range(PK_TOPK):
            zsum = zsum + jnp.exp(top[r] - top[0])
        c1_ref[h] = thr - s1
        e1_ref[h] = jnp.exp(s1 - v1[0]) / zsum
        e2_ref[h] = jnp.exp(s2 - v2[0])


def _peer_stats(st):
    nk, t = st.shape
    tb = min(256, t)
    st4 = st.reshape(PK_HEADS, 2, PK_NKEYS, t)
    shp = jax.ShapeDtypeStruct((PK_HEADS, PK_NKEYS, t), F32)
    spec = pl.BlockSpec((PK_HEADS, PK_NKEYS, tb), lambda i: (0, 0, i))
    return pl.pallas_call(
        _peer_stats_kernel,
        grid=(t // tb,),
        in_specs=[pl.BlockSpec((PK_HEADS, 2, PK_NKEYS, tb), lambda i: (0, 0, 0, i))],
        out_specs=[spec, spec, spec],
        out_shape=[shp, shp, shp],
        compiler_params=_params("parallel"),
        name="peer_stats",
    )(st4)


PK_ITILE = 8
GELU_C0 = 0.7978845608028654
GELU_C1 = 0.044715


def _gelu_tanh(a):
    return 0.5 * a * (1.0 + jnp.tanh(GELU_C0 * (a + GELU_C1 * (a * a * a))))


def _peer_dense_kernel(x_ref, g2_ref, ht_ref, u_ref, vt_ref, s2_ref, c1_ref, e1_ref, e2_ref,
                       o_ref, acc_ref, p_ref):
    e = pl.program_id(1)

    @pl.when(e == 0)
    def _():
        acc_ref[...] = jnp.zeros_like(acc_ref)

    ht = ht_ref[...]
    for il in range(PK_ITILE):
        rows = slice(il * PK_NKEYS, (il + 1) * PK_NKEYS)
        a = _dot(u_ref[rows, :], ht)
        wsum = jnp.zeros(a.shape, F32)
        for h in range(PK_HEADS):
            thr = c1_ref[h, il:il + 1, :]
            e1 = e1_ref[h, il:il + 1, :]
            wsum = wsum + e1 * jnp.where(s2_ref[h] >= thr, e2_ref[h], 0.0)
        p_ref[rows, :] = (_gelu_tanh(a) * wsum).astype(BF16)
    acc_ref[...] += _dot(vt_ref[...], p_ref[...])

    @pl.when(e == pl.num_programs(1) - 1)
    def _():
        o_ref[...] = x_ref[...] + g2_ref[...] * acc_ref[...].T


def _peer_dense(x2, gate2, ht, u_bf, vt_bf, st, c1, e1, e2, seq):
    t, d = x2.shape
    tb = min(512, seq)
    per_b = seq // tb
    ne = u_bf.shape[0]
    te = PK_ITILE * PK_NKEYS
    st4 = st.reshape(PK_HEADS, 2, PK_NKEYS, t)
    return pl.pallas_call(
        _peer_dense_kernel,
        grid=(t // tb, ne // te),
        in_specs=[
            pl.BlockSpec((tb, d), lambda i, e: (i, 0)),
            pl.BlockSpec((None, 1, d), lambda i, e: (i // per_b, 0, 0)),
            pl.BlockSpec((d, tb), lambda i, e: (0, i)),
            pl.BlockSpec((te, d), lambda i, e: (e, 0)),
            pl.BlockSpec((d, te), lambda i, e: (0, e)),
            pl.BlockSpec((PK_HEADS, None, PK_NKEYS, tb), lambda i, e: (0, 1, 0, i)),
            pl.BlockSpec((PK_HEADS, PK_ITILE, tb), lambda i, e: (0, e, i)),
            pl.BlockSpec((PK_HEADS, PK_ITILE, tb), lambda i, e: (0, e, i)),
            pl.BlockSpec((PK_HEADS, PK_NKEYS, tb), lambda i, e: (0, 0, i)),
        ],
        out_specs=pl.BlockSpec((tb, d), lambda i, e: (i, 0)),
        out_shape=jax.ShapeDtypeStruct((t, d), F32),
        scratch_shapes=[pltpu.VMEM((d, tb), F32), pltpu.VMEM((te, tb), BF16)],
        compiler_params=_params("parallel", "arbitrary"),
        name="peer_dense",
    )(x2, gate2, ht, u_bf, vt_bf, st4, c1, e1, e2)


def _final_norm_kernel(x_ref, g_ref, o_ref):
    x = x_ref[...]
    ms = jnp.mean(x * x, axis=-1, keepdims=True)
    o_ref[...] = x * lax.rsqrt(ms + NORM_EPS) * g_ref[...]


def _final_norm(x2, g):
    t, d = x2.shape
    tm = min(1024, t)
    return pl.pallas_call(
        _final_norm_kernel,
        grid=(t // tm,),
        in_specs=[pl.BlockSpec((tm, d), lambda i: (i, 0)), pl.BlockSpec((1, d), lambda i: (0, 0))],
        out_specs=pl.BlockSpec((tm, d), lambda i: (i, 0)),
        out_shape=jax.ShapeDtypeStruct((t, d), F32),
        compiler_params=_params("parallel"),
        name="final_norm",
    )(x2, g.reshape(1, d))


def _permute_w_in(w_in):
    main = jnp.concatenate([w_in[..., 5640:8712], w_in[..., 3584:4608], w_in[..., 0:3584],
                            w_in[..., 4608:5632]], axis=-1).astype(BF16)
    gates = jnp.pad(w_in[..., 5632:5640], ((0, 0), (0, 0), (0, 120))).astype(BF16)
    return main, gates


def kernel(x, c, mod_w, mod_b, norm_mix_g, norm_ffn_g, w_in, ml_conv_w, ml_conv_b, ml_gate_b,
           hg_lb_logits, hg_norm_g, ml_norm_g, w_branch, w_out, pk_wq, pk_keys, pk_u, pk_v, final_g):
    bsz, seq, d = x.shape
    depth = w_in.shape[0]
    t = bsz * seq
    x2 = x.reshape(t, d)

    mod = _modulation(c, mod_w, mod_b).reshape(depth, bsz, 6, 1, d)
    w_main, w_gates = _permute_w_in(w_in)
    w_branch_bf = w_branch.astype(BF16)
    w_out_bf = w_out.astype(BF16)
    u_bf = pk_u.astype(BF16)
    vt_bf = jnp.swapaxes(pk_v, 1, 2).astype(BF16)

    for l in range(depth):
        shift1, scale1, gate1, shift2, scale2, gate2 = (mod[l, :, r] for r in range(6))
        proj, gates = _norm_proj(x2, norm_mix_g[l], scale1, shift1, w_main[l], w_gates[l], seq)
        ya = _sb_attention(proj, bsz, seq)
        yb = _hgrn(proj, hg_lb_logits, hg_norm_g[l], l, bsz, seq)
        yc = _mlstm(proj, gates, gates[:, 0:8].T, ml_conv_w[l], ml_conv_b[l], ml_gate_b[l],
                    ml_norm_g[l], bsz, seq)
        x2 = _merge(x2, ya, yb, yc, proj, w_branch_bf[l], w_out_bf[l], gate1, seq)
        wk_t = _fold_keys(pk_keys[l], pk_wq[l]).astype(BF16)
        ht, st = _peer_scores(x2, norm_ffn_g[l], scale2, shift2, wk_t, seq)
        c1, e1, e2 = _peer_stats(st)
        x2 = _peer_dense(x2, gate2, ht, u_bf[l], vt_bf[l], st, c1, e1, e2, seq)
    return _final_norm(x2, final_g).reshape(bsz, seq, d)
```

```python
import functools

import numpy as np
import jax
import jax.numpy as jnp
from jax import lax
from jax.experimental import pallas as pl
from jax.experimental.pallas import tpu as pltpu

F32 = jnp.float32
BF16 = jnp.bfloat16

D_MODEL = 1024
DEPTH = 4
NORM_EPS = 1e-6
MASK_NEG = -1e30
LB_FLOOR = 1e-30
SB_HEAD_DIM = 64
HEADS4 = 4
HEAD_DIM = 128
BRANCH_WIDTH = 512
ML_CONV = 4
PK_HEADS = 8
PK_NKEYS = 128
PK_HALF = 64
PK_TOPK = 16

PROJ_WIDTH = 8704
COL_BRG = 0
COL_MLQ = 3072
COL_MLK = 3584
COL_SBQ = 4096
COL_SBK = 4608
COL_SBV = 5120
COL_HGQ = 5632
COL_HGF = 6144
COL_HGI = 6656
COL_HGG = 7168
COL_MLV = 7680
COL_MLO = 8192

CHUNK = 128
VMEM_LIMIT = 56 * 1024 * 1024


def _dot(a, b):
    return jnp.dot(a, b, preferred_element_type=F32)


def _dot_nt(a, b):
    return lax.dot_general(a, b, (((1,), (1,)), ((), ())), preferred_element_type=F32)


def _dot_tn(a, b):
    return lax.dot_general(a, b, (((0,), (0,)), ((), ())), preferred_element_type=F32)


def _split3(x):
    hi = x.astype(BF16)
    r = x - hi.astype(F32)
    mid = r.astype(BF16)
    lo = (r - mid.astype(F32)).astype(BF16)
    return hi, mid, lo


def _sel_dot(m01, x):
    hi, mid, lo = _split3(x)
    return _dot(m01, hi) + _dot(m01, mid) + _dot(m01, lo)


def _dot_sel(x, m01):
    hi, mid, lo = _split3(x)
    return _dot(hi, m01) + _dot(mid, m01) + _dot(lo, m01)


def _sigmoid(x):
    return 1.0 / (1.0 + jnp.exp(-x))


def _silu(x):
    return x * _sigmoid(x)


def _log_sigmoid(x):
    return jnp.minimum(x, 0.0) - jnp.log(1.0 + jnp.exp(-jnp.abs(x)))


def _params(*sem):
    return pltpu.CompilerParams(dimension_semantics=sem, vmem_limit_bytes=VMEM_LIMIT)


def _mod_kernel(c_ref, w_ref, b_ref, o_ref):
    cond = _silu(c_ref[...])
    o_ref[...] = jnp.dot(cond, w_ref[...], preferred_element_type=F32,
                         precision=lax.Precision.HIGHEST) + b_ref[...]


def _modulation(c, mod_w, mod_b):
    depth, d, n = mod_w.shape
    bsz = c.shape[0]
    tn = d
    return pl.pallas_call(
        _mod_kernel,
        grid=(depth, n // tn),
        in_specs=[
            pl.BlockSpec((bsz, d), lambda l, j: (0, 0)),
            pl.BlockSpec((None, d, tn), lambda l, j: (l, 0, j)),
            pl.BlockSpec((None, 1, tn), lambda l, j: (l, 0, j)),
        ],
        out_specs=pl.BlockSpec((None, bsz, tn), lambda l, j: (l, 0, j)),
        out_shape=jax.ShapeDtypeStruct((depth, bsz, n), F32),
        compiler_params=_params("parallel", "parallel"),
        name="modulation",
    )(c, mod_w, mod_b.reshape(depth, 1, n))


def _rms_mod(x, g, scale, shift):
    ms = jnp.mean(x * x, axis=-1, keepdims=True)
    y = x * lax.rsqrt(ms + NORM_EPS) * g
    return y * (1.0 + scale) + shift


def _norm_proj_kernel(x_ref, g_ref, sc_ref, sh_ref, w_ref, wg_ref, proj_ref, gates_ref, h_ref):
    @pl.when(pl.program_id(1) == 0)
    def _():
        hb = _rms_mod(x_ref[...], g_ref[...], sc_ref[...], sh_ref[...]).astype(BF16)
        h_ref[...] = hb
        gates_ref[...] = _dot(hb, wg_ref[...])

    proj_ref[...] = _dot(h_ref[...], w_ref[...]).astype(BF16)


def _norm_proj(x2, g, scale, shift, w_main, w_gates, seq):
    t, d = x2.shape
    tm = min(1024, seq)
    tn = 512
    per_b = seq // tm
    n = w_main.shape[1]
    return pl.pallas_call(
        _norm_proj_kernel,
        grid=(t // tm, n // tn),
        in_specs=[
            pl.BlockSpec((tm, d), lambda i, j: (i, 0)),
            pl.BlockSpec((1, d), lambda i, j: (0, 0)),
            pl.BlockSpec((None, 1, d), lambda i, j: (i // per_b, 0, 0)),
            pl.BlockSpec((None, 1, d), lambda i, j: (i // per_b, 0, 0)),
            pl.BlockSpec((d, tn), lambda i, j: (0, j)),
            pl.BlockSpec((d, 128), lambda i, j: (0, 0)),
        ],
        out_specs=[
            pl.BlockSpec((tm, tn), lambda i, j: (i, j)),
            pl.BlockSpec((tm, 128), lambda i, j: (i, 0)),
        ],
        out_shape=[jax.ShapeDtypeStruct((t, n), BF16), jax.ShapeDtypeStruct((t, 128), F32)],
        scratch_shapes=[pltpu.VMEM((tm, d), BF16)],
        compiler_params=_params("parallel", "arbitrary"),
        name="norm_proj",
    )(x2, g.reshape(1, d), scale, shift, w_main, w_gates)


SB_TQ = 256
SB_TK = 128
SB_PAIRS = 2
SB_UNROLL = 2


def _sb_kernel(q_ref, k_ref, v_ref, o_ref):
    qi = pl.program_id(2)
    lane = lax.broadcasted_iota(jnp.int32, (1, 128), 1)
    first = lane < SB_HEAD_DIM
    qh = []
    for p in range(SB_PAIRS):
        q2 = q_ref[:, p * 128:(p + 1) * 128] * (SB_HEAD_DIM ** -0.5)
        zero = jnp.zeros_like(q2)
        qh.append((jnp.where(first, q2, zero), jnp.where(first, zero, q2)))
    t_idx = qi * SB_TQ + lax.broadcasted_iota(jnp.int32, (SB_TQ, 1), 0)
    rs = lax.broadcasted_iota(jnp.int32, (SB_TK, SB_TK), 0)
    cs = lax.broadcasted_iota(jnp.int32, (SB_TK, SB_TK), 1)
    tri = jnp.where(rs > cs, 1.0, 0.0).astype(BF16)
    nkb = (qi + 1) * (SB_TQ // SB_TK)

    def body(it, carry):
        cc = list(carry[0])
        acc = list(carry[1])
        for sub in range(SB_UNROLL):
            kb = nkb - 1 - (it * SB_UNROLL + sub)
            r0 = pl.multiple_of(kb * SB_TK, SB_TK)
            past = (kb * SB_TK + lane) < t_idx
            for p in range(SB_PAIRS):
                ks = k_ref[pl.ds(r0, SB_TK), p * 128:(p + 1) * 128]
                vs = v_ref[pl.ds(r0, SB_TK), p * 128:(p + 1) * 128]
                vzero = jnp.zeros_like(vs)
                vh = (jnp.where(first, vs, vzero), jnp.where(first, vzero, vs))
                for h in range(2):
                    z = _dot_nt(qh[p][h], ks)
                    ls = _log_sigmoid(z)
                    lk = jnp.where(past, ls - z, 0.0)
                    lk_hi = lk.astype(BF16)
                    lk_lo = (lk - lk_hi.astype(F32)).astype(BF16)
                    later = _dot(lk_hi, tri) + _dot(lk_lo, tri) + cc[2 * p + h]
                    w = jnp.where(past, jnp.exp(ls + later), 0.0)
                    acc[p] = acc[p] + _dot(w.astype(BF16), vh[h])
                    cc[2 * p + h] = cc[2 * p + h] + jnp.sum(lk, axis=-1, keepdims=True)
        return tuple(cc), tuple(acc)

    init = (tuple(jnp.zeros((SB_TQ, 1), F32) for _ in range(2 * SB_PAIRS)),
            tuple(jnp.zeros((SB_TQ, 128), F32) for _ in range(SB_PAIRS)))
    _, acc = lax.fori_loop(0, nkb // SB_UNROLL, body, init)
    for p in range(SB_PAIRS):
        o_ref[:, p * 128:(p + 1) * 128] = acc[p].astype(o_ref.dtype)


def _sb_attention(proj, bsz, seq):
    t = proj.shape[0]
    nq = seq // SB_TQ
    wd = 128 * SB_PAIRS
    qb, kb, vb = COL_SBQ // wd, COL_SBK // wd, COL_SBV // wd
    return pl.pallas_call(
        _sb_kernel,
        grid=(bsz, BRANCH_WIDTH // wd, nq),
        in_specs=[
            pl.BlockSpec((SB_TQ, wd), lambda b, p, i: (b * nq + i, qb + p)),
            pl.BlockSpec((seq, wd), lambda b, p, i: (b, kb + p)),
            pl.BlockSpec((seq, wd), lambda b, p, i: (b, vb + p)),
        ],
        out_specs=pl.BlockSpec((SB_TQ, wd), lambda b, p, i: (b * nq + i, p)),
        out_shape=jax.ShapeDtypeStruct((t, BRANCH_WIDTH), BF16),
        compiler_params=_params("parallel", "parallel", "arbitrary"),
        name="sb_attention",
    )(proj, proj, proj)


def _hgrn_consts():
    n_l = CHUNK
    t = np.arange(n_l)[:, None]
    u = np.arange(n_l)[None, :]
    dq, dk, masks = [], [], []
    n = n_l
    while n >= 2:
        half = n // 2
        a_t = (t // n) * n
        mid = a_t + half - 1
        up_t = (t - a_t) >= half
        up_u = (u - (u // n) * n) >= half
        dq.append(up_t & (u > mid) & (u <= t))
        dk.append((~up_t) & (u > t) & (u <= mid))
        masks.append(((t // n) == (u // n)) & up_t & (~up_u))
        n //= 2
    masks.append(t == u)
    cm = np.concatenate(dq + dk + [u <= t, u > t], axis=0).astype(np.float32)
    return cm, np.stack(masks).astype(np.float32)


HG_LEVELS = 7


def _hgrn_kernel(lbl_ref, q_ref, f_ref, i_ref, g_ref, cm_ref, mask_ref, ng_ref, o_ref, st_ref,
                 *, layer, nchunk):
    @pl.when(pl.program_id(1) == 0)
    def _():
        st_ref[...] = jnp.zeros_like(st_ref)

    lg = lbl_ref[...]
    e = jnp.exp(lg - jnp.max(lg, axis=0, keepdims=True))
    sm = e / jnp.sum(e, axis=0, keepdims=True)
    lb = jnp.sum(sm[0:layer + 1], axis=0, keepdims=True) - sm[0:1]
    log_lb = jnp.log(jnp.maximum(lb, LB_FLOOR))
    log_1m = jnp.log(1.0 - lb)
    cm = cm_ref[...]
    nl = HG_LEVELS * CHUNK

    def chunk_body(c, carry):
        r0 = pl.multiple_of(c * CHUNK, CHUNK)
        rows = pl.ds(r0, CHUNK)
        for h in range(HEADS4):
            hs = slice(h * HEAD_DIM, (h + 1) * HEAD_DIM)
            z = f_ref[rows, hs].astype(F32)
            q = q_ref[rows, hs].astype(F32)
            iv = i_ref[rows, hs].astype(F32)
            gv = g_ref[rows, hs].astype(F32)
            a = log_lb[:, hs]
            bb = log_1m[:, hs] + _log_sigmoid(z)
            lf = jnp.maximum(a, bb) + jnp.log(1.0 + jnp.exp(-jnp.abs(a - bb)))
            kin = (1.0 - lb[:, hs]) * _sigmoid(-z)
            i_act = _silu(iv).astype(BF16)
            ee = jnp.exp(_sel_dot(cm, lf))
            scores = mask_ref[HG_LEVELS] * _dot_nt(q.astype(BF16), kin.astype(BF16))
            for l in range(HG_LEVELS):
                ql = (q * ee[l * CHUNK:(l + 1) * CHUNK]).astype(BF16)
                kl = (kin * ee[nl + l * CHUNK:nl + (l + 1) * CHUNK]).astype(BF16)
                scores = scores + mask_ref[l] * _dot_nt(ql, kl)
            eb = ee[2 * nl:2 * nl + CHUNK]
            esuf = ee[2 * nl + CHUNK:2 * nl + 2 * CHUNK]
            st = st_ref[h]
            o = _dot(scores.astype(BF16), i_act) + _dot_nt((q * eb).astype(BF16), st.astype(BF16))
            st_ref[h] = st * eb[CHUNK - 1:CHUNK, :] + _dot_tn(i_act, (kin * esuf).astype(BF16))
            ms = jnp.mean(o * o, axis=-1, keepdims=True)
            y = o * lax.rsqrt(ms + NORM_EPS) * ng_ref[:, hs] * _silu(gv)
            o_ref[rows, hs] = y.astype(o_ref.dtype)
        return carry

    lax.fori_loop(0, nchunk, chunk_body, 0)


def _hgrn(proj, lb_logits, norm_g, layer, bsz, seq):
    t = proj.shape[0]
    ts = min(512, seq)
    nt = seq // ts
    cm, masks = _hgrn_consts()
    w = BRANCH_WIDTH
    col = lambda c: (lambda b, i: (b * nt + i, c // w))
    return pl.pallas_call(
        functools.partial(_hgrn_kernel, layer=layer, nchunk=ts // CHUNK),
        grid=(bsz, nt),
        in_specs=[
            pl.BlockSpec(lb_logits.shape, lambda b, i: (0, 0)),
            pl.BlockSpec((ts, w), col(COL_HGQ)),
            pl.BlockSpec((ts, w), col(COL_HGF)),
            pl.BlockSpec((ts, w), col(COL_HGI)),
            pl.BlockSpec((ts, w), col(COL_HGG)),
            pl.BlockSpec(cm.shape, lambda b, i: (0, 0)),
            pl.BlockSpec(masks.shape, lambda b, i: (0, 0, 0)),
            pl.BlockSpec((1, w), lambda b, i: (0, 0)),
        ],
        out_specs=pl.BlockSpec((ts, w), lambda b, i: (b * nt + i, 0)),
        out_shape=jax.ShapeDtypeStruct((t, w), BF16),
        scratch_shapes=[pltpu.VMEM((HEADS4, HEAD_DIM, HEAD_DIM), F32)],
        compiler_params=_params("parallel", "arbitrary"),
        name="hgrn2",
    )(lb_logits, proj, proj, proj, proj, jnp.asarray(cm, BF16), jnp.asarray(masks, F32),
      norm_g.reshape(1, w))


def _mlstm_kernel(q_ref, k_ref, v_ref, og_ref, gc_ref, gr_ref, cw_ref, cb_ref, brow_ref, bcol_ref,
                  ng_ref, o_ref, qbuf, kbuf, qact, kact, ct_ref, m_ref, *, ts):
    nchunk = ts // CHUNK
    w = BRANCH_WIDTH

    @pl.when(pl.program_id(1) == 0)
    def _():
        qbuf[0:8, :] = jnp.zeros((8, w), F32)
        kbuf[0:8, :] = jnp.zeros((8, w), F32)
        ct_ref[...] = jnp.zeros_like(ct_ref)
        m_ref[...] = jnp.zeros_like(m_ref)

    qbuf[8:8 + ts, :] = q_ref[...].astype(F32)
    kbuf[8:8 + ts, :] = k_ref[...].astype(F32)

    rs = lax.broadcasted_iota(jnp.int32, (CHUNK, CHUNK), 0)
    cs = lax.broadcasted_iota(jnp.int32, (CHUNK, CHUNK), 1)
    causal = cs <= rs
    tri_c = jnp.where(causal, 1.0, 0.0).astype(BF16)
    tri_r = jnp.where(rs <= cs, 1.0, 0.0).astype(BF16)
    ones_v = jnp.ones((CHUNK, HEAD_DIM), F32)
    kscale = HEAD_DIM ** -0.5

    for cc in range(nchunk):
        yq = jnp.zeros((CHUNK, w), F32) + cb_ref[:, 0:w]
        yk = jnp.zeros((CHUNK, w), F32) + cb_ref[:, w:2 * w]
        for j in range(ML_CONV):
            lo = cc * CHUNK + 8 - (ML_CONV - 1 - j)
            yq = yq + cw_ref[j:j + 1, 0:w] * qbuf[lo:lo + CHUNK, :]
            yk = yk + cw_ref[j:j + 1, w:2 * w] * kbuf[lo:lo + CHUNK, :]
        qact[cc * CHUNK:(cc + 1) * CHUNK, :] = _silu(yq).astype(BF16)
        kact[cc * CHUNK:(cc + 1) * CHUNK, :] = (_silu(yk) * kscale).astype(BF16)

    def chunk_body(c, carry):
        r0 = pl.multiple_of(c * CHUNK, CHUNK)
        rows = pl.ds(r0, CHUNK)
        pre_c = gc_ref[rows, :] + brow_ref[...]
        pre_r = gr_ref[:, rows] + bcol_ref[...]
        b_c = _sel_dot(tri_c, _log_sigmoid(pre_c))
        b_r = _dot_sel(_log_sigmoid(pre_r), tri_r)
        for h in range(HEADS4):
            hs = slice(h * HEAD_DIM, (h + 1) * HEAD_DIM)
            qh = qact[rows, hs]
            kh = kact[rows, hs]
            vx = jnp.concatenate([v_ref[rows, hs].astype(F32), ones_v], axis=1)
            bcol = b_c[:, HEADS4 + h:HEADS4 + h + 1]
            icol = pre_c[:, h:h + 1]
            brow = b_r[HEADS4 + h:HEADS4 + h + 1, :]
            irow = pre_r[h:h + 1, :]
            m = m_ref[h:h + 1, 0:1]
            log_d = jnp.where(causal, bcol - brow + irow, MASK_NEG)
            log_inter = bcol + m
            m_t = jnp.maximum(log_inter, jnp.max(log_d, axis=-1, keepdims=True))
            dmat = jnp.where(causal, jnp.exp(log_d - m_t), 0.0)
            inter = jnp.exp(log_inter - m_t)
            qk = _dot_nt(qh, kh) * dmat
            ct = ct_ref[h]
            hx = _dot(qk.astype(BF16), vx.astype(BF16)) + inter * _dot(qh, ct.astype(BF16))
            hout = hx[:, 0:HEAD_DIM] / jnp.maximum(jnp.abs(hx[:, HEAD_DIM:]), jnp.exp(-m_t))
            b_last = bcol[CHUNK - 1:CHUNK, :]
            m_new = jnp.maximum(b_last + m, jnp.max(b_last - brow + irow, axis=-1, keepdims=True))
            w_col = jnp.exp(b_last - bcol + icol - m_new)
            dec = jnp.exp(b_last + m - m_new)
            ct_ref[h] = dec * ct + _dot_tn(kh, (vx * w_col).astype(BF16))
            m_ref[h:h + 1, :] = jnp.broadcast_to(m_new, (1, 128))
            ms = jnp.mean(hout * hout, axis=-1, keepdims=True)
            y = hout * lax.rsqrt(ms + NORM_EPS) * ng_ref[:, hs] * _sigmoid(og_ref[rows, hs].astype(F32))
            o_ref[rows, hs] = y.astype(o_ref.dtype)
        return carry

    lax.fori_loop(0, nchunk, chunk_body, 0)
    qbuf[0:8, :] = qbuf[ts:ts + 8, :]
    kbuf[0:8, :] = kbuf[ts:ts + 8, :]


def _mlstm(proj, gates, gates_t, conv_w, conv_b, gate_b, norm_g, bsz, seq):
    t = proj.shape[0]
    ts = min(512, seq)
    nt = seq // ts
    w = BRANCH_WIDTH
    col = lambda c: (lambda b, i: (b * nt + i, c // w))
    brow = jnp.zeros((1, 128), F32).at[0, 0:8].set(gate_b)
    bcol = gate_b.reshape(8, 1)
    return pl.pallas_call(
        functools.partial(_mlstm_kernel, ts=ts),
        grid=(bsz, nt),
        in_specs=[
            pl.BlockSpec((ts, w), col(COL_MLQ)),
            pl.BlockSpec((ts, w), col(COL_MLK)),
            pl.BlockSpec((ts, w), col(COL_MLV)),
            pl.BlockSpec((ts, w), col(COL_MLO)),
            pl.BlockSpec((ts, 128), lambda b, i: (b * nt + i, 0)),
            pl.BlockSpec((8, ts), lambda b, i: (0, b * nt + i)),
            pl.BlockSpec((ML_CONV, 2 * w), lambda b, i: (0, 0)),
            pl.BlockSpec((1, 2 * w), lambda b, i: (0, 0)),
            pl.BlockSpec((1, 128), lambda b, i: (0, 0)),
            pl.BlockSpec((8, 1), lambda b, i: (0, 0)),
            pl.BlockSpec((1, w), lambda b, i: (0, 0)),
        ],
        out_specs=pl.BlockSpec((ts, w), lambda b, i: (b * nt + i, 0)),
        out_shape=jax.ShapeDtypeStruct((t, w), BF16),
        scratch_shapes=[
            pltpu.VMEM((ts + 8, w), F32),
            pltpu.VMEM((ts + 8, w), F32),
            pltpu.VMEM((ts, w), BF16),
            pltpu.VMEM((ts, w), BF16),
            pltpu.VMEM((HEADS4, HEAD_DIM, 2 * HEAD_DIM), F32),
            pltpu.VMEM((8, 128), F32),
        ],
        compiler_params=_params("parallel", "arbitrary"),
        name="mlstm",
    )(proj, proj, proj, proj, gates, gates_t, conv_w, conv_b.reshape(1, 2 * w), brow, bcol,
      norm_g.reshape(1, w))


def _merge_kernel(x_ref, ya_ref, yb_ref, yc_ref, bg_ref, wb_ref, wo_ref, g1_ref, o_ref):
    d = D_MODEL
    merged = jnp.zeros(x_ref.shape, F32)
    for g, y_ref in enumerate((ya_ref, yb_ref, yc_ref)):
        up = _dot(y_ref[...], wb_ref[g])
        merged = merged + _sigmoid(bg_ref[:, g * d:(g + 1) * d].astype(F32)) * up
    y = _dot(merged.astype(BF16), wo_ref[...])
    o_ref[...] = x_ref[...] + g1_ref[...] * y


def _merge(x2, ya, yb, yc, proj, w_branch, w_out, gate1, seq):
    t, d = x2.shape
    tm = min(512, seq)
    per_b = seq // tm
    w = BRANCH_WIDTH
    row = lambda i: (i, 0)
    return pl.pallas_call(
        _merge_kernel,
        grid=(t // tm,),
        in_specs=[
            pl.BlockSpec((tm, d), row),
            pl.BlockSpec((tm, w), row),
            pl.BlockSpec((tm, w), row),
            pl.BlockSpec((tm, w), row),
            pl.BlockSpec((tm, 3 * d), lambda i: (i, COL_BRG // (3 * d))),
            pl.BlockSpec((3, w, d), lambda i: (0, 0, 0)),
            pl.BlockSpec((d, d), lambda i: (0, 0)),
            pl.BlockSpec((None, 1, d), lambda i: (i // per_b, 0, 0)),
        ],
        out_specs=pl.BlockSpec((tm, d), row),
        out_shape=jax.ShapeDtypeStruct((t, d), F32),
        compiler_params=_params("parallel"),
        name="merge",
    )(x2, ya, yb, yc, proj, w_branch, w_out, gate1)


def _fold_keys_kernel(keys_ref, wq_ref, o_ref):
    o_ref[...] = lax.dot_general(keys_ref[...], wq_ref[...], (((1,), (1,)), ((), ())),
                                 preferred_element_type=F32, precision=lax.Precision.HIGHEST)


def _fold_keys(pk_keys, pk_wq):
    d = pk_wq.shape[0]
    sets = 2 * PK_HEADS
    keys = pk_keys.reshape(sets, PK_NKEYS, PK_HALF)
    wq = pk_wq.reshape(d, sets, PK_HALF).transpose(1, 0, 2)
    out = pl.pallas_call(
        _fold_keys_kernel,
        grid=(sets,),
        in_specs=[
            pl.BlockSpec((None, PK_NKEYS, PK_HALF), lambda s: (s, 0, 0)),
            pl.BlockSpec((None, d, PK_HALF), lambda s: (s, 0, 0)),
        ],
        out_specs=pl.BlockSpec((None, PK_NKEYS, d), lambda s: (s, 0, 0)),
        out_shape=jax.ShapeDtypeStruct((sets, PK_NKEYS, d), F32),
        compiler_params=_params("parallel"),
        name="fold_keys",
    )(keys, wq)
    return out.reshape(sets * PK_NKEYS, d)


def _peer_scores_kernel(x_ref, g_ref, sc_ref, sh_ref, wk_ref, ht_ref, st_ref):
    h = _rms_mod(x_ref[...], g_ref[...], sc_ref[...], sh_ref[...])
    ht = h.T.astype(BF16)
    ht_ref[...] = ht
    st_ref[...] = _dot(wk_ref[...], ht)


def _peer_scores(x2, g, scale, shift, wk_t, seq):
    t, d = x2.shape
    tb = min(512, seq)
    per_b = seq // tb
    nk = wk_t.shape[0]
    return pl.pallas_call(
        _peer_scores_kernel,
        grid=(t // tb,),
        in_specs=[
            pl.BlockSpec((tb, d), lambda i: (i, 0)),
            pl.BlockSpec((1, d), lambda i: (0, 0)),
            pl.BlockSpec((None, 1, d), lambda i: (i // per_b, 0, 0)),
            pl.BlockSpec((None, 1, d), lambda i: (i // per_b, 0, 0)),
            pl.BlockSpec((nk, d), lambda i: (0, 0)),
        ],
        out_specs=[
            pl.BlockSpec((d, tb), lambda i: (0, i)),
            pl.BlockSpec((nk, tb), lambda i: (0, i)),
        ],
        out_shape=[jax.ShapeDtypeStruct((d, t), BF16), jax.ShapeDtypeStruct((nk, t), F32)],
        compiler_params=_params("parallel"),
        name="peer_scores",
    )(x2, g.reshape(1, d), scale, shift, wk_t)


NEG_INF = float("-inf")
PK_ROUNDS = PK_TOPK + 1
PK_RANK_NONE = 32.0


def _top_rows(x, rounds, want_rank=False):
    vals = []
    rank = jnp.full(x.shape, PK_RANK_NONE, F32)
    for r in range(rounds):
        m = jnp.max(x, axis=0, keepdims=True)
        vals.append(m)
        eq = x == m
        if want_rank:
            rank = jnp.where(eq, float(r), rank)
        if r + 1 < rounds:
            x = jnp.where(eq, NEG_INF, x)
    return (vals, rank) if want_rank else vals


def _peer_stats_kernel(st_ref, cnt_ref, e1_ref, r2_ref, e2_ref):
    tb = st_ref.shape[-1]
    for h in range(PK_HEADS):
        s1 = st_ref[h, 0]
        s2 = st_ref[h, 1]
        v1 = _top_rows(s1, PK_ROUNDS)
        v2, rank2 = _top_rows(s2, PK_ROUNDS, want_rank=True)
        v2_16 = jnp.concatenate(v2[0:16], axis=0)
        blocks = [v1[0] + v2_16]
        for a in range(1, 8):
            blocks.append(v1[a] + v2_16[0:8])
        blocks.append(jnp.concatenate(v1[8:16], axis=0) + v2[0])
        blocks.append(jnp.concatenate(
            [v1[0] + v2[16], v1[16] + v2[0], jnp.full((6, tb), NEG_INF, F32)], axis=0))
        cand = jnp.concatenate(blocks, axis=0)
        top = _top_rows(cand, PK_ROUNDS)
        thr = 0.5 * (top[PK_TOPK - 1] + top[PK_TOPK])
        zsum = jnp.zeros((1, tb), F32)
        for r in range(PK_TOPK):
            zsum = zsum + jnp.exp(top[r] - top[0])
        need = thr - s1
        cnt = jnp.zeros(s1.shape, F32)
        for b in range(PK_TOPK):
            cnt = cnt + jnp.where(v2[b] >= need, 1.0, 0.0)
        cnt_ref[h] = cnt
        e1_ref[h] = jnp.exp(s1 - v1[0]) / zsum
        r2_ref[h] = rank2.astype(BF16)
        e2_ref[h] = jnp.exp(s2 - v2[0]).astype(BF16)


def _peer_stats(st):
    nk, t = st.shape
    tb = min(256, t)
    st4 = st.reshape(PK_HEADS, 2, PK_NKEYS, t)
    shp = lambda dt: jax.ShapeDtypeStruct((PK_HEADS, PK_NKEYS, t), dt)
    spec = pl.BlockSpec((PK_HEADS, PK_NKEYS, tb), lambda i: (0, 0, i))
    return pl.pallas_call(
        _peer_stats_kernel,
        grid=(t // tb,),
        in_specs=[pl.BlockSpec((PK_HEADS, 2, PK_NKEYS, tb), lambda i: (0, 0, 0, i))],
        out_specs=[spec, spec, spec, spec],
        out_shape=[shp(F32), shp(F32), shp(BF16), shp(BF16)],
        compiler_params=_params("parallel"),
        name="peer_stats",
    )(st4)


PK_ITILE = 8
PK_JSUB = 16
GELU_C0 = 0.7978845608028654
GELU_C1 = 0.044715


def _gelu_tanh(a):
    return 0.5 * a * (1.0 + jnp.tanh(GELU_C0 * (a + GELU_C1 * (a * a * a))))


def _peer_tile_weights(a, row0, cnt_ref, e1_ref, r2_ref, e2_ref, p_ref):
    tb = a.shape[1]
    shape = (PK_NKEYS, tb)
    zero = jnp.zeros(shape, BF16)
    for il in range(PK_ITILE):
        g = _gelu_tanh(a[il * PK_NKEYS:(il + 1) * PK_NKEYS, :].astype(BF16))
        wsum = zero
        for h in range(PK_HEADS):
            cb = jnp.broadcast_to(cnt_ref[h, row0 + il:row0 + il + 1, :], shape).astype(BF16)
            eb = jnp.broadcast_to(e1_ref[h, row0 + il:row0 + il + 1, :], shape).astype(BF16)
            wsum = wsum + eb * jnp.minimum(jnp.maximum(cb - r2_ref[h], zero), e2_ref[h])
        p_ref[il * PK_NKEYS:(il + 1) * PK_NKEYS, :] = g * wsum


def _peer_dense_kernel(x_ref, g2_ref, ht_ref, u0_ref, ua_ref, ub_ref, vt_ref, cnt_ref, e1_ref,
                       r2_ref, e2_ref, o_ref, acc_ref, a_ref, pa_ref, pb_ref):
    g = pl.program_id(1)
    te = pa_ref.shape[0]
    ht = ht_ref[...]

    @pl.when(g == 0)
    def _():
        acc_ref[...] = jnp.zeros_like(acc_ref)
        a_ref[0] = _dot(u0_ref[...], ht)

    a_even = a_ref[g % 2]
    _peer_tile_weights(a_even, 0, cnt_ref, e1_ref, r2_ref, e2_ref, pa_ref)
    a_odd = _dot(ua_ref[...], ht)
    a_ref[(g + 1) % 2] = _dot(ub_ref[...], ht)
    acc_ref[...] += _dot(vt_ref[:, 0:te], pa_ref[...])
    _peer_tile_weights(a_odd, PK_ITILE, cnt_ref, e1_ref, r2_ref, e2_ref, pb_ref)
    acc_ref[...] += _dot(vt_ref[:, te:2 * te], pb_ref[...])

    @pl.when(g == pl.num_programs(1) - 1)
    def _():
        o_ref[...] = x_ref[...] + g2_ref[...] * acc_ref[...].T


def _peer_dense(x2, gate2, ht, u_bf, vt_bf, cnt, e1, r2, e2, seq):
    t, d = x2.shape
    tb = min(512, seq)
    per_b = seq // tb
    te = PK_ITILE * PK_NKEYS
    ntile = u_bf.shape[0] // te
    u_spec = lambda f: pl.BlockSpec((te, d), lambda i, g: (f(g), 0))
    pair_i = pl.BlockSpec((PK_HEADS, 2 * PK_ITILE, tb), lambda i, g: (0, g, i))
    full_j = pl.BlockSpec((PK_HEADS, PK_NKEYS, tb), lambda i, g: (0, 0, i))
    return pl.pallas_call(
        _peer_dense_kernel,
        grid=(t // tb, ntile // 2),
        in_specs=[
            pl.BlockSpec((tb, d), lambda i, g: (i, 0)),
            pl.BlockSpec((None, 1, d), lambda i, g: (i // per_b, 0, 0)),
            pl.BlockSpec((d, tb), lambda i, g: (0, i)),
            u_spec(lambda g: 0), u_spec(lambda g: 2 * g + 1),
            u_spec(lambda g: jnp.minimum(2 * g + 2, ntile - 1)),
            pl.BlockSpec((d, 2 * te), lambda i, g: (0, g)),
            pair_i, pair_i, full_j, full_j,
        ],
        out_specs=pl.BlockSpec((tb, d), lambda i, g: (i, 0)),
        out_shape=jax.ShapeDtypeStruct((t, d), F32),
        scratch_shapes=[pltpu.VMEM((d, tb), F32), pltpu.VMEM((2, te, tb), F32),
                        pltpu.VMEM((te, tb), BF16), pltpu.VMEM((te, tb), BF16)],
        compiler_params=_params("parallel", "arbitrary"),
        name="peer_dense",
    )(x2, gate2, ht, u_bf, u_bf, u_bf, vt_bf, cnt, e1, r2, e2)


def _final_norm_kernel(x_ref, g_ref, o_ref):
    x = x_ref[...]
    ms = jnp.mean(x * x, axis=-1, keepdims=True)
    o_ref[...] = x * lax.rsqrt(ms + NORM_EPS) * g_ref[...]


def _final_norm(x2, g):
    t, d = x2.shape
    tm = min(1024, t)
    return pl.pallas_call(
        _final_norm_kernel,
        grid=(t // tm,),
        in_specs=[pl.BlockSpec((tm, d), lambda i: (i, 0)), pl.BlockSpec((1, d), lambda i: (0, 0))],
        out_specs=pl.BlockSpec((tm, d), lambda i: (i, 0)),
        out_shape=jax.ShapeDtypeStruct((t, d), F32),
        compiler_params=_params("parallel"),
        name="final_norm",
    )(x2, g.reshape(1, d))


def _permute_w_in(w_in):
    main = jnp.concatenate([w_in[..., 5640:8712], w_in[..., 3584:4608], w_in[..., 0:3584],
                            w_in[..., 4608:5632]], axis=-1).astype(BF16)
    gates = jnp.pad(w_in[..., 5632:5640], ((0, 0), (0, 0), (0, 120))).astype(BF16)
    return main, gates


def kernel(x, c, mod_w, mod_b, norm_mix_g, norm_ffn_g, w_in, ml_conv_w, ml_conv_b, ml_gate_b,
           hg_lb_logits, hg_norm_g, ml_norm_g, w_branch, w_out, pk_wq, pk_keys, pk_u, pk_v, final_g):
    bsz, seq, d = x.shape
    depth = w_in.shape[0]
    t = bsz * seq
    x2 = x.reshape(t, d)

    mod = _modulation(c, mod_w, mod_b).reshape(depth, bsz, 6, 1, d)
    w_main, w_gates = _permute_w_in(w_in)
    w_branch_bf = w_branch.astype(BF16)
    w_out_bf = w_out.astype(BF16)
    u_bf = pk_u.astype(BF16)
    vt_bf = jnp.swapaxes(pk_v, 1, 2).astype(BF16)

    for l in range(depth):
        shift1, scale1, gate1, shift2, scale2, gate2 = (mod[l, :, r] for r in range(6))
        proj, gates = _norm_proj(x2, norm_mix_g[l], scale1, shift1, w_main[l], w_gates[l], seq)
        ya = _sb_attention(proj, bsz, seq)
        yb = _hgrn(proj, hg_lb_logits, hg_norm_g[l], l, bsz, seq)
        yc = _mlstm(proj, gates, gates[:, 0:8].T, ml_conv_w[l], ml_conv_b[l], ml_gate_b[l],
                    ml_norm_g[l], bsz, seq)
        x2 = _merge(x2, ya, yb, yc, proj, w_branch_bf[l], w_out_bf[l], gate1, seq)
        wk_t = _fold_keys(pk_keys[l], pk_wq[l]).astype(BF16)
        ht, st = _peer_scores(x2, norm_ffn_g[l], scale2, shift2, wk_t, seq)
        cnt, e1, r2, e2 = _peer_stats(st)
        x2 = _peer_dense(x2, gate2, ht, u_bf[l], vt_bf[l], cnt, e1, r2, e2, seq)
    return _final_norm(x2, final_g).reshape(bsz, seq, d)
```

```python
import functools

import numpy as np
import jax
import jax.numpy as jnp
from jax import lax
from jax.experimental import pallas as pl
from jax.experimental.pallas import tpu as pltpu

F32 = jnp.float32
BF16 = jnp.bfloat16

D_MODEL = 1024
DEPTH = 4
NORM_EPS = 1e-6
MASK_NEG = -1e30
LB_FLOOR = 1e-30
SB_HEAD_DIM = 64
HEADS4 = 4
HEAD_DIM = 128
BRANCH_WIDTH = 512
ML_CONV = 4
PK_HEADS = 8
PK_NKEYS = 128
PK_HALF = 64
PK_TOPK = 16

PROJ_WIDTH = 8704
COL_BRG = 0
COL_MLQ = 3072
COL_MLK = 3584
COL_SBQ = 4096
COL_SBK = 4608
COL_SBV = 5120
COL_HGQ = 5632
COL_HGF = 6144
COL_HGI = 6656
COL_HGG = 7168
COL_MLV = 7680
COL_MLO = 8192

CHUNK = 128
LANES = 128
VMEM_LIMIT = 56 * 1024 * 1024


def _dot(a, b):
    return jnp.dot(a, b, preferred_element_type=F32)


def _dot_nt(a, b):
    return lax.dot_general(a, b, (((1,), (1,)), ((), ())), preferred_element_type=F32)


def _dot_tn(a, b):
    return lax.dot_general(a, b, (((0,), (0,)), ((), ())), preferred_element_type=F32)


def _split3(x):
    hi = x.astype(BF16)
    r = x - hi.astype(F32)
    mid = r.astype(BF16)
    lo = (r - mid.astype(F32)).astype(BF16)
    return hi, mid, lo


def _sel_dot(m01, x):
    hi, mid, lo = _split3(x)
    return _dot(m01, hi) + _dot(m01, mid) + _dot(m01, lo)


def _dot_sel(x, m01):
    hi, mid, lo = _split3(x)
    return _dot(hi, m01) + _dot(mid, m01) + _dot(lo, m01)


def _sel_dot2(m01, x):
    hi = x.astype(BF16)
    lo = (x - hi.astype(F32)).astype(BF16)
    return _dot(m01, hi) + _dot(m01, lo)


def _sigmoid(x):
    return 1.0 / (1.0 + jnp.exp(-x))


def _silu(x):
    return x * _sigmoid(x)


def _log_sigmoid(x):
    return jnp.minimum(x, 0.0) - jnp.log(1.0 + jnp.exp(-jnp.abs(x)))


def _params(*sem):
    return pltpu.CompilerParams(dimension_semantics=sem, vmem_limit_bytes=VMEM_LIMIT)


def _mod_kernel(c_ref, w_ref, b_ref, o_ref):
    cond = _silu(c_ref[...])
    o_ref[...] = jnp.dot(cond, w_ref[...], preferred_element_type=F32,
                         precision=lax.Precision.HIGHEST) + b_ref[...]


def _modulation(c, mod_w, mod_b):
    depth, d, n = mod_w.shape
    bsz = c.shape[0]
    tn = d
    return pl.pallas_call(
        _mod_kernel,
        grid=(depth, n // tn),
        in_specs=[
            pl.BlockSpec((bsz, d), lambda l, j: (0, 0)),
            pl.BlockSpec((None, d, tn), lambda l, j: (l, 0, j)),
            pl.BlockSpec((None, 1, tn), lambda l, j: (l, 0, j)),
        ],
        out_specs=pl.BlockSpec((None, bsz, tn), lambda l, j: (l, 0, j)),
        out_shape=jax.ShapeDtypeStruct((depth, bsz, n), F32),
        compiler_params=_params("parallel", "parallel"),
        name="modulation",
    )(c, mod_w, mod_b.reshape(depth, 1, n))


def _rms_mod(x, g, scale, shift):
    ms = jnp.mean(x * x, axis=-1, keepdims=True)
    y = x * lax.rsqrt(ms + NORM_EPS) * g
    return y * (1.0 + scale) + shift


def _norm_proj_kernel(x_ref, g_ref, sc_ref, sh_ref, w_ref, wg_ref, proj_ref, gates_ref, h_ref):
    @pl.when(pl.program_id(1) == 0)
    def _():
        hb = _rms_mod(x_ref[...], g_ref[...], sc_ref[...], sh_ref[...]).astype(BF16)
        h_ref[...] = hb
        gates_ref[...] = _dot(hb, wg_ref[...])

    proj_ref[...] = _dot(h_ref[...], w_ref[...]).astype(BF16)


def _norm_proj(x2, g, scale, shift, w_main, w_gates, seq):
    t, d = x2.shape
    tm = min(1024, seq)
    tn = 512
    per_b = seq // tm
    n = w_main.shape[1]
    return pl.pallas_call(
        _norm_proj_kernel,
        grid=(t // tm, n // tn),
        in_specs=[
            pl.BlockSpec((tm, d), lambda i, j: (i, 0)),
            pl.BlockSpec((1, d), lambda i, j: (0, 0)),
            pl.BlockSpec((None, 1, d), lambda i, j: (i // per_b, 0, 0)),
            pl.BlockSpec((None, 1, d), lambda i, j: (i // per_b, 0, 0)),
            pl.BlockSpec((d, tn), lambda i, j: (0, j)),
            pl.BlockSpec((d, 128), lambda i, j: (0, 0)),
        ],
        out_specs=[
            pl.BlockSpec((tm, tn), lambda i, j: (i, j)),
            pl.BlockSpec((tm, 128), lambda i, j: (i, 0)),
        ],
        out_shape=[jax.ShapeDtypeStruct((t, n), BF16), jax.ShapeDtypeStruct((t, 128), F32)],
        scratch_shapes=[pltpu.VMEM((tm, d), BF16)],
        compiler_params=_params("parallel", "arbitrary"),
        name="norm_proj",
    )(x2, g.reshape(1, d), scale, shift, w_main, w_gates)


SB_TQ = 256
SB_TK = 128
SB_PAIRS = 4
SB_UNROLL = 2
assert SB_UNROLL * SB_TK == SB_TQ
LOG2_E = 1.4426950408889634


def _sb_kernel(q_ref, k_ref, v_ref, o_ref):
    qi = pl.program_id(2)
    lane = lax.broadcasted_iota(jnp.int32, (1, 128), 1)
    first = lane < SB_HEAD_DIM
    qh = []
    for p in range(SB_PAIRS):
        q2 = q_ref[:, p * 128:(p + 1) * 128] * (SB_HEAD_DIM ** -0.5 * LOG2_E)
        zero = jnp.zeros_like(q2)
        qh.append((jnp.where(first, q2, zero), jnp.where(first, zero, q2)))
    t_idx = qi * SB_TQ + lax.broadcasted_iota(jnp.int32, (SB_TQ, 1), 0)
    rs = lax.broadcasted_iota(jnp.int32, (SB_TK, SB_TK), 0)
    cs = lax.broadcasted_iota(jnp.int32, (SB_TK, SB_TK), 1)
    tri = jnp.where(rs > cs, 1.0, 0.0).astype(BF16)
    nkb = (qi + 1) * (SB_TQ // SB_TK)

    def body(it, carry, masked):
        cc = list(carry[0])
        acc = list(carry[1])
        for sub in range(SB_UNROLL):
            kb = nkb - 1 - (it * SB_UNROLL + sub)
            r0 = pl.multiple_of(kb * SB_TK, SB_TK)
            past = (kb * SB_TK + lane) < t_idx
            for p in range(SB_PAIRS):
                ks = k_ref[pl.ds(r0, SB_TK), p * 128:(p + 1) * 128]
                vs = v_ref[pl.ds(r0, SB_TK), p * 128:(p + 1) * 128]
                vzero = jnp.zeros_like(vs)
                vh = (jnp.where(first, vs, vzero), jnp.where(first, vzero, vs))
                for h in range(2):
                    z = _dot_nt(qh[p][h], ks)
                    ls = jnp.minimum(z, 0.0) - jnp.log2(1.0 + jnp.exp2(-jnp.abs(z)))
                    lk = ls - z
                    if masked:
                        lk = jnp.where(past, lk, 0.0)
                    lk_hi = lk.astype(BF16)
                    lk_lo = (lk - lk_hi.astype(F32)).astype(BF16)
                    later = _dot(lk_hi, tri) + _dot(lk_lo, tri) + cc[2 * p + h]
                    w = jnp.exp2(ls + later)
                    if masked:
                        w = jnp.where(past, w, 0.0)
                    acc[p] = acc[p] + _dot(w.astype(BF16), vh[h])
                    cc[2 * p + h] = cc[2 * p + h] + jnp.sum(lk, axis=-1, keepdims=True)
        return tuple(cc), tuple(acc)

    init = (tuple(jnp.zeros((SB_TQ, 1), F32) for _ in range(2 * SB_PAIRS)),
            tuple(jnp.zeros((SB_TQ, 128), F32) for _ in range(SB_PAIRS)))
    carry = body(0, init, True)
    _, acc = lax.fori_loop(1, nkb // SB_UNROLL, functools.partial(body, masked=False), carry)
    for p in range(SB_PAIRS):
        o_ref[:, p * 128:(p + 1) * 128] = acc[p].astype(o_ref.dtype)


def _sb_attention(proj, bsz, seq):
    t = proj.shape[0]
    nq = seq // SB_TQ
    wd = 128 * SB_PAIRS
    qb, kb, vb = COL_SBQ // wd, COL_SBK // wd, COL_SBV // wd
    return pl.pallas_call(
        _sb_kernel,
        grid=(bsz, BRANCH_WIDTH // wd, nq),
        in_specs=[
            pl.BlockSpec((SB_TQ, wd), lambda b, p, i: (b * nq + i, qb + p)),
            pl.BlockSpec((seq, wd), lambda b, p, i: (b, kb + p)),
            pl.BlockSpec((seq, wd), lambda b, p, i: (b, vb + p)),
        ],
        out_specs=pl.BlockSpec((SB_TQ, wd), lambda b, p, i: (b * nq + i, p)),
        out_shape=jax.ShapeDtypeStruct((t, BRANCH_WIDTH), BF16),
        compiler_params=_params("parallel", "parallel", "arbitrary"),
        name="sb_attention",
    )(proj, proj, proj)


def _hgrn_consts():
    n_l = CHUNK
    t = np.arange(n_l)[:, None]
    u = np.arange(n_l)[None, :]
    dq, dk, masks = [], [], []
    n = n_l
    while n >= 2:
        half = n // 2
        a_t = (t // n) * n
        mid = a_t + half - 1
        up_t = (t - a_t) >= half
        up_u = (u - (u // n) * n) >= half
        dq.append(up_t & (u > mid) & (u <= t))
        dk.append((~up_t) & (u > t) & (u <= mid))
        masks.append(((t // n) == (u // n)) & up_t & (~up_u))
        n //= 2
    masks.append(t == u)
    cm = np.concatenate(dq + dk + [u <= t, u > t], axis=0).astype(np.float32)
    return cm, np.stack(masks).astype(np.float32)


HG_LEVELS = 7


def _hgrn_kernel(lbl_ref, q_ref, f_ref, i_ref, g_ref, cm_ref, mask_ref, ng_ref, o_ref, st_ref,
                 ee_ref, *, layer, nchunk):
    @pl.when(pl.program_id(1) == 0)
    def _():
        st_ref[...] = jnp.zeros_like(st_ref)

    lg = lbl_ref[...]
    e = jnp.exp(lg - jnp.max(lg, axis=0, keepdims=True))
    sm = e / jnp.sum(e, axis=0, keepdims=True)
    lb = jnp.sum(sm[0:layer + 1], axis=0, keepdims=True) - sm[0:1]
    log_lb = jnp.log(jnp.maximum(lb, LB_FLOOR))
    log_1m = jnp.log(1.0 - lb)
    cm = cm_ref[...]
    nl = HG_LEVELS * CHUNK

    def chunk_body(c, carry):
        r0 = pl.multiple_of(c * CHUNK, CHUNK)
        rows = pl.ds(r0, CHUNK)
        z = f_ref[rows, :].astype(F32)
        bb = log_1m + _log_sigmoid(z)
        lf = jnp.maximum(log_lb, bb) + jnp.log(1.0 + jnp.exp(-jnp.abs(log_lb - bb)))
        kin_all = (1.0 - lb) * _sigmoid(-z)
        ee_ref[...] = jnp.exp(_sel_dot2(cm, lf))
        for h in range(HEADS4):
            hs = slice(h * HEAD_DIM, (h + 1) * HEAD_DIM)
            q = q_ref[rows, hs].astype(F32)
            iv = i_ref[rows, hs].astype(F32)
            gv = g_ref[rows, hs].astype(F32)
            kin = kin_all[:, hs]
            i_act = _silu(iv).astype(BF16)
            scores = mask_ref[HG_LEVELS] * _dot_nt(q.astype(BF16), kin.astype(BF16))
            for l in range(HG_LEVELS):
                ql = (q * ee_ref[l * CHUNK:(l + 1) * CHUNK, hs]).astype(BF16)
                kl = (kin * ee_ref[nl + l * CHUNK:nl + (l + 1) * CHUNK, hs]).astype(BF16)
                scores = scores + mask_ref[l] * _dot_nt(ql, kl)
            eb = ee_ref[2 * nl:2 * nl + CHUNK, hs]
            esuf = ee_ref[2 * nl + CHUNK:2 * nl + 2 * CHUNK, hs]
            st = st_ref[h]
            o = _dot(scores.astype(BF16), i_act) + _dot_nt((q * eb).astype(BF16), st.astype(BF16))
            st_ref[h] = st * eb[CHUNK - 1:CHUNK, :] + _dot_tn(i_act, (kin * esuf).astype(BF16))
            ms = jnp.mean(o * o, axis=-1, keepdims=True)
            y = o * lax.rsqrt(ms + NORM_EPS) * ng_ref[:, hs] * _silu(gv)
            o_ref[rows, hs] = y.astype(o_ref.dtype)
        return carry

    lax.fori_loop(0, nchunk, chunk_body, 0)


def _hgrn(proj, lb_logits, norm_g, layer, bsz, seq):
    t = proj.shape[0]
    ts = min(512, seq)
    nt = seq // ts
    cm, masks = _hgrn_consts()
    w = BRANCH_WIDTH
    col = lambda c: (lambda b, i: (b * nt + i, c // w))
    return pl.pallas_call(
        functools.partial(_hgrn_kernel, layer=layer, nchunk=ts // CHUNK),
        grid=(bsz, nt),
        in_specs=[
            pl.BlockSpec(lb_logits.shape, lambda b, i: (0, 0)),
            pl.BlockSpec((ts, w), col(COL_HGQ)),
            pl.BlockSpec((ts, w), col(COL_HGF)),
            pl.BlockSpec((ts, w), col(COL_HGI)),
            pl.BlockSpec((ts, w), col(COL_HGG)),
            pl.BlockSpec(cm.shape, lambda b, i: (0, 0)),
            pl.BlockSpec(masks.shape, lambda b, i: (0, 0, 0)),
            pl.BlockSpec((1, w), lambda b, i: (0, 0)),
        ],
        out_specs=pl.BlockSpec((ts, w), lambda b, i: (b * nt + i, 0)),
        out_shape=jax.ShapeDtypeStruct((t, w), BF16),
        scratch_shapes=[pltpu.VMEM((HEADS4, HEAD_DIM, HEAD_DIM), F32),
                        pltpu.VMEM(((2 * HG_LEVELS + 2) * CHUNK, w), F32)],
        compiler_params=_params("parallel", "arbitrary"),
        name="hgrn2",
    )(lb_logits, proj, proj, proj, proj, jnp.asarray(cm, BF16), jnp.asarray(masks, F32),
      norm_g.reshape(1, w))


def _mlstm_kernel(q_ref, k_ref, v_ref, og_ref, gc_ref, gr_ref, cw_ref, cb_ref, brow_ref, bcol_ref,
                  ng_ref, o_ref, qbuf, kbuf, qact, kact, ct_ref, m_ref, *, ts):
    nchunk = ts // CHUNK
    w = BRANCH_WIDTH

    @pl.when(pl.program_id(1) == 0)
    def _():
        qbuf[0:8, :] = jnp.zeros((8, w), F32)
        kbuf[0:8, :] = jnp.zeros((8, w), F32)
        ct_ref[...] = jnp.zeros_like(ct_ref)
        m_ref[...] = jnp.zeros_like(m_ref)

    qbuf[8:8 + ts, :] = q_ref[...].astype(F32)
    kbuf[8:8 + ts, :] = k_ref[...].astype(F32)

    rs = lax.broadcasted_iota(jnp.int32, (CHUNK, CHUNK), 0)
    cs = lax.broadcasted_iota(jnp.int32, (CHUNK, CHUNK), 1)
    causal = cs <= rs
    tri_c = jnp.where(causal, 1.0, 0.0).astype(BF16)
    tri_r = jnp.where(rs <= cs, 1.0, 0.0).astype(BF16)
    ones_v = jnp.ones((CHUNK, HEAD_DIM), F32)
    kscale = HEAD_DIM ** -0.5

    for cc in range(nchunk):
        yq = jnp.zeros((CHUNK, w), F32) + cb_ref[:, 0:w]
        yk = jnp.zeros((CHUNK, w), F32) + cb_ref[:, w:2 * w]
        for j in range(ML_CONV):
            lo = cc * CHUNK + 8 - (ML_CONV - 1 - j)
            yq = yq + cw_ref[j:j + 1, 0:w] * qbuf[lo:lo + CHUNK, :]
            yk = yk + cw_ref[j:j + 1, w:2 * w] * kbuf[lo:lo + CHUNK, :]
        qact[cc * CHUNK:(cc + 1) * CHUNK, :] = _silu(yq).astype(BF16)
        kact[cc * CHUNK:(cc + 1) * CHUNK, :] = (_silu(yk) * kscale).astype(BF16)

    def chunk_body(c, carry):
        r0 = pl.multiple_of(c * CHUNK, CHUNK)
        rows = pl.ds(r0, CHUNK)
        pre_c = gc_ref[rows, :] + brow_ref[...]
        pre_r = gr_ref[:, rows] + bcol_ref[...]
        b_c = _sel_dot(tri_c, _log_sigmoid(pre_c))
        b_r = _dot_sel(_log_sigmoid(pre_r), tri_r)
        for h in range(HEADS4):
            hs = slice(h * HEAD_DIM, (h + 1) * HEAD_DIM)
            qh = qact[rows, hs]
            kh = kact[rows, hs]
            vx = jnp.concatenate([v_ref[rows, hs].astype(F32), ones_v], axis=1)
            bcol = b_c[:, HEADS4 + h:HEADS4 + h + 1]
            icol = pre_c[:, h:h + 1]
            brow = b_r[HEADS4 + h:HEADS4 + h + 1, :]
            irow = pre_r[h:h + 1, :]
            m = m_ref[h:h + 1, 0:1]
            log_d = jnp.where(causal, bcol - brow + irow, MASK_NEG)
            log_inter = bcol + m
            m_t = jnp.maximum(log_inter, jnp.max(log_d, axis=-1, keepdims=True))
            dmat = jnp.where(causal, jnp.exp(log_d - m_t), 0.0)
            inter = jnp.exp(log_inter - m_t)
            qk = _dot_nt(qh, kh) * dmat
            ct = ct_ref[h]
            hx = _dot(qk.astype(BF16), vx.astype(BF16)) + inter * _dot(qh, ct.astype(BF16))
            hout = hx[:, 0:HEAD_DIM] / jnp.maximum(jnp.abs(hx[:, HEAD_DIM:]), jnp.exp(-m_t))
            b_last = bcol[CHUNK - 1:CHUNK, :]
            m_new = jnp.maximum(b_last + m, jnp.max(b_last - brow + irow, axis=-1, keepdims=True))
            w_col = jnp.exp(b_last - bcol + icol - m_new)
            dec = jnp.exp(b_last + m - m_new)
            ct_ref[h] = dec * ct + _dot_tn(kh, (vx * w_col).astype(BF16))
            m_ref[h:h + 1, :] = jnp.broadcast_to(m_new, (1, 128))
            ms = jnp.mean(hout * hout, axis=-1, keepdims=True)
            y = hout * lax.rsqrt(ms + NORM_EPS) * ng_ref[:, hs] * _sigmoid(og_ref[rows, hs].astype(F32))
            o_ref[rows, hs] = y.astype(o_ref.dtype)
        return carry

    lax.fori_loop(0, nchunk, chunk_body, 0)
    qbuf[0:8, :] = qbuf[ts:ts + 8, :]
    kbuf[0:8, :] = kbuf[ts:ts + 8, :]


def _mlstm(proj, gates, gates_t, conv_w, conv_b, gate_b, norm_g, bsz, seq):
    t = proj.shape[0]
    ts = min(512, seq)
    nt = seq // ts
    w = BRANCH_WIDTH
    col = lambda c: (lambda b, i: (b * nt + i, c // w))
    brow = jnp.zeros((1, 128), F32).at[0, 0:8].set(gate_b)
    bcol = gate_b.reshape(8, 1)
    return pl.pallas_call(
        functools.partial(_mlstm_kernel, ts=ts),
        grid=(bsz, nt),
        in_specs=[
            pl.BlockSpec((ts, w), col(COL_MLQ)),
            pl.BlockSpec((ts, w), col(COL_MLK)),
            pl.BlockSpec((ts, w), col(COL_MLV)),
            pl.BlockSpec((ts, w), col(COL_MLO)),
            pl.BlockSpec((ts, 128), lambda b, i: (b * nt + i, 0)),
            pl.BlockSpec((8, ts), lambda b, i: (0, b * nt + i)),
            pl.BlockSpec((ML_CONV, 2 * w), lambda b, i: (0, 0)),
            pl.BlockSpec((1, 2 * w), lambda b, i: (0, 0)),
            pl.BlockSpec((1, 128), lambda b, i: (0, 0)),
            pl.BlockSpec((8, 1), lambda b, i: (0, 0)),
            pl.BlockSpec((1, w), lambda b, i: (0, 0)),
        ],
        out_specs=pl.BlockSpec((ts, w), lambda b, i: (b * nt + i, 0)),
        out_shape=jax.ShapeDtypeStruct((t, w), BF16),
        scratch_shapes=[
            pltpu.VMEM((ts + 8, w), F32),
            pltpu.VMEM((ts + 8, w), F32),
            pltpu.VMEM((ts, w), BF16),
            pltpu.VMEM((ts, w), BF16),
            pltpu.VMEM((HEADS4, HEAD_DIM, 2 * HEAD_DIM), F32),
            pltpu.VMEM((8, 128), F32),
        ],
        compiler_params=_params("parallel", "arbitrary"),
        name="mlstm",
    )(proj, proj, proj, proj, gates, gates_t, conv_w, conv_b.reshape(1, 2 * w), brow, bcol,
      norm_g.reshape(1, w))


def _merge_kernel(x_ref, ya_ref, yb_ref, yc_ref, bg_ref, wb_ref, wo_ref, g1_ref, o_ref):
    d = D_MODEL
    merged = jnp.zeros(x_ref.shape, F32)
    for g, y_ref in enumerate((ya_ref, yb_ref, yc_ref)):
        up = _dot(y_ref[...], wb_ref[g])
        merged = merged + _sigmoid(bg_ref[:, g * d:(g + 1) * d].astype(F32)) * up
    y = _dot(merged.astype(BF16), wo_ref[...])
    o_ref[...] = x_ref[...] + g1_ref[...] * y


def _merge(x2, ya, yb, yc, proj, w_branch, w_out, gate1, seq):
    t, d = x2.shape
    tm = min(512, seq)
    per_b = seq // tm
    w = BRANCH_WIDTH
    row = lambda i: (i, 0)
    return pl.pallas_call(
        _merge_kernel,
        grid=(t // tm,),
        in_specs=[
            pl.BlockSpec((tm, d), row),
            pl.BlockSpec((tm, w), row),
            pl.BlockSpec((tm, w), row),
            pl.BlockSpec((tm, w), row),
            pl.BlockSpec((tm, 3 * d), lambda i: (i, COL_BRG // (3 * d))),
            pl.BlockSpec((3, w, d), lambda i: (0, 0, 0)),
            pl.BlockSpec((d, d), lambda i: (0, 0)),
            pl.BlockSpec((None, 1, d), lambda i: (i // per_b, 0, 0)),
        ],
        out_specs=pl.BlockSpec((tm, d), row),
        out_shape=jax.ShapeDtypeStruct((t, d), F32),
        compiler_params=_params("parallel"),
        name="merge",
    )(x2, ya, yb, yc, proj, w_branch, w_out, gate1)


def _fold_keys_kernel(keys_ref, wq_ref, o_ref):
    o_ref[...] = lax.dot_general(keys_ref[...], wq_ref[...], (((1,), (1,)), ((), ())),
                                 preferred_element_type=F32, precision=lax.Precision.HIGHEST)


def _fold_keys(pk_keys, pk_wq):
    d = pk_wq.shape[0]
    sets = 2 * PK_HEADS
    keys = pk_keys.reshape(sets, PK_NKEYS, PK_HALF)
    wq = pk_wq.reshape(d, sets, PK_HALF).transpose(1, 0, 2)
    out = pl.pallas_call(
        _fold_keys_kernel,
        grid=(sets,),
        in_specs=[
            pl.BlockSpec((None, PK_NKEYS, PK_HALF), lambda s: (s, 0, 0)),
            pl.BlockSpec((None, d, PK_HALF), lambda s: (s, 0, 0)),
        ],
        out_specs=pl.BlockSpec((None, PK_NKEYS, d), lambda s: (s, 0, 0)),
        out_shape=jax.ShapeDtypeStruct((sets, PK_NKEYS, d), F32),
        compiler_params=_params("parallel"),
        name="fold_keys",
    )(keys, wq)
    return out.reshape(sets * PK_NKEYS, d)


def _peer_scores_kernel(x_ref, g_ref, sc_ref, sh_ref, wk_ref, ht_ref, st_ref):
    h = _rms_mod(x_ref[...], g_ref[...], sc_ref[...], sh_ref[...])
    ht = h.T.astype(BF16)
    ht_ref[...] = ht
    st_ref[...] = _dot(wk_ref[...], ht)


def _peer_scores(x2, g, scale, shift, wk_t, seq):
    t, d = x2.shape
    tb = min(512, seq)
    per_b = seq // tb
    nk = wk_t.shape[0]
    return pl.pallas_call(
        _peer_scores_kernel,
        grid=(t // tb,),
        in_specs=[
            pl.BlockSpec((tb, d), lambda i: (i, 0)),
            pl.BlockSpec((1, d), lambda i: (0, 0)),
            pl.BlockSpec((None, 1, d), lambda i: (i // per_b, 0, 0)),
            pl.BlockSpec((None, 1, d), lambda i: (i // per_b, 0, 0)),
            pl.BlockSpec((nk, d), lambda i: (0, 0)),
        ],
        out_specs=[
            pl.BlockSpec((d, tb), lambda i: (0, i)),
            pl.BlockSpec((nk, tb), lambda i: (0, i)),
        ],
        out_shape=[jax.ShapeDtypeStruct((d, t), BF16), jax.ShapeDtypeStruct((nk, t), F32)],
        compiler_params=_params("parallel"),
        name="peer_scores",
    )(x2, g.reshape(1, d), scale, shift, wk_t)


NEG_INF = float("-inf")
PK_ROUNDS = PK_TOPK + 1
PK_RANK_NONE = 32.0


def _top_rows(x, rounds, want_rank=False):
    vals = []
    rank = jnp.full(x.shape, PK_RANK_NONE, F32)
    for r in range(rounds):
        m = jnp.max(x, axis=0, keepdims=True)
        vals.append(m)
        eq = x == m
        if want_rank:
            rank = jnp.where(eq, float(r), rank)
        if r + 1 < rounds:
            x = jnp.where(eq, NEG_INF, x)
    return (vals, rank) if want_rank else vals


def _peer_stats_kernel(st_ref, cnt_ref, e1_ref, r2_ref, e2_ref):
    tb = st_ref.shape[-1]
    for h in range(PK_HEADS):
        s1 = st_ref[h, 0]
        s2 = st_ref[h, 1]
        v1 = _top_rows(s1, PK_ROUNDS)
        v2, rank2 = _top_rows(s2, PK_ROUNDS, want_rank=True)
        v2_16 = jnp.concatenate(v2[0:16], axis=0)
        blocks = [v1[0] + v2_16]
        for a in range(1, 8):
            blocks.append(v1[a] + v2_16[0:8])
        blocks.append(jnp.concatenate(v1[8:16], axis=0) + v2[0])
        blocks.append(jnp.concatenate(
            [v1[0] + v2[16], v1[16] + v2[0], jnp.full((6, tb), NEG_INF, F32)], axis=0))
        cand = jnp.concatenate(blocks, axis=0)
        top = _top_rows(cand, PK_ROUNDS)
        thr = 0.5 * (top[PK_TOPK - 1] + top[PK_TOPK])
        zsum = jnp.zeros((1, tb), F32)
        for r in range(PK_TOPK):
            zsum = zsum + jnp.exp(top[r] - top[0])
        need = thr - s1
        cnt = jnp.zeros(s1.shape, F32)
        for b in range(PK_TOPK):
            cnt = cnt + jnp.where(v2[b] >= need, 1.0, 0.0)
        cnt_ref[h] = cnt
        e1_ref[h] = jnp.exp(s1 - v1[0]) / zsum
        r2_ref[h] = rank2.astype(BF16)
        e2_ref[h] = jnp.exp(s2 - v2[0]).astype(BF16)


def _peer_stats(st):
    nk, t = st.shape
    tb = min(256, t)
    st4 = st.reshape(PK_HEADS, 2, PK_NKEYS, t)
    shp = lambda dt: jax.ShapeDtypeStruct((PK_HEADS, PK_NKEYS, t), dt)
    spec = pl.BlockSpec((PK_HEADS, PK_NKEYS, tb), lambda i: (0, 0, i))
    return pl.pallas_call(
        _peer_stats_kernel,
        grid=(t // tb,),
        in_specs=[pl.BlockSpec((PK_HEADS, 2, PK_NKEYS, tb), lambda i: (0, 0, 0, i))],
        out_specs=[spec, spec, spec, spec],
        out_shape=[shp(F32), shp(F32), shp(BF16), shp(BF16)],
        compiler_params=_params("parallel"),
        name="peer_stats",
    )(st4)


PK_ITILE = 8
PK_DENSE_TB = 512
GELU_C0 = 0.7978845608028654
GELU_C1 = 0.044715


def _gelu_tanh(a):
    return 0.5 * a * (1.0 + jnp.tanh(GELU_C0 * (a + GELU_C1 * (a * a * a))))


def _peer_tile_weights(a, row0, cnt_ref, e1_ref, r2_ref, e2_ref, p_ref):
    tb = a.shape[1]
    shape = (PK_NKEYS, tb)
    zero = jnp.zeros(shape, BF16)
    for il in range(PK_ITILE):
        g = _gelu_tanh(a[il * PK_NKEYS:(il + 1) * PK_NKEYS, :].astype(BF16))
        wsum = zero
        for h in range(PK_HEADS):
            cb = jnp.broadcast_to(cnt_ref[h, row0 + il:row0 + il + 1, :], shape).astype(BF16)
            eb = jnp.broadcast_to(e1_ref[h, row0 + il:row0 + il + 1, :], shape).astype(BF16)
            wsum = wsum + eb * jnp.minimum(jnp.maximum(cb - r2_ref[h], zero), e2_ref[h])
        p_ref[il * PK_NKEYS:(il + 1) * PK_NKEYS, :] = g * wsum


def _peer_dense_kernel(x_ref, g2_ref, ht_ref, u0_ref, ua_ref, ub_ref, vt_ref, cnt_ref, e1_ref,
                       r2_ref, e2_ref, o_ref, acc_ref, a_ref, pa_ref, pb_ref):
    g = pl.program_id(1)
    te = pa_ref.shape[0]
    ht = ht_ref[...]

    @pl.when(g == 0)
    def _():
        acc_ref[...] = jnp.zeros_like(acc_ref)
        a_ref[0] = _dot(u0_ref[...], ht)

    a_even = a_ref[g % 2]
    _peer_tile_weights(a_even, 0, cnt_ref, e1_ref, r2_ref, e2_ref, pa_ref)
    a_odd = _dot(ua_ref[...], ht)
    a_ref[(g + 1) % 2] = _dot(ub_ref[...], ht)
    acc_ref[...] += _dot(vt_ref[:, 0:te], pa_ref[...])
    _peer_tile_weights(a_odd, PK_ITILE, cnt_ref, e1_ref, r2_ref, e2_ref, pb_ref)
    acc_ref[...] += _dot(vt_ref[:, te:2 * te], pb_ref[...])

    @pl.when(g == pl.num_programs(1) - 1)
    def _():
        o_ref[...] = x_ref[...] + g2_ref[...] * acc_ref[...].T


def _peer_dense(x2, gate2, ht, u_bf, vt_bf, cnt, e1, r2, e2, seq):
    t, d = x2.shape
    tb = min(PK_DENSE_TB, seq)
    per_b = seq // tb
    te = PK_ITILE * PK_NKEYS
    ntile = u_bf.shape[0] // te
    u_spec = lambda f: pl.BlockSpec((te, d), lambda i, g: (f(g), 0))
    pair_i = pl.BlockSpec((PK_HEADS, 2 * PK_ITILE, tb), lambda i, g: (0, g, i))
    full_j = pl.BlockSpec((PK_HEADS, PK_NKEYS, tb), lambda i, g: (0, 0, i))
    return pl.pallas_call(
        _peer_dense_kernel,
        grid=(t // tb, ntile // 2),
        in_specs=[
            pl.BlockSpec((tb, d), lambda i, g: (i, 0)),
            pl.BlockSpec((None, 1, d), lambda i, g: (i // per_b, 0, 0)),
            pl.BlockSpec((d, tb), lambda i, g: (0, i)),
            u_spec(lambda g: 0), u_spec(lambda g: 2 * g + 1),
            u_spec(lambda g: jnp.minimum(2 * g + 2, ntile - 1)),
            pl.BlockSpec((d, 2 * te), lambda i, g: (0, g)),
            pair_i, pair_i, full_j, full_j,
        ],
        out_specs=pl.BlockSpec((tb, d), lambda i, g: (i, 0)),
        out_shape=jax.ShapeDtypeStruct((t, d), F32),
        scratch_shapes=[pltpu.VMEM((d, tb), F32), pltpu.VMEM((2, te, tb), F32),
                        pltpu.VMEM((te, tb), BF16), pltpu.VMEM((te, tb), BF16)],
        compiler_params=_params("parallel", "arbitrary"),
        name="peer_dense",
    )(x2, gate2, ht, u_bf, u_bf, u_bf, vt_bf, cnt, e1, r2, e2)


def _final_norm_kernel(x_ref, g_ref, o_ref):
    x = x_ref[...]
    ms = jnp.mean(x * x, axis=-1, keepdims=True)
    o_ref[...] = x * lax.rsqrt(ms + NORM_EPS) * g_ref[...]


def _final_norm(x2, g):
    t, d = x2.shape
    tm = min(1024, t)
    return pl.pallas_call(
        _final_norm_kernel,
        grid=(t // tm,),
        in_specs=[pl.BlockSpec((tm, d), lambda i: (i, 0)), pl.BlockSpec((1, d), lambda i: (0, 0))],
        out_specs=pl.BlockSpec((tm, d), lambda i: (i, 0)),
        out_shape=jax.ShapeDtypeStruct((t, d), F32),
        compiler_params=_params("parallel"),
        name="final_norm",
    )(x2, g.reshape(1, d))


def _permute_w_in(w_in):
    main = jnp.concatenate([w_in[..., 5640:8712], w_in[..., 3584:4608], w_in[..., 0:3584],
                            w_in[..., 4608:5632]], axis=-1).astype(BF16)
    gates = jnp.pad(w_in[..., 5632:5640], ((0, 0), (0, 0), (0, 120))).astype(BF16)
    return main, gates


def kernel(x, c, mod_w, mod_b, norm_mix_g, norm_ffn_g, w_in, ml_conv_w, ml_conv_b, ml_gate_b,
           hg_lb_logits, hg_norm_g, ml_norm_g, w_branch, w_out, pk_wq, pk_keys, pk_u, pk_v, final_g):
    bsz, seq, d = x.shape
    depth = w_in.shape[0]
    t = bsz * seq
    x2 = x.reshape(t, d)

    mod = _modulation(c, mod_w, mod_b).reshape(depth, bsz, 6, 1, d)
    w_main, w_gates = _permute_w_in(w_in)
    w_branch_bf = w_branch.astype(BF16)
    w_out_bf = w_out.astype(BF16)
    u_bf = pk_u.astype(BF16)
    vt_bf = jnp.swapaxes(pk_v, 1, 2).astype(BF16)

    for l in range(depth):
        shift1, scale1, gate1, shift2, scale2, gate2 = (mod[l, :, r] for r in range(6))
        proj, gates = _norm_proj(x2, norm_mix_g[l], scale1, shift1, w_main[l], w_gates[l], seq)
        ya = _sb_attention(proj, bsz, seq)
        yb = _hgrn(proj, hg_lb_logits, hg_norm_g[l], l, bsz, seq)
        yc = _mlstm(proj, gates, gates[:, 0:8].T, ml_conv_w[l], ml_conv_b[l], ml_gate_b[l],
                    ml_norm_g[l], bsz, seq)
        x2 = _merge(x2, ya, yb, yc, proj, w_branch_bf[l], w_out_bf[l], gate1, seq)
        wk_t = _fold_keys(pk_keys[l], pk_wq[l]).astype(BF16)
        ht, st = _peer_scores(x2, norm_ffn_g[l], scale2, shift2, wk_t, seq)
        cnt, e1, r2, e2 = _peer_stats(st)
        x2 = _peer_dense(x2, gate2, ht, u_bf[l], vt_bf[l], cnt, e1, r2, e2, seq)
    return _final_norm(x2, final_g).reshape(bsz, seq, d)
```

```python
import functools

import numpy as np
import jax
import jax.numpy as jnp
from jax import lax
from jax.experimental import pallas as pl
from jax.experimental.pallas import tpu as pltpu

F32 = jnp.float32
BF16 = jnp.bfloat16

D_MODEL = 1024
DEPTH = 4
NORM_EPS = 1e-6
MASK_NEG = -1e30
LB_FLOOR = 1e-30
SB_HEAD_DIM = 64
HEADS4 = 4
HEAD_DIM = 128
BRANCH_WIDTH = 512
ML_CONV = 4
PK_HEADS = 8
PK_NKEYS = 128
PK_HALF = 64
PK_TOPK = 16

PROJ_WIDTH = 8704
COL_BRG = 0
COL_MLQ = 3072
COL_MLK = 3584
COL_SBQ = 4096
COL_SBK = 4608
COL_SBV = 5120
COL_HGQ = 5632
COL_HGF = 6144
COL_HGI = 6656
COL_HGG = 7168
COL_MLV = 7680
COL_MLO = 8192

CHUNK = 128
LANES = 128
VMEM_LIMIT = 56 * 1024 * 1024


def _dot(a, b):
    return jnp.dot(a, b, preferred_element_type=F32)


def _dot_nt(a, b):
    return lax.dot_general(a, b, (((1,), (1,)), ((), ())), preferred_element_type=F32)


def _dot_tn(a, b):
    return lax.dot_general(a, b, (((0,), (0,)), ((), ())), preferred_element_type=F32)


def _split3(x):
    hi = x.astype(BF16)
    r = x - hi.astype(F32)
    mid = r.astype(BF16)
    lo = (r - mid.astype(F32)).astype(BF16)
    return hi, mid, lo


def _sel_dot(m01, x):
    hi, mid, lo = _split3(x)
    return _dot(m01, hi) + _dot(m01, mid) + _dot(m01, lo)


def _dot_sel(x, m01):
    hi, mid, lo = _split3(x)
    return _dot(hi, m01) + _dot(mid, m01) + _dot(lo, m01)


def _sel_dot2(m01, x):
    hi = x.astype(BF16)
    lo = (x - hi.astype(F32)).astype(BF16)
    return _dot(m01, hi) + _dot(m01, lo)


def _sigmoid(x):
    return 1.0 / (1.0 + jnp.exp(-x))


def _silu(x):
    return x * _sigmoid(x)


def _log_sigmoid(x):
    return jnp.minimum(x, 0.0) - jnp.log(1.0 + jnp.exp(-jnp.abs(x)))


def _params(*sem):
    return pltpu.CompilerParams(dimension_semantics=sem, vmem_limit_bytes=VMEM_LIMIT)


def _mod_kernel(c_ref, w_ref, b_ref, o_ref):
    cond = _silu(c_ref[...])
    o_ref[...] = jnp.dot(cond, w_ref[...], preferred_element_type=F32,
                         precision=lax.Precision.HIGHEST) + b_ref[...]


def _modulation(c, mod_w, mod_b):
    depth, d, n = mod_w.shape
    bsz = c.shape[0]
    tn = d
    return pl.pallas_call(
        _mod_kernel,
        grid=(depth, n // tn),
        in_specs=[
            pl.BlockSpec((bsz, d), lambda l, j: (0, 0)),
            pl.BlockSpec((None, d, tn), lambda l, j: (l, 0, j)),
            pl.BlockSpec((None, 1, tn), lambda l, j: (l, 0, j)),
        ],
        out_specs=pl.BlockSpec((None, bsz, tn), lambda l, j: (l, 0, j)),
        out_shape=jax.ShapeDtypeStruct((depth, bsz, n), F32),
        compiler_params=_params("parallel", "parallel"),
        name="modulation",
    )(c, mod_w, mod_b.reshape(depth, 1, n))


def _rms_mod(x, g, scale, shift):
    ms = jnp.mean(x * x, axis=-1, keepdims=True)
    y = x * lax.rsqrt(ms + NORM_EPS) * g
    return y * (1.0 + scale) + shift


def _norm_proj_kernel(x_ref, g_ref, sc_ref, sh_ref, w_ref, wg_ref, proj_ref, gates_ref, h_ref):
    @pl.when(pl.program_id(1) == 0)
    def _():
        hb = _rms_mod(x_ref[...], g_ref[...], sc_ref[...], sh_ref[...]).astype(BF16)
        h_ref[...] = hb
        gates_ref[...] = _dot(hb, wg_ref[...])

    proj_ref[...] = _dot(h_ref[...], w_ref[...]).astype(BF16)


def _norm_proj(x2, g, scale, shift, w_main, w_gates, seq):
    t, d = x2.shape
    tm = min(1024, seq)
    tn = PROJ_WIDTH // 4
    per_b = seq // tm
    n = w_main.shape[1]
    return pl.pallas_call(
        _norm_proj_kernel,
        grid=(t // tm, n // tn),
        in_specs=[
            pl.BlockSpec((tm, d), lambda i, j: (i, 0)),
            pl.BlockSpec((1, d), lambda i, j: (0, 0)),
            pl.BlockSpec((None, 1, d), lambda i, j: (i // per_b, 0, 0)),
            pl.BlockSpec((None, 1, d), lambda i, j: (i // per_b, 0, 0)),
            pl.BlockSpec((d, tn), lambda i, j: (0, j)),
            pl.BlockSpec((d, 128), lambda i, j: (0, 0)),
        ],
        out_specs=[
            pl.BlockSpec((tm, tn), lambda i, j: (i, j)),
            pl.BlockSpec((tm, 128), lambda i, j: (i, 0)),
        ],
        out_shape=[jax.ShapeDtypeStruct((t, n), BF16), jax.ShapeDtypeStruct((t, 128), F32)],
        scratch_shapes=[pltpu.VMEM((tm, d), BF16)],
        compiler_params=_params("parallel", "arbitrary"),
        name="norm_proj",
    )(x2, g.reshape(1, d), scale, shift, w_main, w_gates)


SB_TQ = 256
SB_TK = 128
SB_PAIRS = 4
SB_UNROLL = 2
assert SB_UNROLL * SB_TK == SB_TQ
LOG2_E = 1.4426950408889634


def _sb_kernel(q_ref, k_ref, v_ref, o_ref):
    qi = pl.program_id(2)
    lane = lax.broadcasted_iota(jnp.int32, (1, 128), 1)
    first = lane < SB_HEAD_DIM
    qh = []
    for p in range(SB_PAIRS):
        q2 = q_ref[:, p * 128:(p + 1) * 128] * (SB_HEAD_DIM ** -0.5 * LOG2_E)
        zero = jnp.zeros_like(q2)
        qh.append((jnp.where(first, q2, zero), jnp.where(first, zero, q2)))
    t_idx = qi * SB_TQ + lax.broadcasted_iota(jnp.int32, (SB_TQ, 1), 0)
    rs = lax.broadcasted_iota(jnp.int32, (SB_TK, SB_TK), 0)
    cs = lax.broadcasted_iota(jnp.int32, (SB_TK, SB_TK), 1)
    tri = jnp.where(rs > cs, 1.0, 0.0).astype(BF16)
    nkb = (qi + 1) * (SB_TQ // SB_TK)

    def body(it, carry, masked):
        cc = list(carry[0])
        acc = list(carry[1])
        for sub in range(SB_UNROLL):
            kb = nkb - 1 - (it * SB_UNROLL + sub)
            r0 = pl.multiple_of(kb * SB_TK, SB_TK)
            past = (kb * SB_TK + lane) < t_idx
            for p in range(SB_PAIRS):
                ks = k_ref[pl.ds(r0, SB_TK), p * 128:(p + 1) * 128]
                vs = v_ref[pl.ds(r0, SB_TK), p * 128:(p + 1) * 128]
                vzero = jnp.zeros_like(vs)
                vh = (jnp.where(first, vs, vzero), jnp.where(first, vzero, vs))
                for h in range(2):
                    z = _dot_nt(qh[p][h], ks)
                    ls = jnp.minimum(z, 0.0) - jnp.log2(1.0 + jnp.exp2(-jnp.abs(z)))
                    lk = ls - z
                    if masked:
                        lk = jnp.where(past, lk, 0.0)
                    lk_hi = lk.astype(BF16)
                    lk_lo = (lk - lk_hi.astype(F32)).astype(BF16)
                    later = _dot(lk_hi, tri) + _dot(lk_lo, tri) + cc[2 * p + h]
                    w = jnp.exp2(ls + later)
                    if masked:
                        w = jnp.where(past, w, 0.0)
                    acc[p] = acc[p] + _dot(w.astype(BF16), vh[h])
                    cc[2 * p + h] = cc[2 * p + h] + jnp.sum(lk, axis=-1, keepdims=True)
        return tuple(cc), tuple(acc)

    init = (tuple(jnp.zeros((SB_TQ, 1), F32) for _ in range(2 * SB_PAIRS)),
            tuple(jnp.zeros((SB_TQ, 128), F32) for _ in range(SB_PAIRS)))
    carry = body(0, init, True)
    _, acc = lax.fori_loop(1, nkb // SB_UNROLL, functools.partial(body, masked=False), carry)
    for p in range(SB_PAIRS):
        o_ref[:, p * 128:(p + 1) * 128] = acc[p].astype(o_ref.dtype)


def _sb_attention(proj, bsz, seq):
    t = proj.shape[0]
    nq = seq // SB_TQ
    wd = 128 * SB_PAIRS
    qb, kb, vb = COL_SBQ // wd, COL_SBK // wd, COL_SBV // wd
    return pl.pallas_call(
        _sb_kernel,
        grid=(bsz, BRANCH_WIDTH // wd, nq),
        in_specs=[
            pl.BlockSpec((SB_TQ, wd), lambda b, p, i: (b * nq + i, qb + p)),
            pl.BlockSpec((seq, wd), lambda b, p, i: (b, kb + p)),
            pl.BlockSpec((seq, wd), lambda b, p, i: (b, vb + p)),
        ],
        out_specs=pl.BlockSpec((SB_TQ, wd), lambda b, p, i: (b * nq + i, p)),
        out_shape=jax.ShapeDtypeStruct((t, BRANCH_WIDTH), BF16),
        compiler_params=_params("parallel", "parallel", "arbitrary"),
        name="sb_attention",
    )(proj, proj, proj)


def _hgrn_consts():
    n_l = CHUNK
    t = np.arange(n_l)[:, None]
    u = np.arange(n_l)[None, :]
    dq, dk, masks = [], [], []
    n = n_l
    while n >= 2:
        half = n // 2
        a_t = (t // n) * n
        mid = a_t + half - 1
        up_t = (t - a_t) >= half
        up_u = (u - (u // n) * n) >= half
        dq.append(up_t & (u > mid) & (u <= t))
        dk.append((~up_t) & (u > t) & (u <= mid))
        masks.append(((t // n) == (u // n)) & up_t & (~up_u))
        n //= 2
    masks.append(t == u)
    cm = np.concatenate(dq + dk + [u <= t, u > t], axis=0).astype(np.float32)
    return cm, np.stack(masks).astype(np.float32)


HG_LEVELS = 7


def _hgrn_kernel(lbl_ref, q_ref, f_ref, i_ref, g_ref, cm_ref, mask_ref, ng_ref, o_ref, st_ref,
                 ee_ref, *, layer, nchunk):
    @pl.when(pl.program_id(1) == 0)
    def _():
        st_ref[...] = jnp.zeros_like(st_ref)

    lg = lbl_ref[...]
    e = jnp.exp(lg - jnp.max(lg, axis=0, keepdims=True))
    sm = e / jnp.sum(e, axis=0, keepdims=True)
    lb = jnp.sum(sm[0:layer + 1], axis=0, keepdims=True) - sm[0:1]
    log_lb = jnp.log(jnp.maximum(lb, LB_FLOOR))
    log_1m = jnp.log(1.0 - lb)
    cm = cm_ref[...]
    nl = HG_LEVELS * CHUNK

    def chunk_body(c, carry):
        r0 = pl.multiple_of(c * CHUNK, CHUNK)
        rows = pl.ds(r0, CHUNK)
        z = f_ref[rows, :].astype(F32)
        bb = log_1m + _log_sigmoid(z)
        lf = jnp.maximum(log_lb, bb) + jnp.log(1.0 + jnp.exp(-jnp.abs(log_lb - bb)))
        kin_all = (1.0 - lb) * _sigmoid(-z)
        ee_ref[...] = jnp.exp(_sel_dot2(cm, lf))
        for h in range(HEADS4):
            hs = slice(h * HEAD_DIM, (h + 1) * HEAD_DIM)
            q = q_ref[rows, hs].astype(F32)
            iv = i_ref[rows, hs].astype(F32)
            gv = g_ref[rows, hs].astype(F32)
            kin = kin_all[:, hs]
            i_act = _silu(iv).astype(BF16)
            scores = mask_ref[HG_LEVELS] * _dot_nt(q.astype(BF16), kin.astype(BF16))
            for l in range(HG_LEVELS):
                ql = (q * ee_ref[l * CHUNK:(l + 1) * CHUNK, hs]).astype(BF16)
                kl = (kin * ee_ref[nl + l * CHUNK:nl + (l + 1) * CHUNK, hs]).astype(BF16)
                scores = scores + mask_ref[l] * _dot_nt(ql, kl)
            eb = ee_ref[2 * nl:2 * nl + CHUNK, hs]
            esuf = ee_ref[2 * nl + CHUNK:2 * nl + 2 * CHUNK, hs]
            st = st_ref[h]
            o = _dot(scores.astype(BF16), i_act) + _dot_nt((q * eb).astype(BF16), st.astype(BF16))
            st_ref[h] = st * eb[CHUNK - 1:CHUNK, :] + _dot_tn(i_act, (kin * esuf).astype(BF16))
            ms = jnp.mean(o * o, axis=-1, keepdims=True)
            y = o * lax.rsqrt(ms + NORM_EPS) * ng_ref[:, hs] * _silu(gv)
            o_ref[rows, hs] = y.astype(o_ref.dtype)
        return carry

    lax.fori_loop(0, nchunk, chunk_body, 0)


def _hgrn(proj, lb_logits, norm_g, layer, bsz, seq):
    t = proj.shape[0]
    ts = min(512, seq)
    nt = seq // ts
    cm, masks = _hgrn_consts()
    w = BRANCH_WIDTH
    col = lambda c: (lambda b, i: (b * nt + i, c // w))
    return pl.pallas_call(
        functools.partial(_hgrn_kernel, layer=layer, nchunk=ts // CHUNK),
        grid=(bsz, nt),
        in_specs=[
            pl.BlockSpec(lb_logits.shape, lambda b, i: (0, 0)),
            pl.BlockSpec((ts, w), col(COL_HGQ)),
            pl.BlockSpec((ts, w), col(COL_HGF)),
            pl.BlockSpec((ts, w), col(COL_HGI)),
            pl.BlockSpec((ts, w), col(COL_HGG)),
            pl.BlockSpec(cm.shape, lambda b, i: (0, 0)),
            pl.BlockSpec(masks.shape, lambda b, i: (0, 0, 0)),
            pl.BlockSpec((1, w), lambda b, i: (0, 0)),
        ],
        out_specs=pl.BlockSpec((ts, w), lambda b, i: (b * nt + i, 0)),
        out_shape=jax.ShapeDtypeStruct((t, w), BF16),
        scratch_shapes=[pltpu.VMEM((HEADS4, HEAD_DIM, HEAD_DIM), F32),
                        pltpu.VMEM(((2 * HG_LEVELS + 2) * CHUNK, w), F32)],
        compiler_params=_params("parallel", "arbitrary"),
        name="hgrn2",
    )(lb_logits, proj, proj, proj, proj, jnp.asarray(cm, BF16), jnp.asarray(masks, F32),
      norm_g.reshape(1, w))


def _mlstm_kernel(q_ref, k_ref, v_ref, og_ref, gc_ref, gr_ref, cw_ref, cb_ref, brow_ref, bcol_ref,
                  ng_ref, o_ref, qbuf, kbuf, qact, kact, ct_ref, m_ref, *, ts):
    nchunk = ts // CHUNK
    w = BRANCH_WIDTH

    @pl.when(pl.program_id(1) == 0)
    def _():
        qbuf[0:8, :] = jnp.zeros((8, w), F32)
        kbuf[0:8, :] = jnp.zeros((8, w), F32)
        ct_ref[...] = jnp.zeros_like(ct_ref)
        m_ref[...] = jnp.zeros_like(m_ref)

    qbuf[8:8 + ts, :] = q_ref[...].astype(F32)
    kbuf[8:8 + ts, :] = k_ref[...].astype(F32)

    rs = lax.broadcasted_iota(jnp.int32, (CHUNK, CHUNK), 0)
    cs = lax.broadcasted_iota(jnp.int32, (CHUNK, CHUNK), 1)
    causal = cs <= rs
    tri_c = jnp.where(causal, 1.0, 0.0).astype(BF16)
    tri_r = jnp.where(rs <= cs, 1.0, 0.0).astype(BF16)
    ones_v = jnp.ones((CHUNK, HEAD_DIM), F32)
    kscale = HEAD_DIM ** -0.5

    for cc in range(nchunk):
        yq = jnp.zeros((CHUNK, w), F32) + cb_ref[:, 0:w]
        yk = jnp.zeros((CHUNK, w), F32) + cb_ref[:, w:2 * w]
        for j in range(ML_CONV):
            lo = cc * CHUNK + 8 - (ML_CONV - 1 - j)
            yq = yq + cw_ref[j:j + 1, 0:w] * qbuf[lo:lo + CHUNK, :]
            yk = yk + cw_ref[j:j + 1, w:2 * w] * kbuf[lo:lo + CHUNK, :]
        qact[cc * CHUNK:(cc + 1) * CHUNK, :] = _silu(yq).astype(BF16)
        kact[cc * CHUNK:(cc + 1) * CHUNK, :] = (_silu(yk) * kscale).astype(BF16)

    def chunk_body(c, carry):
        r0 = pl.multiple_of(c * CHUNK, CHUNK)
        rows = pl.ds(r0, CHUNK)
        pre_c = gc_ref[rows, :] + brow_ref[...]
        pre_r = gr_ref[:, rows] + bcol_ref[...]
        b_c = _sel_dot(tri_c, _log_sigmoid(pre_c))
        b_r = _dot_sel(_log_sigmoid(pre_r), tri_r)
        for h in range(HEADS4):
            hs = slice(h * HEAD_DIM, (h + 1) * HEAD_DIM)
            qh = qact[rows, hs]
            kh = kact[rows, hs]
            vx = jnp.concatenate([v_ref[rows, hs].astype(F32), ones_v], axis=1)
            bcol = b_c[:, HEADS4 + h:HEADS4 + h + 1]
            icol = pre_c[:, h:h + 1]
            brow = b_r[HEADS4 + h:HEADS4 + h + 1, :]
            irow = pre_r[h:h + 1, :]
            m = m_ref[h:h + 1, 0:1]
            log_d = jnp.where(causal, bcol - brow + irow, MASK_NEG)
            log_inter = bcol + m
            m_t = jnp.maximum(log_inter, jnp.max(log_d, axis=-1, keepdims=True))
            dmat = jnp.where(causal, jnp.exp(log_d - m_t), 0.0)
            inter = jnp.exp(log_inter - m_t)
            qk = _dot_nt(qh, kh) * dmat
            ct = ct_ref[h]
            hx = _dot(qk.astype(BF16), vx.astype(BF16)) + inter * _dot(qh, ct.astype(BF16))
            hout = hx[:, 0:HEAD_DIM] / jnp.maximum(jnp.abs(hx[:, HEAD_DIM:]), jnp.exp(-m_t))
            b_last = bcol[CHUNK - 1:CHUNK, :]
            m_new = jnp.maximum(b_last + m, jnp.max(b_last - brow + irow, axis=-1, keepdims=True))
            w_col = jnp.exp(b_last - bcol + icol - m_new)
            dec = jnp.exp(b_last + m - m_new)
            ct_ref[h] = dec * ct + _dot_tn(kh, (vx * w_col).astype(BF16))
            m_ref[h:h + 1, :] = jnp.broadcast_to(m_new, (1, 128))
            ms = jnp.mean(hout * hout, axis=-1, keepdims=True)
            y = hout * lax.rsqrt(ms + NORM_EPS) * ng_ref[:, hs] * _sigmoid(og_ref[rows, hs].astype(F32))
            o_ref[rows, hs] = y.astype(o_ref.dtype)
        return carry

    lax.fori_loop(0, nchunk, chunk_body, 0)
    qbuf[0:8, :] = qbuf[ts:ts + 8, :]
    kbuf[0:8, :] = kbuf[ts:ts + 8, :]


def _mlstm(proj, gates, gates_t, conv_w, conv_b, gate_b, norm_g, bsz, seq):
    t = proj.shape[0]
    ts = min(512, seq)
    nt = seq // ts
    w = BRANCH_WIDTH
    col = lambda c: (lambda b, i: (b * nt + i, c // w))
    brow = jnp.zeros((1, 128), F32).at[0, 0:8].set(gate_b)
    bcol = gate_b.reshape(8, 1)
    return pl.pallas_call(
        functools.partial(_mlstm_kernel, ts=ts),
        grid=(bsz, nt),
        in_specs=[
            pl.BlockSpec((ts, w), col(COL_MLQ)),
            pl.BlockSpec((ts, w), col(COL_MLK)),
            pl.BlockSpec((ts, w), col(COL_MLV)),
            pl.BlockSpec((ts, w), col(COL_MLO)),
            pl.BlockSpec((ts, 128), lambda b, i: (b * nt + i, 0)),
            pl.BlockSpec((8, ts), lambda b, i: (0, b * nt + i)),
            pl.BlockSpec((ML_CONV, 2 * w), lambda b, i: (0, 0)),
            pl.BlockSpec((1, 2 * w), lambda b, i: (0, 0)),
            pl.BlockSpec((1, 128), lambda b, i: (0, 0)),
            pl.BlockSpec((8, 1), lambda b, i: (0, 0)),
            pl.BlockSpec((1, w), lambda b, i: (0, 0)),
        ],
        out_specs=pl.BlockSpec((ts, w), lambda b, i: (b * nt + i, 0)),
        out_shape=jax.ShapeDtypeStruct((t, w), BF16),
        scratch_shapes=[
            pltpu.VMEM((ts + 8, w), F32),
            pltpu.VMEM((ts + 8, w), F32),
            pltpu.VMEM((ts, w), BF16),
            pltpu.VMEM((ts, w), BF16),
            pltpu.VMEM((HEADS4, HEAD_DIM, 2 * HEAD_DIM), F32),
            pltpu.VMEM((8, 128), F32),
        ],
        compiler_params=_params("parallel", "arbitrary"),
        name="mlstm",
    )(proj, proj, proj, proj, gates, gates_t, conv_w, conv_b.reshape(1, 2 * w), brow, bcol,
      norm_g.reshape(1, w))


def _merge_kernel(x_ref, ya_ref, yb_ref, yc_ref, bg_ref, wb_ref, wo_ref, g1_ref, o_ref):
    d = D_MODEL
    merged = jnp.zeros(x_ref.shape, F32)
    for g, y_ref in enumerate((ya_ref, yb_ref, yc_ref)):
        up = _dot(y_ref[...], wb_ref[g])
        merged = merged + _sigmoid(bg_ref[:, g * d:(g + 1) * d].astype(F32)) * up
    y = _dot(merged.astype(BF16), wo_ref[...])
    o_ref[...] = x_ref[...] + g1_ref[...] * y


def _merge(x2, ya, yb, yc, proj, w_branch, w_out, gate1, seq):
    t, d = x2.shape
    tm = min(512, seq)
    per_b = seq // tm
    w = BRANCH_WIDTH
    row = lambda i: (i, 0)
    return pl.pallas_call(
        _merge_kernel,
        grid=(t // tm,),
        in_specs=[
            pl.BlockSpec((tm, d), row),
            pl.BlockSpec((tm, w), row),
            pl.BlockSpec((tm, w), row),
            pl.BlockSpec((tm, w), row),
            pl.BlockSpec((tm, 3 * d), lambda i: (i, COL_BRG // (3 * d))),
            pl.BlockSpec((3, w, d), lambda i: (0, 0, 0)),
            pl.BlockSpec((d, d), lambda i: (0, 0)),
            pl.BlockSpec((None, 1, d), lambda i: (i // per_b, 0, 0)),
        ],
        out_specs=pl.BlockSpec((tm, d), row),
        out_shape=jax.ShapeDtypeStruct((t, d), F32),
        compiler_params=_params("parallel"),
        name="merge",
    )(x2, ya, yb, yc, proj, w_branch, w_out, gate1)


def _fold_keys_kernel(keys_ref, wq_ref, o_ref):
    o_ref[...] = lax.dot_general(keys_ref[...], wq_ref[...], (((1,), (1,)), ((), ())),
                                 preferred_element_type=F32, precision=lax.Precision.HIGHEST)


def _fold_keys(pk_keys, pk_wq):
    d = pk_wq.shape[0]
    sets = 2 * PK_HEADS
    keys = pk_keys.reshape(sets, PK_NKEYS, PK_HALF)
    wq = pk_wq.reshape(d, sets, PK_HALF).transpose(1, 0, 2)
    out = pl.pallas_call(
        _fold_keys_kernel,
        grid=(sets,),
        in_specs=[
            pl.BlockSpec((None, PK_NKEYS, PK_HALF), lambda s: (s, 0, 0)),
            pl.BlockSpec((None, d, PK_HALF), lambda s: (s, 0, 0)),
        ],
        out_specs=pl.BlockSpec((None, PK_NKEYS, d), lambda s: (s, 0, 0)),
        out_shape=jax.ShapeDtypeStruct((sets, PK_NKEYS, d), F32),
        compiler_params=_params("parallel"),
        name="fold_keys",
    )(keys, wq)
    return out.reshape(sets * PK_NKEYS, d)


def _peer_scores_kernel(x_ref, g_ref, sc_ref, sh_ref, wk_ref, ht_ref, st_ref):
    h = _rms_mod(x_ref[...], g_ref[...], sc_ref[...], sh_ref[...])
    ht = h.T.astype(BF16)
    ht_ref[...] = ht
    st_ref[...] = _dot(wk_ref[...], ht)


def _peer_scores(x2, g, scale, shift, wk_t, seq):
    t, d = x2.shape
    tb = min(512, seq)
    per_b = seq // tb
    nk = wk_t.shape[0]
    return pl.pallas_call(
        _peer_scores_kernel,
        grid=(t // tb,),
        in_specs=[
            pl.BlockSpec((tb, d), lambda i: (i, 0)),
            pl.BlockSpec((1, d), lambda i: (0, 0)),
            pl.BlockSpec((None, 1, d), lambda i: (i // per_b, 0, 0)),
            pl.BlockSpec((None, 1, d), lambda i: (i // per_b, 0, 0)),
            pl.BlockSpec((nk, d), lambda i: (0, 0)),
        ],
        out_specs=[
            pl.BlockSpec((d, tb), lambda i: (0, i)),
            pl.BlockSpec((nk, tb), lambda i: (0, i)),
        ],
        out_shape=[jax.ShapeDtypeStruct((d, t), BF16), jax.ShapeDtypeStruct((nk, t), F32)],
        compiler_params=_params("parallel"),
        name="peer_scores",
    )(x2, g.reshape(1, d), scale, shift, wk_t)


NEG_INF = float("-inf")
PK_ROUNDS = PK_TOPK + 1
PK_RANK_NONE = 32.0


def _top_rows(x, rounds, want_rank=False):
    vals = []
    rank = jnp.full(x.shape, PK_RANK_NONE, F32)
    for r in range(rounds):
        m = jnp.max(x, axis=0, keepdims=True)
        vals.append(m)
        eq = x == m
        if want_rank:
            rank = jnp.where(eq, float(r), rank)
        if r + 1 < rounds:
            x = jnp.where(eq, NEG_INF, x)
    return (vals, rank) if want_rank else vals


def _peer_stats_kernel(st_ref, cnt_ref, e1_ref, r2_ref, e2_ref):
    tb = st_ref.shape[-1]
    for h in range(PK_HEADS):
        s1 = st_ref[h, 0]
        s2 = st_ref[h, 1]
        v1 = _top_rows(s1, PK_ROUNDS)
        v2, rank2 = _top_rows(s2, PK_ROUNDS, want_rank=True)
        v2_16 = jnp.concatenate(v2[0:16], axis=0)
        blocks = [v1[0] + v2_16]
        for a in range(1, 8):
            blocks.append(v1[a] + v2_16[0:8])
        blocks.append(jnp.concatenate(v1[8:16], axis=0) + v2[0])
        blocks.append(jnp.concatenate(
            [v1[0] + v2[16], v1[16] + v2[0], jnp.full((6, tb), NEG_INF, F32)], axis=0))
        cand = jnp.concatenate(blocks, axis=0)
        top = _top_rows(cand, PK_ROUNDS)
        thr = 0.5 * (top[PK_TOPK - 1] + top[PK_TOPK])
        zsum = jnp.zeros((1, tb), F32)
        for r in range(PK_TOPK):
            zsum = zsum + jnp.exp(top[r] - top[0])
        need = thr - s1
        cnt = jnp.zeros(s1.shape, F32)
        for b in range(PK_TOPK):
            cnt = cnt + jnp.where(v2[b] >= need, 1.0, 0.0)
        cnt_ref[h] = cnt
        e1_ref[h] = jnp.exp(s1 - v1[0]) / zsum
        r2_ref[h] = rank2.astype(BF16)
        e2_ref[h] = jnp.exp(s2 - v2[0]).astype(BF16)


def _peer_stats(st):
    nk, t = st.shape
    tb = min(256, t)
    st4 = st.reshape(PK_HEADS, 2, PK_NKEYS, t)
    shp = lambda dt: jax.ShapeDtypeStruct((PK_HEADS, PK_NKEYS, t), dt)
    spec = pl.BlockSpec((PK_HEADS, PK_NKEYS, tb), lambda i: (0, 0, i))
    return pl.pallas_call(
        _peer_stats_kernel,
        grid=(t // tb,),
        in_specs=[pl.BlockSpec((PK_HEADS, 2, PK_NKEYS, tb), lambda i: (0, 0, 0, i))],
        out_specs=[spec, spec, spec, spec],
        out_shape=[shp(F32), shp(F32), shp(BF16), shp(BF16)],
        compiler_params=_params("parallel"),
        name="peer_stats",
    )(st4)


PK_ITILE = 8
PK_DENSE_TB = 512
GELU_C0 = 0.7978845608028654
GELU_C1 = 0.044715


def _gelu_tanh(a):
    return 0.5 * a * (1.0 + jnp.tanh(GELU_C0 * (a + GELU_C1 * (a * a * a))))


def _peer_tile_weights(a, row0, cnt_ref, e1_ref, r2_ref, e2_ref, p_ref):
    tb = a.shape[1]
    shape = (PK_NKEYS, tb)
    zero = jnp.zeros(shape, BF16)
    for il in range(PK_ITILE):
        g = _gelu_tanh(a[il * PK_NKEYS:(il + 1) * PK_NKEYS, :].astype(BF16))
        wsum = zero
        for h in range(PK_HEADS):
            cb = jnp.broadcast_to(cnt_ref[h, row0 + il:row0 + il + 1, :], shape).astype(BF16)
            eb = jnp.broadcast_to(e1_ref[h, row0 + il:row0 + il + 1, :], shape).astype(BF16)
            wsum = wsum + eb * jnp.minimum(jnp.maximum(cb - r2_ref[h], zero), e2_ref[h])
        p_ref[il * PK_NKEYS:(il + 1) * PK_NKEYS, :] = g * wsum


def _peer_dense_kernel(x_ref, g2_ref, ht_ref, u0_ref, ua_ref, ub_ref, vt_ref, cnt_ref, e1_ref,
                       r2_ref, e2_ref, o_ref, acc_ref, a_ref, pa_ref, pb_ref):
    g = pl.program_id(1)
    te = pa_ref.shape[0]
    ht = ht_ref[...]

    @pl.when(g == 0)
    def _():
        acc_ref[...] = jnp.zeros_like(acc_ref)
        a_ref[0] = _dot(u0_ref[...], ht)

    _peer_tile_weights(a_ref.at[g % 2], 0, cnt_ref, e1_ref, r2_ref, e2_ref, pa_ref)
    a_odd = _dot(ua_ref[...], ht)
    acc_ref[...] += _dot(vt_ref[0], pa_ref[...])
    _peer_tile_weights(a_odd, PK_ITILE, cnt_ref, e1_ref, r2_ref, e2_ref, pb_ref)
    acc_ref[...] += _dot(vt_ref[1], pb_ref[...])
    a_ref[(g + 1) % 2] = _dot(ub_ref[...], ht)

    @pl.when(g == pl.num_programs(1) - 1)
    def _():
        o_ref[...] = x_ref[...] + g2_ref[...] * acc_ref[...].T


def _peer_dense(x2, gate2, ht, u_bf, vt_bf, cnt, e1, r2, e2, seq):
    t, d = x2.shape
    tb = min(PK_DENSE_TB, seq)
    per_b = seq // tb
    te = PK_ITILE * PK_NKEYS
    ntile = u_bf.shape[0] // te
    u_spec = lambda f: pl.BlockSpec((te, d), lambda i, g: (f(g), 0))
    pair_i = pl.BlockSpec((PK_HEADS, 2 * PK_ITILE, tb), lambda i, g: (0, g, i))
    full_j = pl.BlockSpec((PK_HEADS, PK_NKEYS, tb), lambda i, g: (0, 0, i))
    return pl.pallas_call(
        _peer_dense_kernel,
        grid=(t // tb, ntile // 2),
        in_specs=[
            pl.BlockSpec((tb, d), lambda i, g: (i, 0)),
            pl.BlockSpec((None, 1, d), lambda i, g: (i // per_b, 0, 0)),
            pl.BlockSpec((d, tb), lambda i, g: (0, i)),
            u_spec(lambda g: 0), u_spec(lambda g: 2 * g + 1),
            u_spec(lambda g: jnp.minimum(2 * g + 2, ntile - 1)),
            pl.BlockSpec((2, d, te), lambda i, g: (g, 0, 0)),
            pair_i, pair_i, full_j, full_j,
        ],
        out_specs=pl.BlockSpec((tb, d), lambda i, g: (i, 0)),
        out_shape=jax.ShapeDtypeStruct((t, d), F32),
        scratch_shapes=[pltpu.VMEM((d, tb), F32), pltpu.VMEM((2, te, tb), F32),
                        pltpu.VMEM((te, tb), BF16), pltpu.VMEM((te, tb), BF16)],
        compiler_params=_params("parallel", "arbitrary"),
        name="peer_dense",
    )(x2, gate2, ht, u_bf, u_bf, u_bf, vt_bf, cnt, e1, r2, e2)


def _final_norm_kernel(x_ref, g_ref, o_ref):
    x = x_ref[...]
    ms = jnp.mean(x * x, axis=-1, keepdims=True)
    o_ref[...] = x * lax.rsqrt(ms + NORM_EPS) * g_ref[...]


def _final_norm(x2, g):
    t, d = x2.shape
    tm = min(1024, t)
    return pl.pallas_call(
        _final_norm_kernel,
        grid=(t // tm,),
        in_specs=[pl.BlockSpec((tm, d), lambda i: (i, 0)), pl.BlockSpec((1, d), lambda i: (0, 0))],
        out_specs=pl.BlockSpec((tm, d), lambda i: (i, 0)),
        out_shape=jax.ShapeDtypeStruct((t, d), F32),
        compiler_params=_params("parallel"),
        name="final_norm",
    )(x2, g.reshape(1, d))


def _permute_w_in(w_in):
    main = jnp.concatenate([w_in[..., 5640:8712], w_in[..., 3584:4608], w_in[..., 0:3584],
                            w_in[..., 4608:5632]], axis=-1).astype(BF16)
    gates = jnp.pad(w_in[..., 5632:5640], ((0, 0), (0, 0), (0, 120))).astype(BF16)
    return main, gates


def kernel(x, c, mod_w, mod_b, norm_mix_g, norm_ffn_g, w_in, ml_conv_w, ml_conv_b, ml_gate_b,
           hg_lb_logits, hg_norm_g, ml_norm_g, w_branch, w_out, pk_wq, pk_keys, pk_u, pk_v, final_g):
    bsz, seq, d = x.shape
    depth = w_in.shape[0]
    t = bsz * seq
    x2 = x.reshape(t, d)

    mod = _modulation(c, mod_w, mod_b).reshape(depth, bsz, 6, 1, d)
    w_main, w_gates = _permute_w_in(w_in)
    w_branch_bf = w_branch.astype(BF16)
    w_out_bf = w_out.astype(BF16)
    u_bf = pk_u.astype(BF16)
    te = PK_ITILE * PK_NKEYS
    vt_bf = jnp.swapaxes(pk_v.reshape(depth, pk_v.shape[1] // te, te, d), 2, 3).astype(BF16)

    for l in range(depth):
        shift1, scale1, gate1, shift2, scale2, gate2 = (mod[l, :, r] for r in range(6))
        proj, gates = _norm_proj(x2, norm_mix_g[l], scale1, shift1, w_main[l], w_gates[l], seq)
        ya = _sb_attention(proj, bsz, seq)
        yb = _hgrn(proj, hg_lb_logits, hg_norm_g[l], l, bsz, seq)
        yc = _mlstm(proj, gates, gates[:, 0:8].T, ml_conv_w[l], ml_conv_b[l], ml_gate_b[l],
                    ml_norm_g[l], bsz, seq)
        x2 = _merge(x2, ya, yb, yc, proj, w_branch_bf[l], w_out_bf[l], gate1, seq)
        wk_t = _fold_keys(pk_keys[l], pk_wq[l]).astype(BF16)
        ht, st = _peer_scores(x2, norm_ffn_g[l], scale2, shift2, wk_t, seq)
        cnt, e1, r2, e2 = _peer_stats(st)
        x2 = _peer_dense(x2, gate2, ht, u_bf[l], vt_bf[l], cnt, e1, r2, e2, seq)
    return _final_norm(x2, final_g).reshape(bsz, seq, d)
```

```python
import functools

import numpy as np
import jax
import jax.numpy as jnp
from jax import lax
from jax.experimental import pallas as pl
from jax.experimental.pallas import tpu as pltpu

F32 = jnp.float32
BF16 = jnp.bfloat16

D_MODEL = 1024
DEPTH = 4
NORM_EPS = 1e-6
MASK_NEG = -1e30
LB_FLOOR = 1e-30
SB_HEAD_DIM = 64
HEADS4 = 4
HEAD_DIM = 128
BRANCH_WIDTH = 512
ML_CONV = 4
PK_HEADS = 8
PK_NKEYS = 128
PK_HALF = 64
PK_TOPK = 16

PROJ_WIDTH = 8704
COL_BRG = 0
COL_MLQ = 3072
COL_MLK = 3584
COL_SBQ = 4096
COL_SBK = 4608
COL_SBV = 5120
COL_HGQ = 5632
COL_HGF = 6144
COL_HGI = 6656
COL_HGG = 7168
COL_MLV = 7680
COL_MLO = 8192

CHUNK = 128
LANES = 128
VMEM_LIMIT = 56 * 1024 * 1024


def _dot(a, b):
    return jnp.dot(a, b, preferred_element_type=F32)


def _dot_nt(a, b):
    return lax.dot_general(a, b, (((1,), (1,)), ((), ())), preferred_element_type=F32)


def _dot_tn(a, b):
    return lax.dot_general(a, b, (((0,), (0,)), ((), ())), preferred_element_type=F32)


def _split3(x):
    hi = x.astype(BF16)
    r = x - hi.astype(F32)
    mid = r.astype(BF16)
    lo = (r - mid.astype(F32)).astype(BF16)
    return hi, mid, lo


def _sel_dot(m01, x):
    hi, mid, lo = _split3(x)
    return _dot(m01, hi) + _dot(m01, mid) + _dot(m01, lo)


def _dot_sel(x, m01):
    hi, mid, lo = _split3(x)
    return _dot(hi, m01) + _dot(mid, m01) + _dot(lo, m01)


def _sel_dot2(m01, x):
    hi = x.astype(BF16)
    lo = (x - hi.astype(F32)).astype(BF16)
    return _dot(m01, hi) + _dot(m01, lo)


def _sigmoid(x):
    return 1.0 / (1.0 + jnp.exp(-x))


def _silu(x):
    return x * _sigmoid(x)


def _log_sigmoid(x):
    return jnp.minimum(x, 0.0) - jnp.log(1.0 + jnp.exp(-jnp.abs(x)))


def _params(*sem):
    return pltpu.CompilerParams(dimension_semantics=sem, vmem_limit_bytes=VMEM_LIMIT)


def _mod_kernel(c_ref, w_ref, b_ref, o_ref):
    cond = _silu(c_ref[...])
    o_ref[...] = jnp.dot(cond, w_ref[...], preferred_element_type=F32,
                         precision=lax.Precision.HIGHEST) + b_ref[...]


def _modulation(c, mod_w, mod_b):
    depth, d, n = mod_w.shape
    bsz = c.shape[0]
    tn = d
    return pl.pallas_call(
        _mod_kernel,
        grid=(depth, n // tn),
        in_specs=[
            pl.BlockSpec((bsz, d), lambda l, j: (0, 0)),
            pl.BlockSpec((None, d, tn), lambda l, j: (l, 0, j)),
            pl.BlockSpec((None, 1, tn), lambda l, j: (l, 0, j)),
        ],
        out_specs=pl.BlockSpec((None, bsz, tn), lambda l, j: (l, 0, j)),
        out_shape=jax.ShapeDtypeStruct((depth, bsz, n), F32),
        compiler_params=_params("parallel", "parallel"),
        name="modulation",
    )(c, mod_w, mod_b.reshape(depth, 1, n))


def _rms_mod(x, g, scale, shift):
    ms = jnp.mean(x * x, axis=-1, keepdims=True)
    y = x * lax.rsqrt(ms + NORM_EPS) * g
    return y * (1.0 + scale) + shift


def _norm_proj_kernel(x_ref, g_ref, sc_ref, sh_ref, w_ref, wg_ref, proj_ref, gates_ref, h_ref):
    @pl.when(pl.program_id(1) == 0)
    def _():
        hb = _rms_mod(x_ref[...], g_ref[...], sc_ref[...], sh_ref[...]).astype(BF16)
        h_ref[...] = hb
        gates_ref[...] = _dot(hb, wg_ref[...])

    proj_ref[...] = _dot(h_ref[...], w_ref[...]).astype(BF16)


def _norm_proj(x2, g, scale, shift, w_main, w_gates, seq):
    t, d = x2.shape
    tm = min(1024, seq)
    tn = PROJ_WIDTH // 4
    per_b = seq // tm
    n = w_main.shape[1]
    return pl.pallas_call(
        _norm_proj_kernel,
        grid=(t // tm, n // tn),
        in_specs=[
            pl.BlockSpec((tm, d), lambda i, j: (i, 0)),
            pl.BlockSpec((1, d), lambda i, j: (0, 0)),
            pl.BlockSpec((None, 1, d), lambda i, j: (i // per_b, 0, 0)),
            pl.BlockSpec((None, 1, d), lambda i, j: (i // per_b, 0, 0)),
            pl.BlockSpec((d, tn), lambda i, j: (0, j)),
            pl.BlockSpec((d, 128), lambda i, j: (0, 0)),
        ],
        out_specs=[
            pl.BlockSpec((tm, tn), lambda i, j: (i, j)),
            pl.BlockSpec((tm, 128), lambda i, j: (i, 0)),
        ],
        out_shape=[jax.ShapeDtypeStruct((t, n), BF16), jax.ShapeDtypeStruct((t, 128), F32)],
        scratch_shapes=[pltpu.VMEM((tm, d), BF16)],
        compiler_params=_params("parallel", "arbitrary"),
        name="norm_proj",
    )(x2, g.reshape(1, d), scale, shift, w_main, w_gates)


SB_TQ = 256
SB_TK = 128
SB_PAIRS = 4
SB_UNROLL = 2
assert SB_UNROLL * SB_TK == SB_TQ
LOG2_E = 1.4426950408889634


def _sb_kernel(q_ref, k_ref, v_ref, o_ref):
    qi = pl.program_id(2)
    lane = lax.broadcasted_iota(jnp.int32, (1, 128), 1)
    first = lane < SB_HEAD_DIM
    qh = []
    for p in range(SB_PAIRS):
        q2 = q_ref[:, p * 128:(p + 1) * 128] * (SB_HEAD_DIM ** -0.5 * LOG2_E)
        zero = jnp.zeros_like(q2)
        qh.append((jnp.where(first, q2, zero), jnp.where(first, zero, q2)))
    t_idx = qi * SB_TQ + lax.broadcasted_iota(jnp.int32, (SB_TQ, 1), 0)
    rs = lax.broadcasted_iota(jnp.int32, (SB_TK, SB_TK), 0)
    cs = lax.broadcasted_iota(jnp.int32, (SB_TK, SB_TK), 1)
    tri = jnp.where(rs > cs, 1.0, 0.0).astype(BF16)
    tri_ext = jnp.concatenate([tri, jnp.ones_like(tri)], axis=1)
    nkb = (qi + 1) * (SB_TQ // SB_TK)

    def body(it, carry, masked):
        cc = list(carry[0])
        acc = list(carry[1])
        for sub in range(SB_UNROLL):
            kb = nkb - 1 - (it * SB_UNROLL + sub)
            r0 = pl.multiple_of(kb * SB_TK, SB_TK)
            past = (kb * SB_TK + lane) < t_idx
            for p in range(SB_PAIRS):
                ks = k_ref[pl.ds(r0, SB_TK), p * 128:(p + 1) * 128]
                vs = v_ref[pl.ds(r0, SB_TK), p * 128:(p + 1) * 128]
                vzero = jnp.zeros_like(vs)
                vh = (jnp.where(first, vs, vzero), jnp.where(first, vzero, vs))
                for h in range(2):
                    z = _dot_nt(qh[p][h], ks)
                    ls = jnp.minimum(z, 0.0) - jnp.log2(1.0 + jnp.exp2(-jnp.abs(z)))
                    lk = ls - z
                    if masked:
                        lk = jnp.where(past, lk, 0.0)
                    sums = _dot(lk.astype(BF16), tri_ext)
                    w = jnp.exp2(ls + sums[:, 0:SB_TK] + cc[2 * p + h])
                    if masked:
                        w = jnp.where(past, w, 0.0)
                    acc[p] = acc[p] + _dot(w.astype(BF16), vh[h])
                    cc[2 * p + h] = cc[2 * p + h] + sums[:, SB_TK:]
        return tuple(cc), tuple(acc)

    init = (tuple(jnp.zeros((SB_TQ, SB_TK), F32) for _ in range(2 * SB_PAIRS)),
            tuple(jnp.zeros((SB_TQ, 128), F32) for _ in range(SB_PAIRS)))
    carry = body(0, init, True)
    _, acc = lax.fori_loop(1, nkb // SB_UNROLL, functools.partial(body, masked=False), carry)
    for p in range(SB_PAIRS):
        o_ref[:, p * 128:(p + 1) * 128] = acc[p].astype(o_ref.dtype)


def _sb_attention(proj, bsz, seq):
    t = proj.shape[0]
    nq = seq // SB_TQ
    wd = 128 * SB_PAIRS
    qb, kb, vb = COL_SBQ // wd, COL_SBK // wd, COL_SBV // wd
    return pl.pallas_call(
        _sb_kernel,
        grid=(bsz, BRANCH_WIDTH // wd, nq),
        in_specs=[
            pl.BlockSpec((SB_TQ, wd), lambda b, p, i: (b * nq + i, qb + p)),
            pl.BlockSpec((seq, wd), lambda b, p, i: (b, kb + p)),
            pl.BlockSpec((seq, wd), lambda b, p, i: (b, vb + p)),
        ],
        out_specs=pl.BlockSpec((SB_TQ, wd), lambda b, p, i: (b * nq + i, p)),
        out_shape=jax.ShapeDtypeStruct((t, BRANCH_WIDTH), BF16),
        compiler_params=_params("parallel", "parallel", "arbitrary"),
        name="sb_attention",
    )(proj, proj, proj)


def _hgrn_consts():
    n_l = CHUNK
    t = np.arange(n_l)[:, None]
    u = np.arange(n_l)[None, :]
    dq, dk, masks = [], [], []
    n = n_l
    while n >= 2:
        half = n // 2
        a_t = (t // n) * n
        mid = a_t + half - 1
        up_t = (t - a_t) >= half
        up_u = (u - (u // n) * n) >= half
        dq.append(up_t & (u > mid) & (u <= t))
        dk.append((~up_t) & (u > t) & (u <= mid))
        masks.append(((t // n) == (u // n)) & up_t & (~up_u))
        n //= 2
    masks.append(t == u)
    cm = np.concatenate(dq + dk + [u <= t, u > t], axis=0).astype(np.float32)
    return cm, np.stack(masks).astype(np.float32)


HG_LEVELS = 7


def _hgrn_kernel(lbl_ref, q_ref, f_ref, i_ref, g_ref, cm_ref, mask_ref, ng_ref, o_ref, st_ref,
                 ee_ref, *, layer, nchunk):
    @pl.when(pl.program_id(1) == 0)
    def _():
        st_ref[...] = jnp.zeros_like(st_ref)

    lg = lbl_ref[...]
    e = jnp.exp(lg - jnp.max(lg, axis=0, keepdims=True))
    sm = e / jnp.sum(e, axis=0, keepdims=True)
    lb = jnp.sum(sm[0:layer + 1], axis=0, keepdims=True) - sm[0:1]
    log_lb = jnp.log(jnp.maximum(lb, LB_FLOOR))
    log_1m = jnp.log(1.0 - lb)
    cm = cm_ref[...]
    nl = HG_LEVELS * CHUNK

    def chunk_body(c, carry):
        r0 = pl.multiple_of(c * CHUNK, CHUNK)
        rows = pl.ds(r0, CHUNK)
        z = f_ref[rows, :].astype(F32)
        bb = log_1m + _log_sigmoid(z)
        lf = jnp.maximum(log_lb, bb) + jnp.log(1.0 + jnp.exp(-jnp.abs(log_lb - bb)))
        kin_all = (1.0 - lb) * _sigmoid(-z)
        ee_ref[...] = jnp.exp(_sel_dot2(cm, lf))
        for h in range(HEADS4):
            hs = slice(h * HEAD_DIM, (h + 1) * HEAD_DIM)
            q = q_ref[rows, hs].astype(F32)
            iv = i_ref[rows, hs].astype(F32)
            gv = g_ref[rows, hs].astype(F32)
            kin = kin_all[:, hs]
            i_act = _silu(iv).astype(BF16)
            scores = mask_ref[HG_LEVELS] * _dot_nt(q.astype(BF16), kin.astype(BF16))
            for l in range(HG_LEVELS):
                ql = (q * ee_ref[l * CHUNK:(l + 1) * CHUNK, hs]).astype(BF16)
                kl = (kin * ee_ref[nl + l * CHUNK:nl + (l + 1) * CHUNK, hs]).astype(BF16)
                scores = scores + mask_ref[l] * _dot_nt(ql, kl)
            eb = ee_ref[2 * nl:2 * nl + CHUNK, hs]
            esuf = ee_ref[2 * nl + CHUNK:2 * nl + 2 * CHUNK, hs]
            st = st_ref[h]
            o = _dot(scores.astype(BF16), i_act) + _dot_nt((q * eb).astype(BF16), st.astype(BF16))
            st_ref[h] = st * eb[CHUNK - 1:CHUNK, :] + _dot_tn(i_act, (kin * esuf).astype(BF16))
            ms = jnp.mean(o * o, axis=-1, keepdims=True)
            y = o * lax.rsqrt(ms + NORM_EPS) * ng_ref[:, hs] * _silu(gv)
            o_ref[rows, hs] = y.astype(o_ref.dtype)
        return carry

    lax.fori_loop(0, nchunk, chunk_body, 0)


def _hgrn(proj, lb_logits, norm_g, layer, bsz, seq):
    t = proj.shape[0]
    ts = min(512, seq)
    nt = seq // ts
    cm, masks = _hgrn_consts()
    w = BRANCH_WIDTH
    col = lambda c: (lambda b, i: (b * nt + i, c // w))
    return pl.pallas_call(
        functools.partial(_hgrn_kernel, layer=layer, nchunk=ts // CHUNK),
        grid=(bsz, nt),
        in_specs=[
            pl.BlockSpec(lb_logits.shape, lambda b, i: (0, 0)),
            pl.BlockSpec((ts, w), col(COL_HGQ)),
            pl.BlockSpec((ts, w), col(COL_HGF)),
            pl.BlockSpec((ts, w), col(COL_HGI)),
            pl.BlockSpec((ts, w), col(COL_HGG)),
            pl.BlockSpec(cm.shape, lambda b, i: (0, 0)),
            pl.BlockSpec(masks.shape, lambda b, i: (0, 0, 0)),
            pl.BlockSpec((1, w), lambda b, i: (0, 0)),
        ],
        out_specs=pl.BlockSpec((ts, w), lambda b, i: (b * nt + i, 0)),
        out_shape=jax.ShapeDtypeStruct((t, w), BF16),
        scratch_shapes=[pltpu.VMEM((HEADS4, HEAD_DIM, HEAD_DIM), F32),
                        pltpu.VMEM(((2 * HG_LEVELS + 2) * CHUNK, w), F32)],
        compiler_params=_params("parallel", "arbitrary"),
        name="hgrn2",
    )(lb_logits, proj, proj, proj, proj, jnp.asarray(cm, BF16), jnp.asarray(masks, F32),
      norm_g.reshape(1, w))


def _mlstm_kernel(q_ref, k_ref, v_ref, og_ref, gc_ref, gr_ref, cw_ref, cb_ref, brow_ref, bcol_ref,
                  ng_ref, o_ref, qbuf, kbuf, qact, kact, ct_ref, m_ref, *, ts):
    nchunk = ts // CHUNK
    w = BRANCH_WIDTH

    @pl.when(pl.program_id(1) == 0)
    def _():
        qbuf[0:8, :] = jnp.zeros((8, w), F32)
        kbuf[0:8, :] = jnp.zeros((8, w), F32)
        ct_ref[...] = jnp.zeros_like(ct_ref)
        m_ref[...] = jnp.zeros_like(m_ref)

    qbuf[8:8 + ts, :] = q_ref[...].astype(F32)
    kbuf[8:8 + ts, :] = k_ref[...].astype(F32)

    rs = lax.broadcasted_iota(jnp.int32, (CHUNK, CHUNK), 0)
    cs = lax.broadcasted_iota(jnp.int32, (CHUNK, CHUNK), 1)
    causal = cs <= rs
    tri_c = jnp.where(causal, 1.0, 0.0).astype(BF16)
    tri_r = jnp.where(rs <= cs, 1.0, 0.0).astype(BF16)
    ones_v = jnp.ones((CHUNK, HEAD_DIM), F32)
    kscale = HEAD_DIM ** -0.5

    for cc in range(nchunk):
        yq = jnp.zeros((CHUNK, w), F32) + cb_ref[:, 0:w]
        yk = jnp.zeros((CHUNK, w), F32) + cb_ref[:, w:2 * w]
        for j in range(ML_CONV):
            lo = cc * CHUNK + 8 - (ML_CONV - 1 - j)
            yq = yq + cw_ref[j:j + 1, 0:w] * qbuf[lo:lo + CHUNK, :]
            yk = yk + cw_ref[j:j + 1, w:2 * w] * kbuf[lo:lo + CHUNK, :]
        qact[cc * CHUNK:(cc + 1) * CHUNK, :] = _silu(yq).astype(BF16)
        kact[cc * CHUNK:(cc + 1) * CHUNK, :] = (_silu(yk) * kscale).astype(BF16)

    def chunk_body(c, carry):
        r0 = pl.multiple_of(c * CHUNK, CHUNK)
        rows = pl.ds(r0, CHUNK)
        pre_c = gc_ref[rows, :] + brow_ref[...]
        pre_r = gr_ref[:, rows] + bcol_ref[...]
        b_c = _sel_dot(tri_c, _log_sigmoid(pre_c))
        b_r = _dot_sel(_log_sigmoid(pre_r), tri_r)
        for h in range(HEADS4):
            hs = slice(h * HEAD_DIM, (h + 1) * HEAD_DIM)
            qh = qact[rows, hs]
            kh = kact[rows, hs]
            vx = jnp.concatenate([v_ref[rows, hs].astype(F32), ones_v], axis=1)
            bcol = b_c[:, HEADS4 + h:HEADS4 + h + 1]
            icol = pre_c[:, h:h + 1]
            brow = b_r[HEADS4 + h:HEADS4 + h + 1, :]
            irow = pre_r[h:h + 1, :]
            m = m_ref[h:h + 1, 0:1]
            log_d = jnp.where(causal, bcol - brow + irow, MASK_NEG)
            log_inter = bcol + m
            m_t = jnp.maximum(log_inter, jnp.max(log_d, axis=-1, keepdims=True))
            dmat = jnp.where(causal, jnp.exp(log_d - m_t), 0.0)
            inter = jnp.exp(log_inter - m_t)
            qk = _dot_nt(qh, kh) * dmat
            ct = ct_ref[h]
            hx = _dot(qk.astype(BF16), vx.astype(BF16)) + inter * _dot(qh, ct.astype(BF16))
            hout = hx[:, 0:HEAD_DIM] / jnp.maximum(jnp.abs(hx[:, HEAD_DIM:]), jnp.exp(-m_t))
            b_last = bcol[CHUNK - 1:CHUNK, :]
            m_new = jnp.maximum(b_last + m, jnp.max(b_last - brow + irow, axis=-1, keepdims=True))
            w_col = jnp.exp(b_last - bcol + icol - m_new)
            dec = jnp.exp(b_last + m - m_new)
            ct_ref[h] = dec * ct + _dot_tn(kh, (vx * w_col).astype(BF16))
            m_ref[h:h + 1, :] = jnp.broadcast_to(m_new, (1, 128))
            ms = jnp.mean(hout * hout, axis=-1, keepdims=True)
            y = hout * lax.rsqrt(ms + NORM_EPS) * ng_ref[:, hs] * _sigmoid(og_ref[rows, hs].astype(F32))
            o_ref[rows, hs] = y.astype(o_ref.dtype)
        return carry

    lax.fori_loop(0, nchunk, chunk_body, 0)
    qbuf[0:8, :] = qbuf[ts:ts + 8, :]
    kbuf[0:8, :] = kbuf[ts:ts + 8, :]


def _mlstm(proj, gates, gates_t, conv_w, conv_b, gate_b, norm_g, bsz, seq):
    t = proj.shape[0]
    ts = min(512, seq)
    nt = seq // ts
    w = BRANCH_WIDTH
    col = lambda c: (lambda b, i: (b * nt + i, c // w))
    brow = jnp.zeros((1, 128), F32).at[0, 0:8].set(gate_b)
    bcol = gate_b.reshape(8, 1)
    return pl.pallas_call(
        functools.partial(_mlstm_kernel, ts=ts),
        grid=(bsz, nt),
        in_specs=[
            pl.BlockSpec((ts, w), col(COL_MLQ)),
            pl.BlockSpec((ts, w), col(COL_MLK)),
            pl.BlockSpec((ts, w), col(COL_MLV)),
            pl.BlockSpec((ts, w), col(COL_MLO)),
            pl.BlockSpec((ts, 128), lambda b, i: (b * nt + i, 0)),
            pl.BlockSpec((8, ts), lambda b, i: (0, b * nt + i)),
            pl.BlockSpec((ML_CONV, 2 * w), lambda b, i: (0, 0)),
            pl.BlockSpec((1, 2 * w), lambda b, i: (0, 0)),
            pl.BlockSpec((1, 128), lambda b, i: (0, 0)),
            pl.BlockSpec((8, 1), lambda b, i: (0, 0)),
            pl.BlockSpec((1, w), lambda b, i: (0, 0)),
        ],
        out_specs=pl.BlockSpec((ts, w), lambda b, i: (b * nt + i, 0)),
        out_shape=jax.ShapeDtypeStruct((t, w), BF16),
        scratch_shapes=[
            pltpu.VMEM((ts + 8, w), F32),
            pltpu.VMEM((ts + 8, w), F32),
            pltpu.VMEM((ts, w), BF16),
            pltpu.VMEM((ts, w), BF16),
            pltpu.VMEM((HEADS4, HEAD_DIM, 2 * HEAD_DIM), F32),
            pltpu.VMEM((8, 128), F32),
        ],
        compiler_params=_params("parallel", "arbitrary"),
        name="mlstm",
    )(proj, proj, proj, proj, gates, gates_t, conv_w, conv_b.reshape(1, 2 * w), brow, bcol,
      norm_g.reshape(1, w))


def _merge_kernel(x_ref, ya_ref, yb_ref, yc_ref, bg_ref, wb_ref, wo_ref, g1_ref, o_ref):
    d = D_MODEL
    merged = jnp.zeros(x_ref.shape, F32)
    for g, y_ref in enumerate((ya_ref, yb_ref, yc_ref)):
        up = _dot(y_ref[...], wb_ref[g])
        merged = merged + _sigmoid(bg_ref[:, g * d:(g + 1) * d].astype(F32)) * up
    y = _dot(merged.astype(BF16), wo_ref[...])
    o_ref[...] = x_ref[...] + g1_ref[...] * y


def _merge(x2, ya, yb, yc, proj, w_branch, w_out, gate1, seq):
    t, d = x2.shape
    tm = min(512, seq)
    per_b = seq // tm
    w = BRANCH_WIDTH
    row = lambda i: (i, 0)
    return pl.pallas_call(
        _merge_kernel,
        grid=(t // tm,),
        in_specs=[
            pl.BlockSpec((tm, d), row),
            pl.BlockSpec((tm, w), row),
            pl.BlockSpec((tm, w), row),
            pl.BlockSpec((tm, w), row),
            pl.BlockSpec((tm, 3 * d), lambda i: (i, COL_BRG // (3 * d))),
            pl.BlockSpec((3, w, d), lambda i: (0, 0, 0)),
            pl.BlockSpec((d, d), lambda i: (0, 0)),
            pl.BlockSpec((None, 1, d), lambda i: (i // per_b, 0, 0)),
        ],
        out_specs=pl.BlockSpec((tm, d), row),
        out_shape=jax.ShapeDtypeStruct((t, d), F32),
        compiler_params=_params("parallel"),
        name="merge",
    )(x2, ya, yb, yc, proj, w_branch, w_out, gate1)


def _fold_keys_kernel(keys_ref, wq_ref, o_ref):
    o_ref[...] = lax.dot_general(keys_ref[...], wq_ref[...], (((1,), (1,)), ((), ())),
                                 preferred_element_type=F32, precision=lax.Precision.HIGHEST)


def _fold_keys(pk_keys, pk_wq):
    d = pk_wq.shape[0]
    sets = 2 * PK_HEADS
    keys = pk_keys.reshape(sets, PK_NKEYS, PK_HALF)
    wq = pk_wq.reshape(d, sets, PK_HALF).transpose(1, 0, 2)
    out = pl.pallas_call(
        _fold_keys_kernel,
        grid=(sets,),
        in_specs=[
            pl.BlockSpec((None, PK_NKEYS, PK_HALF), lambda s: (s, 0, 0)),
            pl.BlockSpec((None, d, PK_HALF), lambda s: (s, 0, 0)),
        ],
        out_specs=pl.BlockSpec((None, PK_NKEYS, d), lambda s: (s, 0, 0)),
        out_shape=jax.ShapeDtypeStruct((sets, PK_NKEYS, d), F32),
        compiler_params=_params("parallel"),
        name="fold_keys",
    )(keys, wq)
    return out.reshape(sets * PK_NKEYS, d)


def _peer_scores_kernel(x_ref, g_ref, sc_ref, sh_ref, wk_ref, ht_ref, st_ref):
    h = _rms_mod(x_ref[...], g_ref[...], sc_ref[...], sh_ref[...])
    ht = h.T.astype(BF16)
    ht_ref[...] = ht
    st_ref[...] = _dot(wk_ref[...], ht)


def _peer_scores(x2, g, scale, shift, wk_t, seq):
    t, d = x2.shape
    tb = min(512, seq)
    per_b = seq // tb
    nk = wk_t.shape[0]
    return pl.pallas_call(
        _peer_scores_kernel,
        grid=(t // tb,),
        in_specs=[
            pl.BlockSpec((tb, d), lambda i: (i, 0)),
            pl.BlockSpec((1, d), lambda i: (0, 0)),
            pl.BlockSpec((None, 1, d), lambda i: (i // per_b, 0, 0)),
            pl.BlockSpec((None, 1, d), lambda i: (i // per_b, 0, 0)),
            pl.BlockSpec((nk, d), lambda i: (0, 0)),
        ],
        out_specs=[
            pl.BlockSpec((d, tb), lambda i: (0, i)),
            pl.BlockSpec((nk, tb), lambda i: (0, i)),
        ],
        out_shape=[jax.ShapeDtypeStruct((d, t), BF16), jax.ShapeDtypeStruct((nk, t), F32)],
        compiler_params=_params("parallel"),
        name="peer_scores",
    )(x2, g.reshape(1, d), scale, shift, wk_t)


NEG_INF = float("-inf")
PK_ROUNDS = PK_TOPK + 1
PK_RANK_NONE = 32.0


def _top_rows(x, rounds, want_rank=False):
    vals = []
    rank = jnp.full(x.shape, PK_RANK_NONE, F32)
    for r in range(rounds):
        m = jnp.max(x, axis=0, keepdims=True)
        vals.append(m)
        eq = x == m
        if want_rank:
            rank = jnp.where(eq, float(r), rank)
        if r + 1 < rounds:
            x = jnp.where(eq, NEG_INF, x)
    return (vals, rank) if want_rank else vals


def _peer_stats_kernel(st_ref, cnt_ref, e1_ref, r2_ref, e2_ref):
    tb = st_ref.shape[-1]
    for h in range(PK_HEADS):
        s1 = st_ref[h, 0]
        s2 = st_ref[h, 1]
        v1 = _top_rows(s1, PK_ROUNDS)
        v2, rank2 = _top_rows(s2, PK_ROUNDS, want_rank=True)
        v2_16 = jnp.concatenate(v2[0:16], axis=0)
        blocks = [v1[0] + v2_16]
        for a in range(1, 8):
            blocks.append(v1[a] + v2_16[0:8])
        blocks.append(jnp.concatenate(v1[8:16], axis=0) + v2[0])
        blocks.append(jnp.concatenate(
            [v1[0] + v2[16], v1[16] + v2[0], jnp.full((6, tb), NEG_INF, F32)], axis=0))
        cand = jnp.concatenate(blocks, axis=0)
        top = _top_rows(cand, PK_ROUNDS)
        thr = 0.5 * (top[PK_TOPK - 1] + top[PK_TOPK])
        zsum = jnp.zeros((1, tb), F32)
        for r in range(PK_TOPK):
            zsum = zsum + jnp.exp(top[r] - top[0])
        need = thr - s1
        cnt = jnp.zeros(s1.shape, F32)
        for b in range(PK_TOPK):
            cnt = cnt + jnp.where(v2[b] >= need, 1.0, 0.0)
        cnt_ref[h] = cnt
        e1_ref[h] = jnp.exp(s1 - v1[0]) / zsum
        r2_ref[h] = rank2.astype(BF16)
        e2_ref[h] = jnp.exp(s2 - v2[0]).astype(BF16)


def _peer_stats(st):
    nk, t = st.shape
    tb = min(256, t)
    st4 = st.reshape(PK_HEADS, 2, PK_NKEYS, t)
    shp = lambda dt: jax.ShapeDtypeStruct((PK_HEADS, PK_NKEYS, t), dt)
    spec = pl.BlockSpec((PK_HEADS, PK_NKEYS, tb), lambda i: (0, 0, i))
    return pl.pallas_call(
        _peer_stats_kernel,
        grid=(t // tb,),
        in_specs=[pl.BlockSpec((PK_HEADS, 2, PK_NKEYS, tb), lambda i: (0, 0, 0, i))],
        out_specs=[spec, spec, spec, spec],
        out_shape=[shp(F32), shp(F32), shp(BF16), shp(BF16)],
        compiler_params=_params("parallel"),
        name="peer_stats",
    )(st4)


PK_ITILE = 8
PK_DENSE_TB = 512
GELU_C0 = 0.7978845608028654
GELU_C1 = 0.044715


def _gelu_tanh(a):
    return 0.5 * a * (1.0 + jnp.tanh(GELU_C0 * (a + GELU_C1 * (a * a * a))))


def _peer_tile_weights(a, row0, cnt_ref, e1_ref, r2_ref, e2_ref, p_ref):
    tb = a.shape[1]
    shape = (PK_NKEYS, tb)
    zero = jnp.zeros(shape, BF16)
    for il in range(PK_ITILE):
        g = _gelu_tanh(a[il * PK_NKEYS:(il + 1) * PK_NKEYS, :].astype(BF16))
        wsum = zero
        for h in range(PK_HEADS):
            cb = jnp.broadcast_to(cnt_ref[h, row0 + il:row0 + il + 1, :], shape).astype(BF16)
            eb = jnp.broadcast_to(e1_ref[h, row0 + il:row0 + il + 1, :], shape).astype(BF16)
            wsum = wsum + eb * jnp.minimum(jnp.maximum(cb - r2_ref[h], zero), e2_ref[h])
        p_ref[il * PK_NKEYS:(il + 1) * PK_NKEYS, :] = g * wsum


def _peer_dense_kernel(x_ref, g2_ref, ht_ref, u0_ref, ua_ref, ub_ref, vt_ref, cnt_ref, e1_ref,
                       r2_ref, e2_ref, o_ref, acc_ref, a_ref, pa_ref, pb_ref):
    g = pl.program_id(1)
    te = pa_ref.shape[0]
    ht = ht_ref[...]

    @pl.when(g == 0)
    def _():
        acc_ref[...] = jnp.zeros_like(acc_ref)
        a_ref[0] = _dot(u0_ref[...], ht)

    _peer_tile_weights(a_ref.at[g % 2], 0, cnt_ref, e1_ref, r2_ref, e2_ref, pa_ref)
    a_odd = _dot(ua_ref[...], ht)
    acc_ref[...] += _dot(vt_ref[0], pa_ref[...])
    _peer_tile_weights(a_odd, PK_ITILE, cnt_ref, e1_ref, r2_ref, e2_ref, pb_ref)
    acc_ref[...] += _dot(vt_ref[1], pb_ref[...])
    a_ref[(g + 1) % 2] = _dot(ub_ref[...], ht)

    @pl.when(g == pl.num_programs(1) - 1)
    def _():
        o_ref[...] = x_ref[...] + g2_ref[...] * acc_ref[...].T


def _peer_dense(x2, gate2, ht, u_bf, vt_bf, cnt, e1, r2, e2, seq):
    t, d = x2.shape
    tb = min(PK_DENSE_TB, seq)
    per_b = seq // tb
    te = PK_ITILE * PK_NKEYS
    ntile = u_bf.shape[0] // te
    u_spec = lambda f: pl.BlockSpec((te, d), lambda i, g: (f(g), 0))
    pair_i = pl.BlockSpec((PK_HEADS, 2 * PK_ITILE, tb), lambda i, g: (0, g, i))
    full_j = pl.BlockSpec((PK_HEADS, PK_NKEYS, tb), lambda i, g: (0, 0, i))
    return pl.pallas_call(
        _peer_dense_kernel,
        grid=(t // tb, ntile // 2),
        in_specs=[
            pl.BlockSpec((tb, d), lambda i, g: (i, 0)),
            pl.BlockSpec((None, 1, d), lambda i, g: (i // per_b, 0, 0)),
            pl.BlockSpec((d, tb), lambda i, g: (0, i)),
            u_spec(lambda g: 0), u_spec(lambda g: 2 * g + 1),
            u_spec(lambda g: jnp.minimum(2 * g + 2, ntile - 1)),
            pl.BlockSpec((2, d, te), lambda i, g: (g, 0, 0)),
            pair_i, pair_i, full_j, full_j,
        ],
        out_specs=pl.BlockSpec((tb, d), lambda i, g: (i, 0)),
        out_shape=jax.ShapeDtypeStruct((t, d), F32),
        scratch_shapes=[pltpu.VMEM((d, tb), F32), pltpu.VMEM((2, te, tb), F32),
                        pltpu.VMEM((te, tb), BF16), pltpu.VMEM((te, tb), BF16)],
        compiler_params=_params("parallel", "arbitrary"),
        name="peer_dense",
    )(x2, gate2, ht, u_bf, u_bf, u_bf, vt_bf, cnt, e1, r2, e2)


def _final_norm_kernel(x_ref, g_ref, o_ref):
    x = x_ref[...]
    ms = jnp.mean(x * x, axis=-1, keepdims=True)
    o_ref[...] = x * lax.rsqrt(ms + NORM_EPS) * g_ref[...]


def _final_norm(x2, g):
    t, d = x2.shape
    tm = min(1024, t)
    return pl.pallas_call(
        _final_norm_kernel,
        grid=(t // tm,),
        in_specs=[pl.BlockSpec((tm, d), lambda i: (i, 0)), pl.BlockSpec((1, d), lambda i: (0, 0))],
        out_specs=pl.BlockSpec((tm, d), lambda i: (i, 0)),
        out_shape=jax.ShapeDtypeStruct((t, d), F32),
        compiler_params=_params("parallel"),
        name="final_norm",
    )(x2, g.reshape(1, d))


def _permute_w_in(w_in):
    main = jnp.concatenate([w_in[..., 5640:8712], w_in[..., 3584:4608], w_in[..., 0:3584],
                            w_in[..., 4608:5632]], axis=-1).astype(BF16)
    gates = jnp.pad(w_in[..., 5632:5640], ((0, 0), (0, 0), (0, 120))).astype(BF16)
    return main, gates


def kernel(x, c, mod_w, mod_b, norm_mix_g, norm_ffn_g, w_in, ml_conv_w, ml_conv_b, ml_gate_b,
           hg_lb_logits, hg_norm_g, ml_norm_g, w_branch, w_out, pk_wq, pk_keys, pk_u, pk_v, final_g):
    bsz, seq, d = x.shape
    depth = w_in.shape[0]
    t = bsz * seq
    x2 = x.reshape(t, d)

    mod = _modulation(c, mod_w, mod_b).reshape(depth, bsz, 6, 1, d)
    w_main, w_gates = _permute_w_in(w_in)
    w_branch_bf = w_branch.astype(BF16)
    w_out_bf = w_out.astype(BF16)
    u_bf = pk_u.astype(BF16)
    te = PK_ITILE * PK_NKEYS
    vt_bf = jnp.swapaxes(pk_v.astype(BF16).reshape(depth, pk_v.shape[1] // te, te, d), 2, 3)

    for l in range(depth):
        shift1, scale1, gate1, shift2, scale2, gate2 = (mod[l, :, r] for r in range(6))
        proj, gates = _norm_proj(x2, norm_mix_g[l], scale1, shift1, w_main[l], w_gates[l], seq)
        ya = _sb_attention(proj, bsz, seq)
        yb = _hgrn(proj, hg_lb_logits, hg_norm_g[l], l, bsz, seq)
        yc = _mlstm(proj, gates, gates[:, 0:8].T, ml_conv_w[l], ml_conv_b[l], ml_gate_b[l],
                    ml_norm_g[l], bsz, seq)
        x2 = _merge(x2, ya, yb, yc, proj, w_branch_bf[l], w_out_bf[l], gate1, seq)
        wk_t = _fold_keys(pk_keys[l], pk_wq[l]).astype(BF16)
        ht, st = _peer_scores(x2, norm_ffn_g[l], scale2, shift2, wk_t, seq)
        cnt, e1, r2, e2 = _peer_stats(st)
        x2 = _peer_dense(x2, gate2, ht, u_bf[l], vt_bf[l], cnt, e1, r2, e2, seq)
    return _final_norm(x2, final_g).reshape(bsz, seq, d)
```

```python
import functools

import numpy as np
import jax
import jax.numpy as jnp
from jax import lax
from jax.experimental import pallas as pl
from jax.experimental.pallas import tpu as pltpu

F32 = jnp.float32
BF16 = jnp.bfloat16

D_MODEL = 1024
DEPTH = 4
NORM_EPS = 1e-6
MASK_NEG = -1e30
LB_FLOOR = 1e-30
SB_HEAD_DIM = 64
HEADS4 = 4
HEAD_DIM = 128
BRANCH_WIDTH = 512
ML_CONV = 4
PK_HEADS = 8
PK_NKEYS = 128
PK_HALF = 64
PK_TOPK = 16

PROJ_WIDTH = 8704
COL_BRG = 0
COL_MLQ = 3072
COL_MLK = 3584
COL_SBQ = 4096
COL_SBK = 4608
COL_SBV = 5120
COL_HGQ = 5632
COL_HGF = 6144
COL_HGI = 6656
COL_HGG = 7168
COL_MLV = 7680
COL_MLO = 8192

CHUNK = 128
LANES = 128
VMEM_LIMIT = 56 * 1024 * 1024


def _dot(a, b):
    return jnp.dot(a, b, preferred_element_type=F32)


def _dot_nt(a, b):
    return lax.dot_general(a, b, (((1,), (1,)), ((), ())), preferred_element_type=F32)


def _dot_tn(a, b):
    return lax.dot_general(a, b, (((0,), (0,)), ((), ())), preferred_element_type=F32)


def _split3(x):
    hi = x.astype(BF16)
    r = x - hi.astype(F32)
    mid = r.astype(BF16)
    lo = (r - mid.astype(F32)).astype(BF16)
    return hi, mid, lo


def _sel_dot(m01, x):
    hi, mid, lo = _split3(x)
    return _dot(m01, hi) + _dot(m01, mid) + _dot(m01, lo)


def _dot_sel(x, m01):
    hi, mid, lo = _split3(x)
    return _dot(hi, m01) + _dot(mid, m01) + _dot(lo, m01)


def _sel_dot2(m01, x):
    hi = x.astype(BF16)
    lo = (x - hi.astype(F32)).astype(BF16)
    return _dot(m01, hi) + _dot(m01, lo)


def _sigmoid(x):
    return 1.0 / (1.0 + jnp.exp(-x))


def _silu(x):
    return x * _sigmoid(x)


def _log_sigmoid(x):
    return jnp.minimum(x, 0.0) - jnp.log(1.0 + jnp.exp(-jnp.abs(x)))


def _params(*sem):
    return pltpu.CompilerParams(dimension_semantics=sem, vmem_limit_bytes=VMEM_LIMIT)


def _mod_kernel(c_ref, w_ref, b_ref, o_ref):
    cond = _silu(c_ref[...])
    o_ref[...] = jnp.dot(cond, w_ref[...], preferred_element_type=F32,
                         precision=lax.Precision.HIGHEST) + b_ref[...]


def _modulation(c, mod_w, mod_b):
    depth, d, n = mod_w.shape
    bsz = c.shape[0]
    tn = d
    return pl.pallas_call(
        _mod_kernel,
        grid=(depth, n // tn),
        in_specs=[
            pl.BlockSpec((bsz, d), lambda l, j: (0, 0)),
            pl.BlockSpec((None, d, tn), lambda l, j: (l, 0, j)),
            pl.BlockSpec((None, 1, tn), lambda l, j: (l, 0, j)),
        ],
        out_specs=pl.BlockSpec((None, bsz, tn), lambda l, j: (l, 0, j)),
        out_shape=jax.ShapeDtypeStruct((depth, bsz, n), F32),
        compiler_params=_params("parallel", "parallel"),
        name="modulation",
    )(c, mod_w, mod_b.reshape(depth, 1, n))


def _rms_mod(x, g, scale, shift):
    ms = jnp.mean(x * x, axis=-1, keepdims=True)
    y = x * lax.rsqrt(ms + NORM_EPS) * g
    return y * (1.0 + scale) + shift


def _norm_proj_kernel(x_ref, g_ref, sc_ref, sh_ref, w_ref, wg_ref, proj_ref, gates_ref, h_ref):
    @pl.when(pl.program_id(1) == 0)
    def _():
        hb = _rms_mod(x_ref[...], g_ref[...], sc_ref[...], sh_ref[...]).astype(BF16)
        h_ref[...] = hb
        gates_ref[...] = _dot(hb, wg_ref[...])

    proj_ref[...] = _dot(h_ref[...], w_ref[...]).astype(BF16)


def _norm_proj(x2, g, scale, shift, w_main, w_gates, seq):
    t, d = x2.shape
    tm = min(1024, seq)
    tn = PROJ_WIDTH // 4
    per_b = seq // tm
    n = w_main.shape[1]
    return pl.pallas_call(
        _norm_proj_kernel,
        grid=(t // tm, n // tn),
        in_specs=[
            pl.BlockSpec((tm, d), lambda i, j: (i, 0)),
            pl.BlockSpec((1, d), lambda i, j: (0, 0)),
            pl.BlockSpec((None, 1, d), lambda i, j: (i // per_b, 0, 0)),
            pl.BlockSpec((None, 1, d), lambda i, j: (i // per_b, 0, 0)),
            pl.BlockSpec((d, tn), lambda i, j: (0, j)),
            pl.BlockSpec((d, 128), lambda i, j: (0, 0)),
        ],
        out_specs=[
            pl.BlockSpec((tm, tn), lambda i, j: (i, j)),
            pl.BlockSpec((tm, 128), lambda i, j: (i, 0)),
        ],
        out_shape=[jax.ShapeDtypeStruct((t, n), BF16), jax.ShapeDtypeStruct((t, 128), F32)],
        scratch_shapes=[pltpu.VMEM((tm, d), BF16)],
        compiler_params=_params("parallel", "arbitrary"),
        name="norm_proj",
    )(x2, g.reshape(1, d), scale, shift, w_main, w_gates)


SB_TQ = 256
SB_TK = 128
SB_PAIRS = 4
SB_UNROLL = 2
assert SB_UNROLL * SB_TK == SB_TQ
LOG2_E = 1.4426950408889634


def _sb_kernel(q_ref, k_ref, v_ref, o_ref):
    qi = pl.program_id(2)
    lane = lax.broadcasted_iota(jnp.int32, (1, 128), 1)
    first = lane < SB_HEAD_DIM
    qh = []
    for p in range(SB_PAIRS):
        q2 = q_ref[:, p * 128:(p + 1) * 128] * (SB_HEAD_DIM ** -0.5 * LOG2_E)
        zero = jnp.zeros_like(q2)
        qh.append((jnp.where(first, q2, zero), jnp.where(first, zero, q2)))
    t_idx = qi * SB_TQ + lax.broadcasted_iota(jnp.int32, (SB_TQ, 1), 0)
    rs = lax.broadcasted_iota(jnp.int32, (SB_TK, SB_TK), 0)
    cs = lax.broadcasted_iota(jnp.int32, (SB_TK, SB_TK), 1)
    tri = jnp.where(rs > cs, 1.0, 0.0).astype(BF16)
    tri_ext = jnp.concatenate([tri, jnp.ones_like(tri)], axis=1)
    nkb = (qi + 1) * (SB_TQ // SB_TK)

    def body(it, carry, masked):
        cc = list(carry[0])
        acc = list(carry[1])
        for sub in range(SB_UNROLL):
            kb = nkb - 1 - (it * SB_UNROLL + sub)
            r0 = pl.multiple_of(kb * SB_TK, SB_TK)
            past = (kb * SB_TK + lane) < t_idx
            for p in range(SB_PAIRS):
                ks = k_ref[pl.ds(r0, SB_TK), p * 128:(p + 1) * 128]
                vs = v_ref[pl.ds(r0, SB_TK), p * 128:(p + 1) * 128]
                vzero = jnp.zeros_like(vs)
                vh = (jnp.where(first, vs, vzero), jnp.where(first, vzero, vs))
                for h in range(2):
                    z = _dot_nt(qh[p][h], ks)
                    ls = jnp.minimum(z, 0.0) - jnp.log2(1.0 + jnp.exp2(-jnp.abs(z)))
                    lk = ls - z
                    if masked:
                        lk = jnp.where(past, lk, 0.0)
                    sums = _dot(lk.astype(BF16), tri_ext)
                    w = jnp.exp2(ls + sums[:, 0:SB_TK] + cc[2 * p + h])
                    if masked:
                        w = jnp.where(past, w, 0.0)
                    acc[p] = acc[p] + _dot(w.astype(BF16), vh[h])
                    cc[2 * p + h] = cc[2 * p + h] + sums[:, SB_TK:]
        return tuple(cc), tuple(acc)

    init = (tuple(jnp.zeros((SB_TQ, SB_TK), F32) for _ in range(2 * SB_PAIRS)),
            tuple(jnp.zeros((SB_TQ, 128), F32) for _ in range(SB_PAIRS)))
    carry = body(0, init, True)
    _, acc = lax.fori_loop(1, nkb // SB_UNROLL, functools.partial(body, masked=False), carry)
    for p in range(SB_PAIRS):
        o_ref[:, p * 128:(p + 1) * 128] = acc[p].astype(o_ref.dtype)


def _sb_attention(proj, bsz, seq):
    t = proj.shape[0]
    nq = seq // SB_TQ
    wd = 128 * SB_PAIRS
    qb, kb, vb = COL_SBQ // wd, COL_SBK // wd, COL_SBV // wd
    return pl.pallas_call(
        _sb_kernel,
        grid=(bsz, BRANCH_WIDTH // wd, nq),
        in_specs=[
            pl.BlockSpec((SB_TQ, wd), lambda b, p, i: (b * nq + i, qb + p)),
            pl.BlockSpec((seq, wd), lambda b, p, i: (b, kb + p)),
            pl.BlockSpec((seq, wd), lambda b, p, i: (b, vb + p)),
        ],
        out_specs=pl.BlockSpec((SB_TQ, wd), lambda b, p, i: (b * nq + i, p)),
        out_shape=jax.ShapeDtypeStruct((t, BRANCH_WIDTH), BF16),
        compiler_params=_params("parallel", "parallel", "arbitrary"),
        name="sb_attention",
    )(proj, proj, proj)


def _hgrn_consts():
    n_l = CHUNK
    t = np.arange(n_l)[:, None]
    u = np.arange(n_l)[None, :]
    dq, dk, masks = [], [], []
    n = n_l
    while n >= 2:
        half = n // 2
        a_t = (t // n) * n
        mid = a_t + half - 1
        up_t = (t - a_t) >= half
        up_u = (u - (u // n) * n) >= half
        dq.append(up_t & (u > mid) & (u <= t))
        dk.append((~up_t) & (u > t) & (u <= mid))
        masks.append(((t // n) == (u // n)) & up_t & (~up_u))
        n //= 2
    masks.append(t == u)
    cm = np.concatenate(dq + dk + [u <= t, u > t], axis=0).astype(np.float32)
    return cm, np.stack(masks).astype(np.float32)


HG_LEVELS = 7


def _hgrn_kernel(lbl_ref, q_ref, f_ref, i_ref, g_ref, cm_ref, mask_ref, ng_ref, o_ref, st_ref,
                 ee_ref, *, layer, nchunk):
    @pl.when(pl.program_id(1) == 0)
    def _():
        st_ref[...] = jnp.zeros_like(st_ref)

    lg = lbl_ref[...]
    e = jnp.exp(lg - jnp.max(lg, axis=0, keepdims=True))
    sm = e / jnp.sum(e, axis=0, keepdims=True)
    lb = jnp.sum(sm[0:layer + 1], axis=0, keepdims=True) - sm[0:1]
    log_lb = jnp.log(jnp.maximum(lb, LB_FLOOR))
    log_1m = jnp.log(1.0 - lb)
    cm = cm_ref[...]
    nl = HG_LEVELS * CHUNK

    def chunk_body(c, carry):
        r0 = pl.multiple_of(c * CHUNK, CHUNK)
        rows = pl.ds(r0, CHUNK)
        z = f_ref[rows, :].astype(F32)
        bb = log_1m + _log_sigmoid(z)
        lf = jnp.maximum(log_lb, bb) + jnp.log(1.0 + jnp.exp(-jnp.abs(log_lb - bb)))
        kin_all = (1.0 - lb) * _sigmoid(-z)
        ee_ref[...] = jnp.exp(_sel_dot2(cm, lf))
        for h in range(HEADS4):
            hs = slice(h * HEAD_DIM, (h + 1) * HEAD_DIM)
            q = q_ref[rows, hs].astype(F32)
            iv = i_ref[rows, hs].astype(F32)
            gv = g_ref[rows, hs].astype(F32)
            kin = kin_all[:, hs]
            i_act = _silu(iv).astype(BF16)
            scores = mask_ref[HG_LEVELS] * _dot_nt(q.astype(BF16), kin.astype(BF16))
            for l in range(HG_LEVELS):
                ql = (q * ee_ref[l * CHUNK:(l + 1) * CHUNK, hs]).astype(BF16)
                kl = (kin * ee_ref[nl + l * CHUNK:nl + (l + 1) * CHUNK, hs]).astype(BF16)
                scores = scores + mask_ref[l] * _dot_nt(ql, kl)
            eb = ee_ref[2 * nl:2 * nl + CHUNK, hs]
            esuf = ee_ref[2 * nl + CHUNK:2 * nl + 2 * CHUNK, hs]
            st = st_ref[h]
            o = _dot(scores.astype(BF16), i_act) + _dot_nt((q * eb).astype(BF16), st.astype(BF16))
            st_ref[h] = st * eb[CHUNK - 1:CHUNK, :] + _dot_tn(i_act, (kin * esuf).astype(BF16))
            ms = jnp.mean(o * o, axis=-1, keepdims=True)
            y = o * lax.rsqrt(ms + NORM_EPS) * ng_ref[:, hs] * _silu(gv)
            o_ref[rows, hs] = y.astype(o_ref.dtype)
        return carry

    lax.fori_loop(0, nchunk, chunk_body, 0, unroll=2)


def _hgrn(proj, lb_logits, norm_g, layer, bsz, seq):
    t = proj.shape[0]
    ts = min(512, seq)
    nt = seq // ts
    cm, masks = _hgrn_consts()
    w = BRANCH_WIDTH
    col = lambda c: (lambda b, i: (b * nt + i, c // w))
    return pl.pallas_call(
        functools.partial(_hgrn_kernel, layer=layer, nchunk=ts // CHUNK),
        grid=(bsz, nt),
        in_specs=[
            pl.BlockSpec(lb_logits.shape, lambda b, i: (0, 0)),
            pl.BlockSpec((ts, w), col(COL_HGQ)),
            pl.BlockSpec((ts, w), col(COL_HGF)),
            pl.BlockSpec((ts, w), col(COL_HGI)),
            pl.BlockSpec((ts, w), col(COL_HGG)),
            pl.BlockSpec(cm.shape, lambda b, i: (0, 0)),
            pl.BlockSpec(masks.shape, lambda b, i: (0, 0, 0)),
            pl.BlockSpec((1, w), lambda b, i: (0, 0)),
        ],
        out_specs=pl.BlockSpec((ts, w), lambda b, i: (b * nt + i, 0)),
        out_shape=jax.ShapeDtypeStruct((t, w), BF16),
        scratch_shapes=[pltpu.VMEM((HEADS4, HEAD_DIM, HEAD_DIM), F32),
                        pltpu.VMEM(((2 * HG_LEVELS + 2) * CHUNK, w), F32)],
        compiler_params=_params("parallel", "arbitrary"),
        name="hgrn2",
    )(lb_logits, proj, proj, proj, proj, jnp.asarray(cm, BF16), jnp.asarray(masks, F32),
      norm_g.reshape(1, w))


def _mlstm_kernel(q_ref, k_ref, v_ref, og_ref, gc_ref, gr_ref, cw_ref, cb_ref, brow_ref, bcol_ref,
                  ng_ref, o_ref, qbuf, kbuf, qact, kact, ct_ref, m_ref, *, ts):
    nchunk = ts // CHUNK
    w = BRANCH_WIDTH

    @pl.when(pl.program_id(1) == 0)
    def _():
        qbuf[0:8, :] = jnp.zeros((8, w), F32)
        kbuf[0:8, :] = jnp.zeros((8, w), F32)
        ct_ref[...] = jnp.zeros_like(ct_ref)
        m_ref[...] = jnp.zeros_like(m_ref)

    qbuf[8:8 + ts, :] = q_ref[...].astype(F32)
    kbuf[8:8 + ts, :] = k_ref[...].astype(F32)

    rs = lax.broadcasted_iota(jnp.int32, (CHUNK, CHUNK), 0)
    cs = lax.broadcasted_iota(jnp.int32, (CHUNK, CHUNK), 1)
    causal = cs <= rs
    tri_c = jnp.where(causal, 1.0, 0.0).astype(BF16)
    tri_r = jnp.where(rs <= cs, 1.0, 0.0).astype(BF16)
    ones_v = jnp.ones((CHUNK, HEAD_DIM), F32)
    kscale = HEAD_DIM ** -0.5

    for cc in range(nchunk):
        yq = jnp.zeros((CHUNK, w), F32) + cb_ref[:, 0:w]
        yk = jnp.zeros((CHUNK, w), F32) + cb_ref[:, w:2 * w]
        for j in range(ML_CONV):
            lo = cc * CHUNK + 8 - (ML_CONV - 1 - j)
            yq = yq + cw_ref[j:j + 1, 0:w] * qbuf[lo:lo + CHUNK, :]
            yk = yk + cw_ref[j:j + 1, w:2 * w] * kbuf[lo:lo + CHUNK, :]
        qact[cc * CHUNK:(cc + 1) * CHUNK, :] = _silu(yq).astype(BF16)
        kact[cc * CHUNK:(cc + 1) * CHUNK, :] = (_silu(yk) * kscale).astype(BF16)

    def chunk_body(c, carry):
        r0 = pl.multiple_of(c * CHUNK, CHUNK)
        rows = pl.ds(r0, CHUNK)
        pre_c = gc_ref[rows, :] + brow_ref[...]
        pre_r = gr_ref[:, rows] + bcol_ref[...]
        b_c = _sel_dot(tri_c, _log_sigmoid(pre_c))
        b_r = _dot_sel(_log_sigmoid(pre_r), tri_r)
        for h in range(HEADS4):
            hs = slice(h * HEAD_DIM, (h + 1) * HEAD_DIM)
            qh = qact[rows, hs]
            kh = kact[rows, hs]
            vx = jnp.concatenate([v_ref[rows, hs].astype(F32), ones_v], axis=1)
            bcol = b_c[:, HEADS4 + h:HEADS4 + h + 1]
            icol = pre_c[:, h:h + 1]
            brow = b_r[HEADS4 + h:HEADS4 + h + 1, :]
            irow = pre_r[h:h + 1, :]
            m = m_ref[h:h + 1, 0:1]
            log_d = jnp.where(causal, bcol - brow + irow, MASK_NEG)
            log_inter = bcol + m
            m_t = jnp.maximum(log_inter, jnp.max(log_d, axis=-1, keepdims=True))
            dmat = jnp.where(causal, jnp.exp(log_d - m_t), 0.0)
            inter = jnp.exp(log_inter - m_t)
            qk = _dot_nt(qh, kh) * dmat
            ct = ct_ref[h]
            hx = _dot(qk.astype(BF16), vx.astype(BF16)) + inter * _dot(qh, ct.astype(BF16))
            hout = hx[:, 0:HEAD_DIM] / jnp.maximum(jnp.abs(hx[:, HEAD_DIM:]), jnp.exp(-m_t))
            b_last = bcol[CHUNK - 1:CHUNK, :]
            m_new = jnp.maximum(b_last + m, jnp.max(b_last - brow + irow, axis=-1, keepdims=True))
            w_col = jnp.exp(b_last - bcol + icol - m_new)
            dec = jnp.exp(b_last + m - m_new)
            ct_ref[h] = dec * ct + _dot_tn(kh, (vx * w_col).astype(BF16))
            m_ref[h:h + 1, :] = jnp.broadcast_to(m_new, (1, 128))
            ms = jnp.mean(hout * hout, axis=-1, keepdims=True)
            y = hout * lax.rsqrt(ms + NORM_EPS) * ng_ref[:, hs] * _sigmoid(og_ref[rows, hs].astype(F32))
            o_ref[rows, hs] = y.astype(o_ref.dtype)
        return carry

    lax.fori_loop(0, nchunk, chunk_body, 0, unroll=2)
    qbuf[0:8, :] = qbuf[ts:ts + 8, :]
    kbuf[0:8, :] = kbuf[ts:ts + 8, :]


def _mlstm(proj, gates, gates_t, conv_w, conv_b, gate_b, norm_g, bsz, seq):
    t = proj.shape[0]
    ts = min(512, seq)
    nt = seq // ts
    w = BRANCH_WIDTH
    col = lambda c: (lambda b, i: (b * nt + i, c // w))
    brow = jnp.zeros((1, 128), F32).at[0, 0:8].set(gate_b)
    bcol = gate_b.reshape(8, 1)
    return pl.pallas_call(
        functools.partial(_mlstm_kernel, ts=ts),
        grid=(bsz, nt),
        in_specs=[
            pl.BlockSpec((ts, w), col(COL_MLQ)),
            pl.BlockSpec((ts, w), col(COL_MLK)),
            pl.BlockSpec((ts, w), col(COL_MLV)),
            pl.BlockSpec((ts, w), col(COL_MLO)),
            pl.BlockSpec((ts, 128), lambda b, i: (b * nt + i, 0)),
            pl.BlockSpec((8, ts), lambda b, i: (0, b * nt + i)),
            pl.BlockSpec((ML_CONV, 2 * w), lambda b, i: (0, 0)),
            pl.BlockSpec((1, 2 * w), lambda b, i: (0, 0)),
            pl.BlockSpec((1, 128), lambda b, i: (0, 0)),
            pl.BlockSpec((8, 1), lambda b, i: (0, 0)),
            pl.BlockSpec((1, w), lambda b, i: (0, 0)),
        ],
        out_specs=pl.BlockSpec((ts, w), lambda b, i: (b * nt + i, 0)),
        out_shape=jax.ShapeDtypeStruct((t, w), BF16),
        scratch_shapes=[
            pltpu.VMEM((ts + 8, w), F32),
            pltpu.VMEM((ts + 8, w), F32),
            pltpu.VMEM((ts, w), BF16),
            pltpu.VMEM((ts, w), BF16),
            pltpu.VMEM((HEADS4, HEAD_DIM, 2 * HEAD_DIM), F32),
            pltpu.VMEM((8, 128), F32),
        ],
        compiler_params=_params("parallel", "arbitrary"),
        name="mlstm",
    )(proj, proj, proj, proj, gates, gates_t, conv_w, conv_b.reshape(1, 2 * w), brow, bcol,
      norm_g.reshape(1, w))


def _merge_kernel(x_ref, ya_ref, yb_ref, yc_ref, bg_ref, wb_ref, wo_ref, g1_ref, o_ref):
    d = D_MODEL
    merged = jnp.zeros(x_ref.shape, F32)
    for g, y_ref in enumerate((ya_ref, yb_ref, yc_ref)):
        up = _dot(y_ref[...], wb_ref[g])
        merged = merged + _sigmoid(bg_ref[:, g * d:(g + 1) * d].astype(F32)) * up
    y = _dot(merged.astype(BF16), wo_ref[...])
    o_ref[...] = x_ref[...] + g1_ref[...] * y


def _merge(x2, ya, yb, yc, proj, w_branch, w_out, gate1, seq):
    t, d = x2.shape
    tm = min(512, seq)
    per_b = seq // tm
    w = BRANCH_WIDTH
    row = lambda i: (i, 0)
    return pl.pallas_call(
        _merge_kernel,
        grid=(t // tm,),
        in_specs=[
            pl.BlockSpec((tm, d), row),
            pl.BlockSpec((tm, w), row),
            pl.BlockSpec((tm, w), row),
            pl.BlockSpec((tm, w), row),
            pl.BlockSpec((tm, 3 * d), lambda i: (i, COL_BRG // (3 * d))),
            pl.BlockSpec((3, w, d), lambda i: (0, 0, 0)),
            pl.BlockSpec((d, d), lambda i: (0, 0)),
            pl.BlockSpec((None, 1, d), lambda i: (i // per_b, 0, 0)),
        ],
        out_specs=pl.BlockSpec((tm, d), row),
        out_shape=jax.ShapeDtypeStruct((t, d), F32),
        compiler_params=_params("parallel"),
        name="merge",
    )(x2, ya, yb, yc, proj, w_branch, w_out, gate1)


def _fold_keys_kernel(keys_ref, wq_ref, o_ref):
    o_ref[...] = lax.dot_general(keys_ref[...], wq_ref[...], (((1,), (1,)), ((), ())),
                                 preferred_element_type=F32, precision=lax.Precision.HIGHEST)


def _fold_keys(pk_keys, pk_wq):
    d = pk_wq.shape[0]
    sets = 2 * PK_HEADS
    keys = pk_keys.reshape(sets, PK_NKEYS, PK_HALF)
    wq = pk_wq.reshape(d, sets, PK_HALF).transpose(1, 0, 2)
    out = pl.pallas_call(
        _fold_keys_kernel,
        grid=(sets,),
        in_specs=[
            pl.BlockSpec((None, PK_NKEYS, PK_HALF), lambda s: (s, 0, 0)),
            pl.BlockSpec((None, d, PK_HALF), lambda s: (s, 0, 0)),
        ],
        out_specs=pl.BlockSpec((None, PK_NKEYS, d), lambda s: (s, 0, 0)),
        out_shape=jax.ShapeDtypeStruct((sets, PK_NKEYS, d), F32),
        compiler_params=_params("parallel"),
        name="fold_keys",
    )(keys, wq)
    return out.reshape(sets * PK_NKEYS, d)


def _peer_scores_kernel(x_ref, g_ref, sc_ref, sh_ref, wk_ref, ht_ref, st_ref):
    h = _rms_mod(x_ref[...], g_ref[...], sc_ref[...], sh_ref[...])
    ht = h.T.astype(BF16)
    ht_ref[...] = ht
    st_ref[...] = _dot(wk_ref[...], ht)


def _peer_scores(x2, g, scale, shift, wk_t, seq):
    t, d = x2.shape
    tb = min(512, seq)
    per_b = seq // tb
    nk = wk_t.shape[0]
    return pl.pallas_call(
        _peer_scores_kernel,
        grid=(t // tb,),
        in_specs=[
            pl.BlockSpec((tb, d), lambda i: (i, 0)),
            pl.BlockSpec((1, d), lambda i: (0, 0)),
            pl.BlockSpec((None, 1, d), lambda i: (i // per_b, 0, 0)),
            pl.BlockSpec((None, 1, d), lambda i: (i // per_b, 0, 0)),
            pl.BlockSpec((nk, d), lambda i: (0, 0)),
        ],
        out_specs=[
            pl.BlockSpec((d, tb), lambda i: (0, i)),
            pl.BlockSpec((nk, tb), lambda i: (0, i)),
        ],
        out_shape=[jax.ShapeDtypeStruct((d, t), BF16), jax.ShapeDtypeStruct((nk, t), F32)],
        compiler_params=_params("parallel"),
        name="peer_scores",
    )(x2, g.reshape(1, d), scale, shift, wk_t)


NEG_INF = float("-inf")
PK_ROUNDS = PK_TOPK + 1
PK_RANK_NONE = 32.0


def _top_rows(x, rounds, want_rank=False):
    vals = []
    rank = jnp.full(x.shape, PK_RANK_NONE, F32)
    for r in range(rounds):
        m = jnp.max(x, axis=0, keepdims=True)
        vals.append(m)
        eq = x == m
        if want_rank:
            rank = jnp.where(eq, float(r), rank)
        if r + 1 < rounds:
            x = jnp.where(eq, NEG_INF, x)
    return (vals, rank) if want_rank else vals


def _peer_stats_kernel(st_ref, cnt_ref, e1_ref, r2_ref, e2_ref):
    tb = st_ref.shape[-1]
    for h in range(PK_HEADS):
        s1 = st_ref[h, 0]
        s2 = st_ref[h, 1]
        v1 = _top_rows(s1, PK_ROUNDS)
        v2, rank2 = _top_rows(s2, PK_ROUNDS, want_rank=True)
        v2_16 = jnp.concatenate(v2[0:16], axis=0)
        blocks = [v1[0] + v2_16]
        for a in range(1, 8):
            blocks.append(v1[a] + v2_16[0:8])
        blocks.append(jnp.concatenate(v1[8:16], axis=0) + v2[0])
        blocks.append(jnp.concatenate(
            [v1[0] + v2[16], v1[16] + v2[0], jnp.full((6, tb), NEG_INF, F32)], axis=0))
        cand = jnp.concatenate(blocks, axis=0)
        top = _top_rows(cand, PK_ROUNDS)
        thr = 0.5 * (top[PK_TOPK - 1] + top[PK_TOPK])
        zsum = jnp.zeros((1, tb), F32)
        for r in range(PK_TOPK):
            zsum = zsum + jnp.exp(top[r] - top[0])
        need = thr - s1
        cnt = jnp.zeros(s1.shape, F32)
        for b in range(PK_TOPK):
            cnt = cnt + jnp.where(v2[b] >= need, 1.0, 0.0)
        cnt_ref[h] = cnt
        e1_ref[h] = jnp.exp(s1 - v1[0]) / zsum
        r2_ref[h] = rank2.astype(BF16)
        e2_ref[h] = jnp.exp(s2 - v2[0]).astype(BF16)


def _peer_stats(st):
    nk, t = st.shape
    tb = min(256, t)
    st4 = st.reshape(PK_HEADS, 2, PK_NKEYS, t)
    shp = lambda dt: jax.ShapeDtypeStruct((PK_HEADS, PK_NKEYS, t), dt)
    spec = pl.BlockSpec((PK_HEADS, PK_NKEYS, tb), lambda i: (0, 0, i))
    return pl.pallas_call(
        _peer_stats_kernel,
        grid=(t // tb,),
        in_specs=[pl.BlockSpec((PK_HEADS, 2, PK_NKEYS, tb), lambda i: (0, 0, 0, i))],
        out_specs=[spec, spec, spec, spec],
        out_shape=[shp(F32), shp(F32), shp(BF16), shp(BF16)],
        compiler_params=_params("parallel"),
        name="peer_stats",
    )(st4)


PK_ITILE = 8
PK_DENSE_TB = 512
GELU_C0 = 0.7978845608028654
GELU_C1 = 0.044715


def _gelu_tanh(a):
    return 0.5 * a * (1.0 + jnp.tanh(GELU_C0 * (a + GELU_C1 * (a * a * a))))


def _peer_tile_weights(a, row0, cnt_ref, e1_ref, r2_ref, e2_ref, p_ref):
    tb = a.shape[1]
    shape = (PK_NKEYS, tb)
    zero = jnp.zeros(shape, BF16)
    for il in range(PK_ITILE):
        g = _gelu_tanh(a[il * PK_NKEYS:(il + 1) * PK_NKEYS, :].astype(BF16))
        wsum = zero
        for h in range(PK_HEADS):
            cb = jnp.broadcast_to(cnt_ref[h, row0 + il:row0 + il + 1, :], shape).astype(BF16)
            eb = jnp.broadcast_to(e1_ref[h, row0 + il:row0 + il + 1, :], shape).astype(BF16)
            wsum = wsum + eb * jnp.where(cb > r2_ref[h], e2_ref[h], zero)
        p_ref[il * PK_NKEYS:(il + 1) * PK_NKEYS, :] = g * wsum


def _peer_dense_kernel(x_ref, g2_ref, ht_ref, u0_ref, ua_ref, ub_ref, vt_ref, cnt_ref, e1_ref,
                       r2_ref, e2_ref, o_ref, acc_ref, a_ref, pa_ref, pb_ref):
    g = pl.program_id(1)
    te = pa_ref.shape[0]
    ht = ht_ref[...]

    @pl.when(g == 0)
    def _():
        acc_ref[...] = jnp.zeros_like(acc_ref)
        a_ref[0] = _dot(u0_ref[...], ht)

    _peer_tile_weights(a_ref.at[g % 2], 0, cnt_ref, e1_ref, r2_ref, e2_ref, pa_ref)
    a_odd = _dot(ua_ref[...], ht)
    acc_ref[...] += _dot(vt_ref[0], pa_ref[...])
    _peer_tile_weights(a_odd, PK_ITILE, cnt_ref, e1_ref, r2_ref, e2_ref, pb_ref)
    acc_ref[...] += _dot(vt_ref[1], pb_ref[...])
    a_ref[(g + 1) % 2] = _dot(ub_ref[...], ht)

    @pl.when(g == pl.num_programs(1) - 1)
    def _():
        o_ref[...] = x_ref[...] + g2_ref[...] * acc_ref[...].T


def _peer_dense(x2, gate2, ht, u_bf, vt_bf, cnt, e1, r2, e2, seq):
    t, d = x2.shape
    tb = min(PK_DENSE_TB, seq)
    per_b = seq // tb
    te = PK_ITILE * PK_NKEYS
    ntile = u_bf.shape[0] // te
    u_spec = lambda f: pl.BlockSpec((te, d), lambda i, g: (f(g), 0))
    pair_i = pl.BlockSpec((PK_HEADS, 2 * PK_ITILE, tb), lambda i, g: (0, g, i))
    full_j = pl.BlockSpec((PK_HEADS, PK_NKEYS, tb), lambda i, g: (0, 0, i))
    return pl.pallas_call(
        _peer_dense_kernel,
        grid=(t // tb, ntile // 2),
        in_specs=[
            pl.BlockSpec((tb, d), lambda i, g: (i, 0)),
            pl.BlockSpec((None, 1, d), lambda i, g: (i // per_b, 0, 0)),
            pl.BlockSpec((d, tb), lambda i, g: (0, i)),
            u_spec(lambda g: 0), u_spec(lambda g: 2 * g + 1),
            u_spec(lambda g: jnp.minimum(2 * g + 2, ntile - 1)),
            pl.BlockSpec((2, d, te), lambda i, g: (g, 0, 0)),
            pair_i, pair_i, full_j, full_j,
        ],
        out_specs=pl.BlockSpec((tb, d), lambda i, g: (i, 0)),
        out_shape=jax.ShapeDtypeStruct((t, d), F32),
        scratch_shapes=[pltpu.VMEM((d, tb), F32), pltpu.VMEM((2, te, tb), F32),
                        pltpu.VMEM((te, tb), BF16), pltpu.VMEM((te, tb), BF16)],
        compiler_params=_params("parallel", "arbitrary"),
        name="peer_dense",
    )(x2, gate2, ht, u_bf, u_bf, u_bf, vt_bf, cnt, e1, r2, e2)


def _final_norm_kernel(x_ref, g_ref, o_ref):
    x = x_ref[...]
    ms = jnp.mean(x * x, axis=-1, keepdims=True)
    o_ref[...] = x * lax.rsqrt(ms + NORM_EPS) * g_ref[...]


def _final_norm(x2, g):
    t, d = x2.shape
    tm = min(1024, t)
    return pl.pallas_call(
        _final_norm_kernel,
        grid=(t // tm,),
        in_specs=[pl.BlockSpec((tm, d), lambda i: (i, 0)), pl.BlockSpec((1, d), lambda i: (0, 0))],
        out_specs=pl.BlockSpec((tm, d), lambda i: (i, 0)),
        out_shape=jax.ShapeDtypeStruct((t, d), F32),
        compiler_params=_params("parallel"),
        name="final_norm",
    )(x2, g.reshape(1, d))


def _permute_w_in(w_in):
    main = jnp.concatenate([w_in[..., 5640:8712], w_in[..., 3584:4608], w_in[..., 0:3584],
                            w_in[..., 4608:5632]], axis=-1).astype(BF16)
    gates = jnp.pad(w_in[..., 5632:5640], ((0, 0), (0, 0), (0, 120))).astype(BF16)
    return main, gates


def kernel(x, c, mod_w, mod_b, norm_mix_g, norm_ffn_g, w_in, ml_conv_w, ml_conv_b, ml_gate_b,
           hg_lb_logits, hg_norm_g, ml_norm_g, w_branch, w_out, pk_wq, pk_keys, pk_u, pk_v, final_g):
    bsz, seq, d = x.shape
    depth = w_in.shape[0]
    t = bsz * seq
    x2 = x.reshape(t, d)

    mod = _modulation(c, mod_w, mod_b).reshape(depth, bsz, 6, 1, d)
    w_main, w_gates = _permute_w_in(w_in)
    w_branch_bf = w_branch.astype(BF16)
    w_out_bf = w_out.astype(BF16)
    u_bf = pk_u.astype(BF16)
    te = PK_ITILE * PK_NKEYS
    vt_bf = jnp.swapaxes(pk_v.astype(BF16).reshape(depth, pk_v.shape[1] // te, te, d), 2, 3)

    for l in range(depth):
        shift1, scale1, gate1, shift2, scale2, gate2 = (mod[l, :, r] for r in range(6))
        proj, gates = _norm_proj(x2, norm_mix_g[l], scale1, shift1, w_main[l], w_gates[l], seq)
        ya = _sb_attention(proj, bsz, seq)
        yb = _hgrn(proj, hg_lb_logits, hg_norm_g[l], l, bsz, seq)
        yc = _mlstm(proj, gates, gates[:, 0:8].T, ml_conv_w[l], ml_conv_b[l], ml_gate_b[l],
                    ml_norm_g[l], bsz, seq)
        x2 = _merge(x2, ya, yb, yc, proj, w_branch_bf[l], w_out_bf[l], gate1, seq)
        wk_t = _fold_keys(pk_keys[l], pk_wq[l]).astype(BF16)
        ht, st = _peer_scores(x2, norm_ffn_g[l], scale2, shift2, wk_t, seq)
        cnt, e1, r2, e2 = _peer_stats(st)
        x2 = _peer_dense(x2, gate2, ht, u_bf[l], vt_bf[l], cnt, e1, r2, e2, seq)
    return _final_norm(x2, final_g).reshape(bsz, seq, d)
```

```python
import functools

import numpy as np
import jax
import jax.numpy as jnp
from jax import lax
from jax.experimental import pallas as pl
from jax.experimental.pallas import tpu as pltpu

F32 = jnp.float32
BF16 = jnp.bfloat16

D_MODEL = 1024
DEPTH = 4
NORM_EPS = 1e-6
MASK_NEG = -1e30
LB_FLOOR = 1e-30
SB_HEAD_DIM = 64
HEADS4 = 4
HEAD_DIM = 128
BRANCH_WIDTH = 512
ML_CONV = 4
PK_HEADS = 8
PK_NKEYS = 128
PK_HALF = 64
PK_TOPK = 16

PROJ_WIDTH = 8704
COL_BRG = 0
COL_MLQ = 3072
COL_MLK = 3584
COL_SBQ = 4096
COL_SBK = 4608
COL_SBV = 5120
COL_HGQ = 5632
COL_HGF = 6144
COL_HGI = 6656
COL_HGG = 7168
COL_MLV = 7680
COL_MLO = 8192

CHUNK = 128
VMEM_LIMIT = 56 * 1024 * 1024


def _dot(a, b):
    return jnp.dot(a, b, preferred_element_type=F32)


def _dot_nt(a, b):
    return lax.dot_general(a, b, (((1,), (1,)), ((), ())), preferred_element_type=F32)


def _dot_tn(a, b):
    return lax.dot_general(a, b, (((0,), (0,)), ((), ())), preferred_element_type=F32)


def _split3(x):
    hi = x.astype(BF16)
    r = x - hi.astype(F32)
    mid = r.astype(BF16)
    lo = (r - mid.astype(F32)).astype(BF16)
    return hi, mid, lo


def _sel_dot(m01, x):
    hi, mid, lo = _split3(x)
    return _dot(m01, hi) + _dot(m01, mid) + _dot(m01, lo)


def _dot_sel(x, m01):
    hi, mid, lo = _split3(x)
    return _dot(hi, m01) + _dot(mid, m01) + _dot(lo, m01)


def _sel_dot2(m01, x):
    hi = x.astype(BF16)
    lo = (x - hi.astype(F32)).astype(BF16)
    return _dot(m01, hi) + _dot(m01, lo)


def _sigmoid(x):
    return 1.0 / (1.0 + jnp.exp(-x))


def _silu(x):
    return x * _sigmoid(x)


def _log_sigmoid(x):
    return jnp.minimum(x, 0.0) - jnp.log(1.0 + jnp.exp(-jnp.abs(x)))


def _params(*sem):
    return pltpu.CompilerParams(dimension_semantics=sem, vmem_limit_bytes=VMEM_LIMIT)


def _mod_kernel(c_ref, w_ref, b_ref, o_ref):
    cond = _silu(c_ref[...])
    o_ref[...] = jnp.dot(cond, w_ref[...], preferred_element_type=F32,
                         precision=lax.Precision.HIGHEST) + b_ref[...]


def _modulation(c, mod_w, mod_b):
    depth, d, n = mod_w.shape
    bsz = c.shape[0]
    tn = d
    return pl.pallas_call(
        _mod_kernel,
        grid=(depth, n // tn),
        in_specs=[
            pl.BlockSpec((bsz, d), lambda l, j: (0, 0)),
            pl.BlockSpec((None, d, tn), lambda l, j: (l, 0, j)),
            pl.BlockSpec((None, 1, tn), lambda l, j: (l, 0, j)),
        ],
        out_specs=pl.BlockSpec((None, bsz, tn), lambda l, j: (l, 0, j)),
        out_shape=jax.ShapeDtypeStruct((depth, bsz, n), F32),
        compiler_params=_params("parallel", "parallel"),
        name="modulation",
    )(c, mod_w, mod_b.reshape(depth, 1, n))


def _rms_mod(x, g, scale, shift):
    ms = jnp.mean(x * x, axis=-1, keepdims=True)
    y = x * lax.rsqrt(ms + NORM_EPS) * g
    return y * (1.0 + scale) + shift


def _norm_proj_kernel(x_ref, g_ref, sc_ref, sh_ref, w_ref, wg_ref, proj_ref, gates_ref, h_ref):
    @pl.when(pl.program_id(1) == 0)
    def _():
        hb = _rms_mod(x_ref[...], g_ref[...], sc_ref[...], sh_ref[...]).astype(BF16)
        h_ref[...] = hb
        gates_ref[...] = _dot(hb, wg_ref[...])

    proj_ref[...] = _dot(h_ref[...], w_ref[...]).astype(BF16)


def _norm_proj(x2, g, scale, shift, w_main, w_gates, seq):
    t, d = x2.shape
    tm = min(1024, seq)
    tn = PROJ_WIDTH // 4
    per_b = seq // tm
    n = w_main.shape[1]
    return pl.pallas_call(
        _norm_proj_kernel,
        grid=(t // tm, n // tn),
        in_specs=[
            pl.BlockSpec((tm, d), lambda i, j: (i, 0)),
            pl.BlockSpec((1, d), lambda i, j: (0, 0)),
            pl.BlockSpec((None, 1, d), lambda i, j: (i // per_b, 0, 0)),
            pl.BlockSpec((None, 1, d), lambda i, j: (i // per_b, 0, 0)),
            pl.BlockSpec((d, tn), lambda i, j: (0, j)),
            pl.BlockSpec((d, 128), lambda i, j: (0, 0)),
        ],
        out_specs=[
            pl.BlockSpec((tm, tn), lambda i, j: (i, j)),
            pl.BlockSpec((tm, 128), lambda i, j: (i, 0)),
        ],
        out_shape=[jax.ShapeDtypeStruct((t, n), BF16), jax.ShapeDtypeStruct((t, 128), F32)],
        scratch_shapes=[pltpu.VMEM((tm, d), BF16)],
        compiler_params=_params("parallel", "arbitrary"),
        name="norm_proj",
    )(x2, g.reshape(1, d), scale, shift, w_main, w_gates)


SB_TQ = 256
SB_TK = 128
SB_PAIRS = 4
SB_UNROLL = 2
assert SB_UNROLL * SB_TK == SB_TQ
LOG2_E = 1.4426950408889634


def _sb_kernel(q_ref, k_ref, v_ref, o_ref):
    qi = pl.program_id(2)
    lane = lax.broadcasted_iota(jnp.int32, (1, 128), 1)
    first = lane < SB_HEAD_DIM
    qh = []
    for p in range(SB_PAIRS):
        q2 = q_ref[:, p * 128:(p + 1) * 128] * (SB_HEAD_DIM ** -0.5 * LOG2_E)
        zero = jnp.zeros_like(q2)
        qh.append((jnp.where(first, q2, zero), jnp.where(first, zero, q2)))
    t_idx = qi * SB_TQ + lax.broadcasted_iota(jnp.int32, (SB_TQ, 1), 0)
    rs = lax.broadcasted_iota(jnp.int32, (SB_TK, SB_TK), 0)
    cs = lax.broadcasted_iota(jnp.int32, (SB_TK, SB_TK), 1)
    tri = jnp.where(rs > cs, 1.0, 0.0).astype(BF16)
    tri_ext = jnp.concatenate([tri, jnp.ones_like(tri)], axis=1)
    nkb = (qi + 1) * (SB_TQ // SB_TK)

    def body(it, carry, masked):
        cc = list(carry[0])
        acc = list(carry[1])
        for sub in range(SB_UNROLL):
            kb = nkb - 1 - (it * SB_UNROLL + sub)
            r0 = pl.multiple_of(kb * SB_TK, SB_TK)
            past = (kb * SB_TK + lane) < t_idx
            for p in range(SB_PAIRS):
                ks = k_ref[pl.ds(r0, SB_TK), p * 128:(p + 1) * 128]
                vs = v_ref[pl.ds(r0, SB_TK), p * 128:(p + 1) * 128]
                vzero = jnp.zeros_like(vs)
                vh = (jnp.where(first, vs, vzero), jnp.where(first, vzero, vs))
                for h in range(2):
                    z = _dot_nt(qh[p][h], ks)
                    ls = jnp.minimum(z, 0.0) - jnp.log2(1.0 + jnp.exp2(-jnp.abs(z)))
                    lk = ls - z
                    if masked:
                        lk = jnp.where(past, lk, 0.0)
                    sums = _dot(lk.astype(BF16), tri_ext)
                    w = jnp.exp2(ls + sums[:, 0:SB_TK] + cc[2 * p + h])
                    if masked:
                        w = jnp.where(past, w, 0.0)
                    acc[p] = acc[p] + _dot(w.astype(BF16), vh[h])
                    cc[2 * p + h] = cc[2 * p + h] + sums[:, SB_TK:]
        return tuple(cc), tuple(acc)

    init = (tuple(jnp.zeros((SB_TQ, SB_TK), F32) for _ in range(2 * SB_PAIRS)),
            tuple(jnp.zeros((SB_TQ, 128), F32) for _ in range(SB_PAIRS)))
    carry = body(0, init, True)
    _, acc = lax.fori_loop(1, nkb // SB_UNROLL, functools.partial(body, masked=False), carry)
    for p in range(SB_PAIRS):
        o_ref[:, p * 128:(p + 1) * 128] = acc[p].astype(o_ref.dtype)


def _sb_attention(proj, bsz, seq):
    t = proj.shape[0]
    nq = seq // SB_TQ
    wd = 128 * SB_PAIRS
    qb, kb, vb = COL_SBQ // wd, COL_SBK // wd, COL_SBV // wd
    return pl.pallas_call(
        _sb_kernel,
        grid=(bsz, BRANCH_WIDTH // wd, nq),
        in_specs=[
            pl.BlockSpec((SB_TQ, wd), lambda b, p, i: (b * nq + i, qb + p)),
            pl.BlockSpec((seq, wd), lambda b, p, i: (b, kb + p)),
            pl.BlockSpec((seq, wd), lambda b, p, i: (b, vb + p)),
        ],
        out_specs=pl.BlockSpec((SB_TQ, wd), lambda b, p, i: (b * nq + i, p)),
        out_shape=jax.ShapeDtypeStruct((t, BRANCH_WIDTH), BF16),
        compiler_params=_params("parallel", "parallel", "arbitrary"),
        name="sb_attention",
    )(proj, proj, proj)


def _hgrn_consts():
    n_l = CHUNK
    t = np.arange(n_l)[:, None]
    u = np.arange(n_l)[None, :]
    dq, dk, masks = [], [], []
    n = n_l
    while n >= 2:
        half = n // 2
        a_t = (t // n) * n
        mid = a_t + half - 1
        up_t = (t - a_t) >= half
        up_u = (u - (u // n) * n) >= half
        dq.append(up_t & (u > mid) & (u <= t))
        dk.append((~up_t) & (u > t) & (u <= mid))
        masks.append(((t // n) == (u // n)) & up_t & (~up_u))
        n //= 2
    masks.append(t == u)
    cm = np.concatenate(dq + dk + [u <= t, u > t], axis=0).astype(np.float32)
    return cm, np.stack(masks).astype(np.float32)


HG_LEVELS = 7


def _hgrn_kernel(lbl_ref, q_ref, f_ref, i_ref, g_ref, cm_ref, mask_ref, ng_ref, o_ref, st_ref,
                 ee_ref, *, layer, nchunk):
    @pl.when(pl.program_id(1) == 0)
    def _():
        st_ref[...] = jnp.zeros_like(st_ref)

    lg = lbl_ref[...]
    e = jnp.exp(lg - jnp.max(lg, axis=0, keepdims=True))
    sm = e / jnp.sum(e, axis=0, keepdims=True)
    lb = jnp.sum(sm[0:layer + 1], axis=0, keepdims=True) - sm[0:1]
    log_lb = jnp.log(jnp.maximum(lb, LB_FLOOR))
    log_1m = jnp.log(1.0 - lb)
    cm = cm_ref[...]
    nl = HG_LEVELS * CHUNK

    def chunk_body(c, carry):
        r0 = pl.multiple_of(c * CHUNK, CHUNK)
        rows = pl.ds(r0, CHUNK)
        z = f_ref[rows, :].astype(F32)
        bb = log_1m + _log_sigmoid(z)
        lf = jnp.maximum(log_lb, bb) + jnp.log(1.0 + jnp.exp(-jnp.abs(log_lb - bb)))
        kin_all = (1.0 - lb) * _sigmoid(-z)
        ee_ref[...] = jnp.exp(_sel_dot2(cm, lf))
        for h in range(HEADS4):
            hs = slice(h * HEAD_DIM, (h + 1) * HEAD_DIM)
            q = q_ref[rows, hs].astype(F32)
            iv = i_ref[rows, hs].astype(F32)
            gv = g_ref[rows, hs].astype(F32)
            kin = kin_all[:, hs]
            i_act = _silu(iv).astype(BF16)
            scores = mask_ref[HG_LEVELS] * _dot_nt(q.astype(BF16), kin.astype(BF16))
            for l in range(HG_LEVELS):
                ql = (q * ee_ref[l * CHUNK:(l + 1) * CHUNK, hs]).astype(BF16)
                kl = (kin * ee_ref[nl + l * CHUNK:nl + (l + 1) * CHUNK, hs]).astype(BF16)
                scores = scores + mask_ref[l] * _dot_nt(ql, kl)
            eb = ee_ref[2 * nl:2 * nl + CHUNK, hs]
            esuf = ee_ref[2 * nl + CHUNK:2 * nl + 2 * CHUNK, hs]
            st = st_ref[h]
            o = _dot(scores.astype(BF16), i_act) + _dot_nt((q * eb).astype(BF16), st.astype(BF16))
            st_ref[h] = st * eb[CHUNK - 1:CHUNK, :] + _dot_tn(i_act, (kin * esuf).astype(BF16))
            ms = jnp.mean(o * o, axis=-1, keepdims=True)
            y = o * lax.rsqrt(ms + NORM_EPS) * ng_ref[:, hs] * _silu(gv)
            o_ref[rows, hs] = y.astype(o_ref.dtype)
        return carry

    lax.fori_loop(0, nchunk, chunk_body, 0, unroll=2)


def _hgrn(proj, lb_logits, norm_g, layer, bsz, seq):
    t = proj.shape[0]
    ts = min(512, seq)
    nt = seq // ts
    cm, masks = _hgrn_consts()
    w = BRANCH_WIDTH
    col = lambda c: (lambda b, i: (b * nt + i, c // w))
    return pl.pallas_call(
        functools.partial(_hgrn_kernel, layer=layer, nchunk=ts // CHUNK),
        grid=(bsz, nt),
        in_specs=[
            pl.BlockSpec(lb_logits.shape, lambda b, i: (0, 0)),
            pl.BlockSpec((ts, w), col(COL_HGQ)),
            pl.BlockSpec((ts, w), col(COL_HGF)),
            pl.BlockSpec((ts, w), col(COL_HGI)),
            pl.BlockSpec((ts, w), col(COL_HGG)),
            pl.BlockSpec(cm.shape, lambda b, i: (0, 0)),
            pl.BlockSpec(masks.shape, lambda b, i: (0, 0, 0)),
            pl.BlockSpec((1, w), lambda b, i: (0, 0)),
        ],
        out_specs=pl.BlockSpec((ts, w), lambda b, i: (b * nt + i, 0)),
        out_shape=jax.ShapeDtypeStruct((t, w), BF16),
        scratch_shapes=[pltpu.VMEM((HEADS4, HEAD_DIM, HEAD_DIM), F32),
                        pltpu.VMEM(((2 * HG_LEVELS + 2) * CHUNK, w), F32)],
        compiler_params=_params("parallel", "arbitrary"),
        name="hgrn2",
    )(lb_logits, proj, proj, proj, proj, jnp.asarray(cm, BF16), jnp.asarray(masks, F32),
      norm_g.reshape(1, w))


def _mlstm_kernel(q_ref, k_ref, v_ref, og_ref, gc_ref, gr_ref, cw_ref, cb_ref, brow_ref, bcol_ref,
                  ng_ref, o_ref, qbuf, kbuf, qact, kact, ct_ref, m_ref, *, ts):
    nchunk = ts // CHUNK
    w = BRANCH_WIDTH

    @pl.when(pl.program_id(1) == 0)
    def _():
        qbuf[0:8, :] = jnp.zeros((8, w), F32)
        kbuf[0:8, :] = jnp.zeros((8, w), F32)
        ct_ref[...] = jnp.zeros_like(ct_ref)
        m_ref[...] = jnp.zeros_like(m_ref)

    qbuf[8:8 + ts, :] = q_ref[...].astype(F32)
    kbuf[8:8 + ts, :] = k_ref[...].astype(F32)

    rs = lax.broadcasted_iota(jnp.int32, (CHUNK, CHUNK), 0)
    cs = lax.broadcasted_iota(jnp.int32, (CHUNK, CHUNK), 1)
    causal = cs <= rs
    tri_c = jnp.where(causal, 1.0, 0.0).astype(BF16)
    tri_r = jnp.where(rs <= cs, 1.0, 0.0).astype(BF16)
    ones_v = jnp.ones((CHUNK, HEAD_DIM), F32)
    kscale = HEAD_DIM ** -0.5

    for cc in range(nchunk):
        yq = jnp.zeros((CHUNK, w), F32) + cb_ref[:, 0:w]
        yk = jnp.zeros((CHUNK, w), F32) + cb_ref[:, w:2 * w]
        for j in range(ML_CONV):
            lo = cc * CHUNK + 8 - (ML_CONV - 1 - j)
            yq = yq + cw_ref[j:j + 1, 0:w] * qbuf[lo:lo + CHUNK, :]
            yk = yk + cw_ref[j:j + 1, w:2 * w] * kbuf[lo:lo + CHUNK, :]
        qact[cc * CHUNK:(cc + 1) * CHUNK, :] = _silu(yq).astype(BF16)
        kact[cc * CHUNK:(cc + 1) * CHUNK, :] = (_silu(yk) * kscale).astype(BF16)

    def chunk_body(c, carry):
        r0 = pl.multiple_of(c * CHUNK, CHUNK)
        rows = pl.ds(r0, CHUNK)
        pre_c = gc_ref[rows, :] + brow_ref[...]
        pre_r = gr_ref[:, rows] + bcol_ref[...]
        b_c = _sel_dot(tri_c, _log_sigmoid(pre_c))
        b_r = _dot_sel(_log_sigmoid(pre_r), tri_r)
        for h in range(HEADS4):
            hs = slice(h * HEAD_DIM, (h + 1) * HEAD_DIM)
            qh = qact[rows, hs]
            kh = kact[rows, hs]
            vx = jnp.concatenate([v_ref[rows, hs].astype(F32), ones_v], axis=1)
            bcol = b_c[:, HEADS4 + h:HEADS4 + h + 1]
            icol = pre_c[:, h:h + 1]
            brow = b_r[HEADS4 + h:HEADS4 + h + 1, :]
            irow = pre_r[h:h + 1, :]
            m = m_ref[h:h + 1, 0:1]
            log_d = jnp.where(causal, bcol - brow + irow, MASK_NEG)
            log_inter = bcol + m
            m_t = jnp.maximum(log_inter, jnp.max(log_d, axis=-1, keepdims=True))
            dmat = jnp.where(causal, jnp.exp(log_d - m_t), 0.0)
            inter = jnp.exp(log_inter - m_t)
            qk = _dot_nt(qh, kh) * dmat
            ct = ct_ref[h]
            hx = _dot(qk.astype(BF16), vx.astype(BF16)) + inter * _dot(qh, ct.astype(BF16))
            hout = hx[:, 0:HEAD_DIM] / jnp.maximum(jnp.abs(hx[:, HEAD_DIM:]), jnp.exp(-m_t))
            b_last = bcol[CHUNK - 1:CHUNK, :]
            m_new = jnp.maximum(b_last + m, jnp.max(b_last - brow + irow, axis=-1, keepdims=True))
            w_col = jnp.exp(b_last - bcol + icol - m_new)
            dec = jnp.exp(b_last + m - m_new)
            ct_ref[h] = dec * ct + _dot_tn(kh, (vx * w_col).astype(BF16))
            m_ref[h:h + 1, :] = jnp.broadcast_to(m_new, (1, 128))
            ms = jnp.mean(hout * hout, axis=-1, keepdims=True)
            y = hout * lax.rsqrt(ms + NORM_EPS) * ng_ref[:, hs] * _sigmoid(og_ref[rows, hs].astype(F32))
            o_ref[rows, hs] = y.astype(o_ref.dtype)
        return carry

    lax.fori_loop(0, nchunk, chunk_body, 0, unroll=2)
    qbuf[0:8, :] = qbuf[ts:ts + 8, :]
    kbuf[0:8, :] = kbuf[ts:ts + 8, :]


def _mlstm(proj, gates, gates_t, conv_w, conv_b, gate_b, norm_g, bsz, seq):
    t = proj.shape[0]
    ts = min(512, seq)
    nt = seq // ts
    w = BRANCH_WIDTH
    col = lambda c: (lambda b, i: (b * nt + i, c // w))
    brow = jnp.zeros((1, 128), F32).at[0, 0:8].set(gate_b)
    bcol = gate_b.reshape(8, 1)
    return pl.pallas_call(
        functools.partial(_mlstm_kernel, ts=ts),
        grid=(bsz, nt),
        in_specs=[
            pl.BlockSpec((ts, w), col(COL_MLQ)),
            pl.BlockSpec((ts, w), col(COL_MLK)),
            pl.BlockSpec((ts, w), col(COL_MLV)),
            pl.BlockSpec((ts, w), col(COL_MLO)),
            pl.BlockSpec((ts, 128), lambda b, i: (b * nt + i, 0)),
            pl.BlockSpec((8, ts), lambda b, i: (0, b * nt + i)),
            pl.BlockSpec((ML_CONV, 2 * w), lambda b, i: (0, 0)),
            pl.BlockSpec((1, 2 * w), lambda b, i: (0, 0)),
            pl.BlockSpec((1, 128), lambda b, i: (0, 0)),
            pl.BlockSpec((8, 1), lambda b, i: (0, 0)),
            pl.BlockSpec((1, w), lambda b, i: (0, 0)),
        ],
        out_specs=pl.BlockSpec((ts, w), lambda b, i: (b * nt + i, 0)),
        out_shape=jax.ShapeDtypeStruct((t, w), BF16),
        scratch_shapes=[
            pltpu.VMEM((ts + 8, w), F32),
            pltpu.VMEM((ts + 8, w), F32),
            pltpu.VMEM((ts, w), BF16),
            pltpu.VMEM((ts, w), BF16),
            pltpu.VMEM((HEADS4, HEAD_DIM, 2 * HEAD_DIM), F32),
            pltpu.VMEM((8, 128), F32),
        ],
        compiler_params=_params("parallel", "arbitrary"),
        name="mlstm",
    )(proj, proj, proj, proj, gates, gates_t, conv_w, conv_b.reshape(1, 2 * w), brow, bcol,
      norm_g.reshape(1, w))


def _merge_kernel(x_ref, ya_ref, yb_ref, yc_ref, bg_ref, wb_ref, wo_ref, g1_ref, o_ref):
    d = D_MODEL
    merged = jnp.zeros(x_ref.shape, F32)
    for g, y_ref in enumerate((ya_ref, yb_ref, yc_ref)):
        up = _dot(y_ref[...], wb_ref[g])
        merged = merged + _sigmoid(bg_ref[:, g * d:(g + 1) * d].astype(F32)) * up
    y = _dot(merged.astype(BF16), wo_ref[...])
    o_ref[...] = x_ref[...] + g1_ref[...] * y


def _merge(x2, ya, yb, yc, proj, w_branch, w_out, gate1, seq):
    t, d = x2.shape
    tm = min(512, seq)
    per_b = seq // tm
    w = BRANCH_WIDTH
    row = lambda i: (i, 0)
    return pl.pallas_call(
        _merge_kernel,
        grid=(t // tm,),
        in_specs=[
            pl.BlockSpec((tm, d), row),
            pl.BlockSpec((tm, w), row),
            pl.BlockSpec((tm, w), row),
            pl.BlockSpec((tm, w), row),
            pl.BlockSpec((tm, 3 * d), lambda i: (i, COL_BRG // (3 * d))),
            pl.BlockSpec((3, w, d), lambda i: (0, 0, 0)),
            pl.BlockSpec((d, d), lambda i: (0, 0)),
            pl.BlockSpec((None, 1, d), lambda i: (i // per_b, 0, 0)),
        ],
        out_specs=pl.BlockSpec((tm, d), row),
        out_shape=jax.ShapeDtypeStruct((t, d), F32),
        compiler_params=_params("parallel"),
        name="merge",
    )(x2, ya, yb, yc, proj, w_branch, w_out, gate1)


def _fold_keys_kernel(keys_ref, wq_ref, o_ref):
    o_ref[...] = lax.dot_general(keys_ref[...], wq_ref[...], (((1,), (1,)), ((), ())),
                                 preferred_element_type=F32, precision=lax.Precision.HIGHEST)


def _fold_keys(pk_keys, pk_wq):
    d = pk_wq.shape[0]
    sets = 2 * PK_HEADS
    keys = pk_keys.reshape(sets, PK_NKEYS, PK_HALF)
    wq = pk_wq.reshape(d, sets, PK_HALF).transpose(1, 0, 2)
    out = pl.pallas_call(
        _fold_keys_kernel,
        grid=(sets,),
        in_specs=[
            pl.BlockSpec((None, PK_NKEYS, PK_HALF), lambda s: (s, 0, 0)),
            pl.BlockSpec((None, d, PK_HALF), lambda s: (s, 0, 0)),
        ],
        out_specs=pl.BlockSpec((None, PK_NKEYS, d), lambda s: (s, 0, 0)),
        out_shape=jax.ShapeDtypeStruct((sets, PK_NKEYS, d), F32),
        compiler_params=_params("parallel"),
        name="fold_keys",
    )(keys, wq)
    return out.reshape(sets * PK_NKEYS, d)


def _peer_scores_kernel(x_ref, g_ref, sc_ref, sh_ref, wk_ref, ht_ref, st_ref):
    h = _rms_mod(x_ref[...], g_ref[...], sc_ref[...], sh_ref[...])
    ht = h.T.astype(BF16)
    ht_ref[...] = ht
    st_ref[...] = _dot(wk_ref[...], ht)


def _peer_scores(x2, g, scale, shift, wk_t, seq):
    t, d = x2.shape
    tb = min(512, seq)
    per_b = seq // tb
    nk = wk_t.shape[0]
    return pl.pallas_call(
        _peer_scores_kernel,
        grid=(t // tb,),
        in_specs=[
            pl.BlockSpec((tb, d), lambda i: (i, 0)),
            pl.BlockSpec((1, d), lambda i: (0, 0)),
            pl.BlockSpec((None, 1, d), lambda i: (i // per_b, 0, 0)),
            pl.BlockSpec((None, 1, d), lambda i: (i // per_b, 0, 0)),
            pl.BlockSpec((nk, d), lambda i: (0, 0)),
        ],
        out_specs=[
            pl.BlockSpec((d, tb), lambda i: (0, i)),
            pl.BlockSpec((nk, tb), lambda i: (0, i)),
        ],
        out_shape=[jax.ShapeDtypeStruct((d, t), BF16), jax.ShapeDtypeStruct((nk, t), F32)],
        compiler_params=_params("parallel"),
        name="peer_scores",
    )(x2, g.reshape(1, d), scale, shift, wk_t)


NEG_INF = float("-inf")
PK_ROUNDS = PK_TOPK + 1
PK_RANK_NONE = 32.0


def _top_rows(x, rounds, want_rank=False):
    vals = []
    rank = jnp.full(x.shape, PK_RANK_NONE, F32)
    for r in range(rounds):
        m = jnp.max(x, axis=0, keepdims=True)
        vals.append(m)
        eq = x == m
        if want_rank:
            rank = jnp.where(eq, float(r), rank)
        if r + 1 < rounds:
            x = jnp.where(eq, NEG_INF, x)
    return (vals, rank) if want_rank else vals


def _count_ge(v, need):
    assert PK_TOPK == 16
    w = jnp.where
    c1 = v[7] >= need
    c2 = w(c1, v[11], v[3]) >= need
    c3 = w(c1, w(c2, v[13], v[9]), w(c2, v[5], v[1])) >= need
    c4 = w(c1, w(c2, w(c3, v[14], v[12]), w(c3, v[10], v[8])),
           w(c2, w(c3, v[6], v[4]), w(c3, v[2], v[0]))) >= need
    c5 = v[15] >= need
    return (w(c1, 8.0, 0.0) + w(c2, 4.0, 0.0) + w(c3, 2.0, 0.0) + w(c4, 1.0, 0.0) + w(c5, 1.0, 0.0))


def _peer_stats_kernel(st_ref, cnt_ref, e1_ref, r2_ref, e2_ref):
    tb = st_ref.shape[-1]
    for h in range(PK_HEADS):
        s1 = st_ref[h, 0]
        s2 = st_ref[h, 1]
        v1 = _top_rows(s1, PK_ROUNDS)
        v2, rank2 = _top_rows(s2, PK_ROUNDS, want_rank=True)
        v2_16 = jnp.concatenate(v2[0:16], axis=0)
        blocks = [v1[0] + v2_16]
        for a in range(1, 8):
            blocks.append(v1[a] + v2_16[0:8])
        blocks.append(jnp.concatenate(v1[8:16], axis=0) + v2[0])
        blocks.append(jnp.concatenate(
            [v1[0] + v2[16], v1[16] + v2[0], jnp.full((6, tb), NEG_INF, F32)], axis=0))
        cand = jnp.concatenate(blocks, axis=0)
        top = _top_rows(cand, PK_ROUNDS)
        thr = 0.5 * (top[PK_TOPK - 1] + top[PK_TOPK])
        zsum = jnp.zeros((1, tb), F32)
        for r in range(PK_TOPK):
            zsum = zsum + jnp.exp(top[r] - top[0])
        need = thr - s1
        cnt_ref[h] = _count_ge(v2, need)
        e1_ref[h] = jnp.exp(s1 - v1[0]) / zsum
        r2_ref[h] = rank2.astype(BF16)
        e2_ref[h] = jnp.exp(s2 - v2[0]).astype(BF16)


def _peer_stats(st):
    nk, t = st.shape
    tb = min(256, t)
    st4 = st.reshape(PK_HEADS, 2, PK_NKEYS, t)
    shp = lambda dt: jax.ShapeDtypeStruct((PK_HEADS, PK_NKEYS, t), dt)
    spec = pl.BlockSpec((PK_HEADS, PK_NKEYS, tb), lambda i: (0, 0, i))
    return pl.pallas_call(
        _peer_stats_kernel,
        grid=(t // tb,),
        in_specs=[pl.BlockSpec((PK_HEADS, 2, PK_NKEYS, tb), lambda i: (0, 0, 0, i))],
        out_specs=[spec, spec, spec, spec],
        out_shape=[shp(F32), shp(F32), shp(BF16), shp(BF16)],
        compiler_params=_params("parallel"),
        name="peer_stats",
    )(st4)


PK_ITILE = 8
PK_DENSE_TB = 512
GELU_C0 = 0.7978845608028654
GELU_C1 = 0.044715


def _gelu_tanh(a):
    return 0.5 * a * (1.0 + jnp.tanh(GELU_C0 * (a + GELU_C1 * (a * a * a))))


def _peer_tile_weights(a, row0, cnt_ref, e1_ref, r2_ref, e2_ref, p_ref):
    tb = a.shape[1]
    shape = (PK_NKEYS, tb)
    zero = jnp.zeros(shape, BF16)
    for il in range(PK_ITILE):
        g = _gelu_tanh(a[il * PK_NKEYS:(il + 1) * PK_NKEYS, :].astype(BF16))
        wsum = zero
        for h in range(PK_HEADS):
            cb = jnp.broadcast_to(cnt_ref[h, row0 + il:row0 + il + 1, :], shape).astype(BF16)
            eb = jnp.broadcast_to(e1_ref[h, row0 + il:row0 + il + 1, :], shape).astype(BF16)
            wsum = wsum + eb * jnp.where(cb > r2_ref[h], e2_ref[h], zero)
        p_ref[il * PK_NKEYS:(il + 1) * PK_NKEYS, :] = g * wsum


def _peer_dense_kernel(x_ref, g2_ref, ht_ref, u0_ref, ua_ref, ub_ref, vt_ref, cnt_ref, e1_ref,
                       r2_ref, e2_ref, o_ref, acc_ref, a_ref, pa_ref, pb_ref):
    g = pl.program_id(1)
    te = pa_ref.shape[0]
    ht = ht_ref[...]

    @pl.when(g == 0)
    def _():
        acc_ref[...] = jnp.zeros_like(acc_ref)
        a_ref[0] = _dot(u0_ref[...], ht)

    _peer_tile_weights(a_ref.at[g % 2], 0, cnt_ref, e1_ref, r2_ref, e2_ref, pa_ref)
    a_odd = _dot(ua_ref[...], ht)
    acc_ref[...] += _dot(vt_ref[0], pa_ref[...])
    _peer_tile_weights(a_odd, PK_ITILE, cnt_ref, e1_ref, r2_ref, e2_ref, pb_ref)
    acc_ref[...] += _dot(vt_ref[1], pb_ref[...])
    a_ref[(g + 1) % 2] = _dot(ub_ref[...], ht)

    @pl.when(g == pl.num_programs(1) - 1)
    def _():
        o_ref[...] = x_ref[...] + g2_ref[...] * acc_ref[...].T


def _peer_dense(x2, gate2, ht, u_bf, vt_bf, cnt, e1, r2, e2, seq):
    t, d = x2.shape
    tb = min(PK_DENSE_TB, seq)
    per_b = seq // tb
    te = PK_ITILE * PK_NKEYS
    ntile = u_bf.shape[0] // te
    u_spec = lambda f: pl.BlockSpec((te, d), lambda i, g: (f(g), 0))
    pair_i = pl.BlockSpec((PK_HEADS, 2 * PK_ITILE, tb), lambda i, g: (0, g, i))
    full_j = pl.BlockSpec((PK_HEADS, PK_NKEYS, tb), lambda i, g: (0, 0, i))
    return pl.pallas_call(
        _peer_dense_kernel,
        grid=(t // tb, ntile // 2),
        in_specs=[
            pl.BlockSpec((tb, d), lambda i, g: (i, 0)),
            pl.BlockSpec((None, 1, d), lambda i, g: (i // per_b, 0, 0)),
            pl.BlockSpec((d, tb), lambda i, g: (0, i)),
            u_spec(lambda g: 0), u_spec(lambda g: 2 * g + 1),
            u_spec(lambda g: jnp.minimum(2 * g + 2, ntile - 1)),
            pl.BlockSpec((2, d, te), lambda i, g: (g, 0, 0)),
            pair_i, pair_i, full_j, full_j,
        ],
        out_specs=pl.BlockSpec((tb, d), lambda i, g: (i, 0)),
        out_shape=jax.ShapeDtypeStruct((t, d), F32),
        scratch_shapes=[pltpu.VMEM((d, tb), F32), pltpu.VMEM((2, te, tb), F32),
                        pltpu.VMEM((te, tb), BF16), pltpu.VMEM((te, tb), BF16)],
        compiler_params=_params("parallel", "arbitrary"),
        name="peer_dense",
    )(x2, gate2, ht, u_bf, u_bf, u_bf, vt_bf, cnt, e1, r2, e2)


def _final_norm_kernel(x_ref, g_ref, o_ref):
    x = x_ref[...]
    ms = jnp.mean(x * x, axis=-1, keepdims=True)
    o_ref[...] = x * lax.rsqrt(ms + NORM_EPS) * g_ref[...]


def _final_norm(x2, g):
    t, d = x2.shape
    tm = min(1024, t)
    return pl.pallas_call(
        _final_norm_kernel,
        grid=(t // tm,),
        in_specs=[pl.BlockSpec((tm, d), lambda i: (i, 0)), pl.BlockSpec((1, d), lambda i: (0, 0))],
        out_specs=pl.BlockSpec((tm, d), lambda i: (i, 0)),
        out_shape=jax.ShapeDtypeStruct((t, d), F32),
        compiler_params=_params("parallel"),
        name="final_norm",
    )(x2, g.reshape(1, d))


def _permute_w_in(w_in):
    main = jnp.concatenate([w_in[..., 5640:8712], w_in[..., 3584:4608], w_in[..., 0:3584],
                            w_in[..., 4608:5632]], axis=-1).astype(BF16)
    gates = jnp.pad(w_in[..., 5632:5640], ((0, 0), (0, 0), (0, 120))).astype(BF16)
    return main, gates


def kernel(x, c, mod_w, mod_b, norm_mix_g, norm_ffn_g, w_in, ml_conv_w, ml_conv_b, ml_gate_b,
           hg_lb_logits, hg_norm_g, ml_norm_g, w_branch, w_out, pk_wq, pk_keys, pk_u, pk_v, final_g):
    bsz, seq, d = x.shape
    depth = w_in.shape[0]
    t = bsz * seq
    x2 = x.reshape(t, d)

    mod = _modulation(c, mod_w, mod_b).reshape(depth, bsz, 6, 1, d)
    w_main, w_gates = _permute_w_in(w_in)
    w_branch_bf = w_branch.astype(BF16)
    w_out_bf = w_out.astype(BF16)
    u_bf = pk_u.astype(BF16)
    te = PK_ITILE * PK_NKEYS
    vt_bf = jnp.swapaxes(pk_v.astype(BF16).reshape(depth, pk_v.shape[1] // te, te, d), 2, 3)

    for l in range(depth):
        shift1, scale1, gate1, shift2, scale2, gate2 = (mod[l, :, r] for r in range(6))
        proj, gates = _norm_proj(x2, norm_mix_g[l], scale1, shift1, w_main[l], w_gates[l], seq)
        ya = _sb_attention(proj, bsz, seq)
        yb = _hgrn(proj, hg_lb_logits, hg_norm_g[l], l, bsz, seq)
        yc = _mlstm(proj, gates, gates[:, 0:8].T, ml_conv_w[l], ml_conv_b[l], ml_gate_b[l],
                    ml_norm_g[l], bsz, seq)
        x2 = _merge(x2, ya, yb, yc, proj, w_branch_bf[l], w_out_bf[l], gate1, seq)
        wk_t = _fold_keys(pk_keys[l], pk_wq[l]).astype(BF16)
        ht, st = _peer_scores(x2, norm_ffn_g[l], scale2, shift2, wk_t, seq)
        cnt, e1, r2, e2 = _peer_stats(st)
        x2 = _peer_dense(x2, gate2, ht, u_bf[l], vt_bf[l], cnt, e1, r2, e2, seq)
    return _final_norm(x2, final_g).reshape(bsz, seq, d)
```

```python
import functools

import numpy as np
import jax
import jax.numpy as jnp
from jax import lax
from jax.experimental import pallas as pl
from jax.experimental.pallas import tpu as pltpu

F32 = jnp.float32
BF16 = jnp.bfloat16

D_MODEL = 1024
DEPTH = 4
NORM_EPS = 1e-6
MASK_NEG = -1e30
LB_FLOOR = 1e-30
SB_HEAD_DIM = 64
HEADS4 = 4
HEAD_DIM = 128
BRANCH_WIDTH = 512
ML_CONV = 4
PK_HEADS = 8
PK_NKEYS = 128
PK_HALF = 64
PK_TOPK = 16

PROJ_WIDTH = 8704
COL_BRG = 0
COL_MLQ = 3072
COL_MLK = 3584
COL_SBQ = 4096
COL_SBK = 4608
COL_SBV = 5120
COL_HGQ = 5632
COL_HGF = 6144
COL_HGI = 6656
COL_HGG = 7168
COL_MLV = 7680
COL_MLO = 8192

CHUNK = 128
VMEM_LIMIT = 56 * 1024 * 1024


def _dot(a, b):
    return jnp.dot(a, b, preferred_element_type=F32)


def _dot_nt(a, b):
    return lax.dot_general(a, b, (((1,), (1,)), ((), ())), preferred_element_type=F32)


def _dot_tn(a, b):
    return lax.dot_general(a, b, (((0,), (0,)), ((), ())), preferred_element_type=F32)


def _split3(x):
    hi = x.astype(BF16)
    r = x - hi.astype(F32)
    mid = r.astype(BF16)
    lo = (r - mid.astype(F32)).astype(BF16)
    return hi, mid, lo


def _sel_dot(m01, x):
    hi, mid, lo = _split3(x)
    return _dot(m01, hi) + _dot(m01, mid) + _dot(m01, lo)


def _dot_sel(x, m01):
    hi, mid, lo = _split3(x)
    return _dot(hi, m01) + _dot(mid, m01) + _dot(lo, m01)


def _sel_dot2(m01, x):
    hi = x.astype(BF16)
    lo = (x - hi.astype(F32)).astype(BF16)
    return _dot(m01, hi) + _dot(m01, lo)


def _sigmoid(x):
    return 1.0 / (1.0 + jnp.exp(-x))


def _silu(x):
    return x * _sigmoid(x)


def _log_sigmoid(x):
    return jnp.minimum(x, 0.0) - jnp.log(1.0 + jnp.exp(-jnp.abs(x)))


def _params(*sem):
    return pltpu.CompilerParams(dimension_semantics=sem, vmem_limit_bytes=VMEM_LIMIT)


def _mod_kernel(c_ref, w_ref, b_ref, o_ref):
    cond = _silu(c_ref[...])
    o_ref[...] = jnp.dot(cond, w_ref[...], preferred_element_type=F32,
                         precision=lax.Precision.HIGHEST) + b_ref[...]


def _modulation(c, mod_w, mod_b):
    depth, d, n = mod_w.shape
    bsz = c.shape[0]
    tn = d
    return pl.pallas_call(
        _mod_kernel,
        grid=(depth, n // tn),
        in_specs=[
            pl.BlockSpec((bsz, d), lambda l, j: (0, 0)),
            pl.BlockSpec((None, d, tn), lambda l, j: (l, 0, j)),
            pl.BlockSpec((None, 1, tn), lambda l, j: (l, 0, j)),
        ],
        out_specs=pl.BlockSpec((None, bsz, tn), lambda l, j: (l, 0, j)),
        out_shape=jax.ShapeDtypeStruct((depth, bsz, n), F32),
        compiler_params=_params("parallel", "parallel"),
        name="modulation",
    )(c, mod_w, mod_b.reshape(depth, 1, n))


def _rms_mod(x, g, scale, shift):
    ms = jnp.mean(x * x, axis=-1, keepdims=True)
    y = x * lax.rsqrt(ms + NORM_EPS) * g
    return y * (1.0 + scale) + shift


def _norm_proj_kernel(x_ref, g_ref, sc_ref, sh_ref, w_ref, wg_ref, proj_ref, gates_ref, h_ref):
    @pl.when(pl.program_id(1) == 0)
    def _():
        hb = _rms_mod(x_ref[...], g_ref[...], sc_ref[...], sh_ref[...]).astype(BF16)
        h_ref[...] = hb
        gates_ref[...] = _dot(hb, wg_ref[...])

    proj_ref[...] = _dot(h_ref[...], w_ref[...]).astype(BF16)


def _norm_proj(x2, g, scale, shift, w_main, w_gates, seq):
    t, d = x2.shape
    tm = min(1024, seq)
    tn = PROJ_WIDTH // 4
    per_b = seq // tm
    n = w_main.shape[1]
    return pl.pallas_call(
        _norm_proj_kernel,
        grid=(t // tm, n // tn),
        in_specs=[
            pl.BlockSpec((tm, d), lambda i, j: (i, 0)),
            pl.BlockSpec((1, d), lambda i, j: (0, 0)),
            pl.BlockSpec((None, 1, d), lambda i, j: (i // per_b, 0, 0)),
            pl.BlockSpec((None, 1, d), lambda i, j: (i // per_b, 0, 0)),
            pl.BlockSpec((d, tn), lambda i, j: (0, j)),
            pl.BlockSpec((d, 128), lambda i, j: (0, 0)),
        ],
        out_specs=[
            pl.BlockSpec((tm, tn), lambda i, j: (i, j)),
            pl.BlockSpec((tm, 128), lambda i, j: (i, 0)),
        ],
        out_shape=[jax.ShapeDtypeStruct((t, n), BF16), jax.ShapeDtypeStruct((t, 128), F32)],
        scratch_shapes=[pltpu.VMEM((tm, d), BF16)],
        compiler_params=_params("parallel", "arbitrary"),
        name="norm_proj",
    )(x2, g.reshape(1, d), scale, shift, w_main, w_gates)


SB_TQ = 256
SB_TK = 256
SB_PAIRS = 4
SB_UNROLL = 1
assert SB_UNROLL * SB_TK == SB_TQ
LOG2_E = 1.4426950408889634


def _sb_kernel(q_ref, k_ref, v_ref, o_ref):
    qi = pl.program_id(2)
    lane = lax.broadcasted_iota(jnp.int32, (1, 128), 1)
    first = lane < SB_HEAD_DIM
    qh = []
    for p in range(SB_PAIRS):
        q2 = q_ref[:, p * 128:(p + 1) * 128] * (SB_HEAD_DIM ** -0.5 * LOG2_E)
        zero = jnp.zeros_like(q2)
        qh.append((jnp.where(first, q2, zero), jnp.where(first, zero, q2)))
    t_idx = qi * SB_TQ + lax.broadcasted_iota(jnp.int32, (SB_TQ, 1), 0)
    rs = lax.broadcasted_iota(jnp.int32, (SB_TK, SB_TK), 0)
    cs = lax.broadcasted_iota(jnp.int32, (SB_TK, SB_TK), 1)
    tri = jnp.where(rs > cs, 1.0, 0.0).astype(BF16)
    tri_ext = jnp.concatenate([tri, jnp.ones_like(tri)], axis=1)
    nkb = (qi + 1) * (SB_TQ // SB_TK)

    def body(it, carry, masked):
        cc = list(carry[0])
        acc = list(carry[1])
        for sub in range(SB_UNROLL):
            kb = nkb - 1 - (it * SB_UNROLL + sub)
            r0 = pl.multiple_of(kb * SB_TK, SB_TK)
            past = (kb * SB_TK + lax.broadcasted_iota(jnp.int32, (1, SB_TK), 1)) < t_idx
            for p in range(SB_PAIRS):
                ks = k_ref[pl.ds(r0, SB_TK), p * 128:(p + 1) * 128]
                vs = v_ref[pl.ds(r0, SB_TK), p * 128:(p + 1) * 128]
                vzero = jnp.zeros_like(vs)
                vh = (jnp.where(first, vs, vzero), jnp.where(first, vzero, vs))
                for h in range(2):
                    z = _dot_nt(qh[p][h], ks)
                    ls = jnp.minimum(z, 0.0) - jnp.log2(1.0 + jnp.exp2(-jnp.abs(z)))
                    lk = ls - z
                    if masked:
                        lk = jnp.where(past, lk, 0.0)
                    sums = _dot(lk.astype(BF16), tri_ext)
                    w = jnp.exp2(ls + sums[:, 0:SB_TK] + cc[2 * p + h])
                    if masked:
                        w = jnp.where(past, w, 0.0)
                    acc[p] = acc[p] + _dot(w.astype(BF16), vh[h])
                    cc[2 * p + h] = cc[2 * p + h] + sums[:, SB_TK:]
        return tuple(cc), tuple(acc)

    init = (tuple(jnp.zeros((SB_TQ, SB_TK), F32) for _ in range(2 * SB_PAIRS)),
            tuple(jnp.zeros((SB_TQ, 128), F32) for _ in range(SB_PAIRS)))
    carry = body(0, init, True)
    _, acc = lax.fori_loop(1, nkb // SB_UNROLL, functools.partial(body, masked=False), carry)
    for p in range(SB_PAIRS):
        o_ref[:, p * 128:(p + 1) * 128] = acc[p].astype(o_ref.dtype)


def _sb_attention(proj, bsz, seq):
    t = proj.shape[0]
    nq = seq // SB_TQ
    wd = 128 * SB_PAIRS
    qb, kb, vb = COL_SBQ // wd, COL_SBK // wd, COL_SBV // wd
    return pl.pallas_call(
        _sb_kernel,
        grid=(bsz, BRANCH_WIDTH // wd, nq),
        in_specs=[
            pl.BlockSpec((SB_TQ, wd), lambda b, p, i: (b * nq + i, qb + p)),
            pl.BlockSpec((seq, wd), lambda b, p, i: (b, kb + p)),
            pl.BlockSpec((seq, wd), lambda b, p, i: (b, vb + p)),
        ],
        out_specs=pl.BlockSpec((SB_TQ, wd), lambda b, p, i: (b * nq + i, p)),
        out_shape=jax.ShapeDtypeStruct((t, BRANCH_WIDTH), BF16),
        compiler_params=_params("parallel", "parallel", "arbitrary"),
        name="sb_attention",
    )(proj, proj, proj)


def _hgrn_consts():
    n_l = CHUNK
    t = np.arange(n_l)[:, None]
    u = np.arange(n_l)[None, :]
    dq, dk, masks = [], [], []
    n = n_l
    while n >= 2:
        half = n // 2
        a_t = (t // n) * n
        mid = a_t + half - 1
        up_t = (t - a_t) >= half
        up_u = (u - (u // n) * n) >= half
        dq.append(up_t & (u > mid) & (u <= t))
        dk.append((~up_t) & (u > t) & (u <= mid))
        masks.append(((t // n) == (u // n)) & up_t & (~up_u))
        n //= 2
    masks.append(t == u)
    cm = np.concatenate(dq + dk + [u <= t, u > t], axis=0).astype(np.float32)
    return cm, np.stack(masks).astype(np.float32)


HG_LEVELS = 7


def _hgrn_kernel(lbl_ref, q_ref, f_ref, i_ref, g_ref, cm_ref, mask_ref, ng_ref, o_ref, st_ref,
                 ee_ref, *, layer, nchunk):
    @pl.when(pl.program_id(1) == 0)
    def _():
        st_ref[...] = jnp.zeros_like(st_ref)

    lg = lbl_ref[...]
    e = jnp.exp(lg - jnp.max(lg, axis=0, keepdims=True))
    sm = e / jnp.sum(e, axis=0, keepdims=True)
    lb = jnp.sum(sm[0:layer + 1], axis=0, keepdims=True) - sm[0:1]
    log_lb = jnp.log(jnp.maximum(lb, LB_FLOOR))
    log_1m = jnp.log(1.0 - lb)
    cm = cm_ref[...]
    nl = HG_LEVELS * CHUNK

    def chunk_body(c, carry):
        r0 = pl.multiple_of(c * CHUNK, CHUNK)
        rows = pl.ds(r0, CHUNK)
        z = f_ref[rows, :].astype(F32)
        bb = log_1m + _log_sigmoid(z)
        lf = jnp.maximum(log_lb, bb) + jnp.log(1.0 + jnp.exp(-jnp.abs(log_lb - bb)))
        kin_all = (1.0 - lb) * _sigmoid(-z)
        ee_ref[...] = jnp.exp(_sel_dot2(cm, lf))
        for h in range(HEADS4):
            hs = slice(h * HEAD_DIM, (h + 1) * HEAD_DIM)
            q = q_ref[rows, hs].astype(F32)
            iv = i_ref[rows, hs].astype(F32)
            gv = g_ref[rows, hs].astype(F32)
            kin = kin_all[:, hs]
            i_act = _silu(iv).astype(BF16)
            scores = mask_ref[HG_LEVELS] * _dot_nt(q.astype(BF16), kin.astype(BF16))
            for l in range(HG_LEVELS):
                ql = (q * ee_ref[l * CHUNK:(l + 1) * CHUNK, hs]).astype(BF16)
                kl = (kin * ee_ref[nl + l * CHUNK:nl + (l + 1) * CHUNK, hs]).astype(BF16)
                scores = scores + mask_ref[l] * _dot_nt(ql, kl)
            eb = ee_ref[2 * nl:2 * nl + CHUNK, hs]
            esuf = ee_ref[2 * nl + CHUNK:2 * nl + 2 * CHUNK, hs]
            st = st_ref[h]
            o = _dot(scores.astype(BF16), i_act) + _dot_nt((q * eb).astype(BF16), st.astype(BF16))
            st_ref[h] = st * eb[CHUNK - 1:CHUNK, :] + _dot_tn(i_act, (kin * esuf).astype(BF16))
            ms = jnp.mean(o * o, axis=-1, keepdims=True)
            y = o * lax.rsqrt(ms + NORM_EPS) * ng_ref[:, hs] * _silu(gv)
            o_ref[rows, hs] = y.astype(o_ref.dtype)
        return carry

    lax.fori_loop(0, nchunk, chunk_body, 0, unroll=2)


def _hgrn(proj, lb_logits, norm_g, layer, bsz, seq):
    t = proj.shape[0]
    ts = min(512, seq)
    nt = seq // ts
    cm, masks = _hgrn_consts()
    w = BRANCH_WIDTH
    col = lambda c: (lambda b, i: (b * nt + i, c // w))
    return pl.pallas_call(
        functools.partial(_hgrn_kernel, layer=layer, nchunk=ts // CHUNK),
        grid=(bsz, nt),
        in_specs=[
            pl.BlockSpec(lb_logits.shape, lambda b, i: (0, 0)),
            pl.BlockSpec((ts, w), col(COL_HGQ)),
            pl.BlockSpec((ts, w), col(COL_HGF)),
            pl.BlockSpec((ts, w), col(COL_HGI)),
            pl.BlockSpec((ts, w), col(COL_HGG)),
            pl.BlockSpec(cm.shape, lambda b, i: (0, 0)),
            pl.BlockSpec(masks.shape, lambda b, i: (0, 0, 0)),
            pl.BlockSpec((1, w), lambda b, i: (0, 0)),
        ],
        out_specs=pl.BlockSpec((ts, w), lambda b, i: (b * nt + i, 0)),
        out_shape=jax.ShapeDtypeStruct((t, w), BF16),
        scratch_shapes=[pltpu.VMEM((HEADS4, HEAD_DIM, HEAD_DIM), F32),
                        pltpu.VMEM(((2 * HG_LEVELS + 2) * CHUNK, w), F32)],
        compiler_params=_params("parallel", "arbitrary"),
        name="hgrn2",
    )(lb_logits, proj, proj, proj, proj, jnp.asarray(cm, BF16), jnp.asarray(masks, F32),
      norm_g.reshape(1, w))


def _mlstm_kernel(q_ref, k_ref, v_ref, og_ref, gc_ref, gr_ref, cw_ref, cb_ref, brow_ref, bcol_ref,
                  ng_ref, o_ref, qbuf, kbuf, qact, kact, ct_ref, m_ref, *, ts):
    nchunk = ts // CHUNK
    w = BRANCH_WIDTH

    @pl.when(pl.program_id(1) == 0)
    def _():
        qbuf[0:8, :] = jnp.zeros((8, w), F32)
        kbuf[0:8, :] = jnp.zeros((8, w), F32)
        ct_ref[...] = jnp.zeros_like(ct_ref)
        m_ref[...] = jnp.zeros_like(m_ref)

    qbuf[8:8 + ts, :] = q_ref[...].astype(F32)
    kbuf[8:8 + ts, :] = k_ref[...].astype(F32)

    rs = lax.broadcasted_iota(jnp.int32, (CHUNK, CHUNK), 0)
    cs = lax.broadcasted_iota(jnp.int32, (CHUNK, CHUNK), 1)
    causal = cs <= rs
    tri_c = jnp.where(causal, 1.0, 0.0).astype(BF16)
    tri_r = jnp.where(rs <= cs, 1.0, 0.0).astype(BF16)
    ones_v = jnp.ones((CHUNK, HEAD_DIM), F32)
    kscale = HEAD_DIM ** -0.5

    for cc in range(nchunk):
        yq = jnp.zeros((CHUNK, w), F32) + cb_ref[:, 0:w]
        yk = jnp.zeros((CHUNK, w), F32) + cb_ref[:, w:2 * w]
        for j in range(ML_CONV):
            lo = cc * CHUNK + 8 - (ML_CONV - 1 - j)
            yq = yq + cw_ref[j:j + 1, 0:w] * qbuf[lo:lo + CHUNK, :]
            yk = yk + cw_ref[j:j + 1, w:2 * w] * kbuf[lo:lo + CHUNK, :]
        qact[cc * CHUNK:(cc + 1) * CHUNK, :] = _silu(yq).astype(BF16)
        kact[cc * CHUNK:(cc + 1) * CHUNK, :] = (_silu(yk) * kscale).astype(BF16)

    def chunk_body(c, carry):
        r0 = pl.multiple_of(c * CHUNK, CHUNK)
        rows = pl.ds(r0, CHUNK)
        pre_c = gc_ref[rows, :] + brow_ref[...]
        pre_r = gr_ref[:, rows] + bcol_ref[...]
        b_c = _sel_dot(tri_c, _log_sigmoid(pre_c))
        b_r = _dot_sel(_log_sigmoid(pre_r), tri_r)
        for h in range(HEADS4):
            hs = slice(h * HEAD_DIM, (h + 1) * HEAD_DIM)
            qh = qact[rows, hs]
            kh = kact[rows, hs]
            vx = jnp.concatenate([v_ref[rows, hs].astype(F32), ones_v], axis=1)
            bcol = b_c[:, HEADS4 + h:HEADS4 + h + 1]
            icol = pre_c[:, h:h + 1]
            brow = b_r[HEADS4 + h:HEADS4 + h + 1, :]
            irow = pre_r[h:h + 1, :]
            m = m_ref[h:h + 1, 0:1]
            log_d = jnp.where(causal, bcol - brow + irow, MASK_NEG)
            log_inter = bcol + m
            m_t = jnp.maximum(log_inter, jnp.max(log_d, axis=-1, keepdims=True))
            dmat = jnp.where(causal, jnp.exp(log_d - m_t), 0.0)
            inter = jnp.exp(log_inter - m_t)
            qk = _dot_nt(qh, kh) * dmat
            ct = ct_ref[h]
            hx = _dot(qk.astype(BF16), vx.astype(BF16)) + inter * _dot(qh, ct.astype(BF16))
            hout = hx[:, 0:HEAD_DIM] / jnp.maximum(jnp.abs(hx[:, HEAD_DIM:]), jnp.exp(-m_t))
            b_last = bcol[CHUNK - 1:CHUNK, :]
            m_new = jnp.maximum(b_last + m, jnp.max(b_last - brow + irow, axis=-1, keepdims=True))
            w_col = jnp.exp(b_last - bcol + icol - m_new)
            dec = jnp.exp(b_last + m - m_new)
            ct_ref[h] = dec * ct + _dot_tn(kh, (vx * w_col).astype(BF16))
            m_ref[h:h + 1, :] = jnp.broadcast_to(m_new, (1, 128))
            ms = jnp.mean(hout * hout, axis=-1, keepdims=True)
            y = hout * lax.rsqrt(ms + NORM_EPS) * ng_ref[:, hs] * _sigmoid(og_ref[rows, hs].astype(F32))
            o_ref[rows, hs] = y.astype(o_ref.dtype)
        return carry

    lax.fori_loop(0, nchunk, chunk_body, 0, unroll=2)
    qbuf[0:8, :] = qbuf[ts:ts + 8, :]
    kbuf[0:8, :] = kbuf[ts:ts + 8, :]


def _mlstm(proj, gates, gates_t, conv_w, conv_b, gate_b, norm_g, bsz, seq):
    t = proj.shape[0]
    ts = min(512, seq)
    nt = seq // ts
    w = BRANCH_WIDTH
    col = lambda c: (lambda b, i: (b * nt + i, c // w))
    brow = jnp.zeros((1, 128), F32).at[0, 0:8].set(gate_b)
    bcol = gate_b.reshape(8, 1)
    return pl.pallas_call(
        functools.partial(_mlstm_kernel, ts=ts),
        grid=(bsz, nt),
        in_specs=[
            pl.BlockSpec((ts, w), col(COL_MLQ)),
            pl.BlockSpec((ts, w), col(COL_MLK)),
            pl.BlockSpec((ts, w), col(COL_MLV)),
            pl.BlockSpec((ts, w), col(COL_MLO)),
            pl.BlockSpec((ts, 128), lambda b, i: (b * nt + i, 0)),
            pl.BlockSpec((8, ts), lambda b, i: (0, b * nt + i)),
            pl.BlockSpec((ML_CONV, 2 * w), lambda b, i: (0, 0)),
            pl.BlockSpec((1, 2 * w), lambda b, i: (0, 0)),
            pl.BlockSpec((1, 128), lambda b, i: (0, 0)),
            pl.BlockSpec((8, 1), lambda b, i: (0, 0)),
            pl.BlockSpec((1, w), lambda b, i: (0, 0)),
        ],
        out_specs=pl.BlockSpec((ts, w), lambda b, i: (b * nt + i, 0)),
        out_shape=jax.ShapeDtypeStruct((t, w), BF16),
        scratch_shapes=[
            pltpu.VMEM((ts + 8, w), F32),
            pltpu.VMEM((ts + 8, w), F32),
            pltpu.VMEM((ts, w), BF16),
            pltpu.VMEM((ts, w), BF16),
            pltpu.VMEM((HEADS4, HEAD_DIM, 2 * HEAD_DIM), F32),
            pltpu.VMEM((8, 128), F32),
        ],
        compiler_params=_params("parallel", "arbitrary"),
        name="mlstm",
    )(proj, proj, proj, proj, gates, gates_t, conv_w, conv_b.reshape(1, 2 * w), brow, bcol,
      norm_g.reshape(1, w))


def _merge_kernel(x_ref, ya_ref, yb_ref, yc_ref, bg_ref, wb_ref, wo_ref, g1_ref, o_ref):
    d = D_MODEL
    merged = jnp.zeros(x_ref.shape, F32)
    for g, y_ref in enumerate((ya_ref, yb_ref, yc_ref)):
        up = _dot(y_ref[...], wb_ref[g])
        merged = merged + _sigmoid(bg_ref[:, g * d:(g + 1) * d].astype(F32)) * up
    y = _dot(merged.astype(BF16), wo_ref[...])
    o_ref[...] = x_ref[...] + g1_ref[...] * y


def _merge(x2, ya, yb, yc, proj, w_branch, w_out, gate1, seq):
    t, d = x2.shape
    tm = min(512, seq)
    per_b = seq // tm
    w = BRANCH_WIDTH
    row = lambda i: (i, 0)
    return pl.pallas_call(
        _merge_kernel,
        grid=(t // tm,),
        in_specs=[
            pl.BlockSpec((tm, d), row),
            pl.BlockSpec((tm, w), row),
            pl.BlockSpec((tm, w), row),
            pl.BlockSpec((tm, w), row),
            pl.BlockSpec((tm, 3 * d), lambda i: (i, COL_BRG // (3 * d))),
            pl.BlockSpec((3, w, d), lambda i: (0, 0, 0)),
            pl.BlockSpec((d, d), lambda i: (0, 0)),
            pl.BlockSpec((None, 1, d), lambda i: (i // per_b, 0, 0)),
        ],
        out_specs=pl.BlockSpec((tm, d), row),
        out_shape=jax.ShapeDtypeStruct((t, d), F32),
        compiler_params=_params("parallel"),
        name="merge",
    )(x2, ya, yb, yc, proj, w_branch, w_out, gate1)


def _fold_keys_kernel(keys_ref, wq_ref, o_ref):
    o_ref[...] = lax.dot_general(keys_ref[...], wq_ref[...], (((1,), (1,)), ((), ())),
                                 preferred_element_type=F32, precision=lax.Precision.HIGHEST)


def _fold_keys(pk_keys, pk_wq):
    d = pk_wq.shape[0]
    sets = 2 * PK_HEADS
    keys = pk_keys.reshape(sets, PK_NKEYS, PK_HALF)
    wq = pk_wq.reshape(d, sets, PK_HALF).transpose(1, 0, 2)
    out = pl.pallas_call(
        _fold_keys_kernel,
        grid=(sets,),
        in_specs=[
            pl.BlockSpec((None, PK_NKEYS, PK_HALF), lambda s: (s, 0, 0)),
            pl.BlockSpec((None, d, PK_HALF), lambda s: (s, 0, 0)),
        ],
        out_specs=pl.BlockSpec((None, PK_NKEYS, d), lambda s: (s, 0, 0)),
        out_shape=jax.ShapeDtypeStruct((sets, PK_NKEYS, d), F32),
        compiler_params=_params("parallel"),
        name="fold_keys",
    )(keys, wq)
    return out.reshape(sets * PK_NKEYS, d)


def _peer_scores_kernel(x_ref, g_ref, sc_ref, sh_ref, wk_ref, ht_ref, st_ref):
    h = _rms_mod(x_ref[...], g_ref[...], sc_ref[...], sh_ref[...])
    ht = h.T.astype(BF16)
    ht_ref[...] = ht
    st_ref[...] = _dot(wk_ref[...], ht)


def _peer_scores(x2, g, scale, shift, wk_t, seq):
    t, d = x2.shape
    tb = min(512, seq)
    per_b = seq // tb
    nk = wk_t.shape[0]
    return pl.pallas_call(
        _peer_scores_kernel,
        grid=(t // tb,),
        in_specs=[
            pl.BlockSpec((tb, d), lambda i: (i, 0)),
            pl.BlockSpec((1, d), lambda i: (0, 0)),
            pl.BlockSpec((None, 1, d), lambda i: (i // per_b, 0, 0)),
            pl.BlockSpec((None, 1, d), lambda i: (i // per_b, 0, 0)),
            pl.BlockSpec((nk, d), lambda i: (0, 0)),
        ],
        out_specs=[
            pl.BlockSpec((d, tb), lambda i: (0, i)),
            pl.BlockSpec((nk, tb), lambda i: (0, i)),
        ],
        out_shape=[jax.ShapeDtypeStruct((d, t), BF16), jax.ShapeDtypeStruct((nk, t), F32)],
        compiler_params=_params("parallel"),
        name="peer_scores",
    )(x2, g.reshape(1, d), scale, shift, wk_t)


NEG_INF = float("-inf")
PK_ROUNDS = PK_TOPK + 1
PK_RANK_NONE = 32.0


def _top_rows(x, rounds, want_rank=False):
    vals = []
    rank = jnp.full(x.shape, PK_RANK_NONE, F32)
    for r in range(rounds):
        m = jnp.max(x, axis=0, keepdims=True)
        vals.append(m)
        eq = x == m
        if want_rank:
            rank = jnp.where(eq, float(r), rank)
        if r + 1 < rounds:
            x = jnp.where(eq, NEG_INF, x)
    return (vals, rank) if want_rank else vals


def _count_ge(v, need):
    assert PK_TOPK == 16
    w = jnp.where
    c1 = v[7] >= need
    c2 = w(c1, v[11], v[3]) >= need
    c3 = w(c1, w(c2, v[13], v[9]), w(c2, v[5], v[1])) >= need
    c4 = w(c1, w(c2, w(c3, v[14], v[12]), w(c3, v[10], v[8])),
           w(c2, w(c3, v[6], v[4]), w(c3, v[2], v[0]))) >= need
    c5 = v[15] >= need
    return (w(c1, 8.0, 0.0) + w(c2, 4.0, 0.0) + w(c3, 2.0, 0.0) + w(c4, 1.0, 0.0) + w(c5, 1.0, 0.0))


def _peer_stats_kernel(st_ref, cnt_ref, e1_ref, r2_ref, e2_ref):
    tb = st_ref.shape[-1]
    for h in range(PK_HEADS):
        s1 = st_ref[h, 0]
        s2 = st_ref[h, 1]
        v1 = _top_rows(s1, PK_ROUNDS)
        v2, rank2 = _top_rows(s2, PK_ROUNDS, want_rank=True)
        v2_16 = jnp.concatenate(v2[0:16], axis=0)
        blocks = [v1[0] + v2_16]
        for a in range(1, 8):
            blocks.append(v1[a] + v2_16[0:8])
        blocks.append(jnp.concatenate(v1[8:16], axis=0) + v2[0])
        blocks.append(jnp.concatenate(
            [v1[0] + v2[16], v1[16] + v2[0], jnp.full((6, tb), NEG_INF, F32)], axis=0))
        cand = jnp.concatenate(blocks, axis=0)
        top = _top_rows(cand, PK_ROUNDS)
        thr = 0.5 * (top[PK_TOPK - 1] + top[PK_TOPK])
        zsum = jnp.zeros((1, tb), F32)
        for r in range(PK_TOPK):
            zsum = zsum + jnp.exp(top[r] - top[0])
        need = thr - s1
        cnt_ref[h] = _count_ge(v2, need)
        e1_ref[h] = jnp.exp(s1 - v1[0]) / zsum
        r2_ref[h] = rank2.astype(BF16)
        e2_ref[h] = jnp.exp(s2 - v2[0]).astype(BF16)


def _peer_stats(st):
    nk, t = st.shape
    tb = min(256, t)
    st4 = st.reshape(PK_HEADS, 2, PK_NKEYS, t)
    shp = lambda dt: jax.ShapeDtypeStruct((PK_HEADS, PK_NKEYS, t), dt)
    spec = pl.BlockSpec((PK_HEADS, PK_NKEYS, tb), lambda i: (0, 0, i))
    return pl.pallas_call(
        _peer_stats_kernel,
        grid=(t // tb,),
        in_specs=[pl.BlockSpec((PK_HEADS, 2, PK_NKEYS, tb), lambda i: (0, 0, 0, i))],
        out_specs=[spec, spec, spec, spec],
        out_shape=[shp(F32), shp(F32), shp(BF16), shp(BF16)],
        compiler_params=_params("parallel"),
        name="peer_stats",
    )(st4)


PK_ITILE = 8
PK_DENSE_TB = 512
GELU_C0 = 0.7978845608028654
GELU_C1 = 0.044715


def _gelu_tanh(a):
    return 0.5 * a * (1.0 + jnp.tanh(GELU_C0 * (a + GELU_C1 * (a * a * a))))


def _peer_tile_weights(a, row0, cnt_ref, e1_ref, r2_ref, e2_ref, p_ref):
    tb = a.shape[1]
    shape = (PK_NKEYS, tb)
    zero = jnp.zeros(shape, BF16)
    for il in range(PK_ITILE):
        g = _gelu_tanh(a[il * PK_NKEYS:(il + 1) * PK_NKEYS, :].astype(BF16))
        wsum = zero
        for h in range(PK_HEADS):
            cb = jnp.broadcast_to(cnt_ref[h, row0 + il:row0 + il + 1, :], shape).astype(BF16)
            eb = jnp.broadcast_to(e1_ref[h, row0 + il:row0 + il + 1, :], shape).astype(BF16)
            wsum = wsum + eb * jnp.where(cb > r2_ref[h], e2_ref[h], zero)
        p_ref[il * PK_NKEYS:(il + 1) * PK_NKEYS, :] = g * wsum


def _peer_dense_kernel(x_ref, g2_ref, ht_ref, u0_ref, ua_ref, ub_ref, vt_ref, cnt_ref, e1_ref,
                       r2_ref, e2_ref, o_ref, acc_ref, a_ref, pa_ref, pb_ref):
    g = pl.program_id(1)
    te = pa_ref.shape[0]
    ht = ht_ref[...]

    @pl.when(g == 0)
    def _():
        acc_ref[...] = jnp.zeros_like(acc_ref)
        a_ref[0] = _dot(u0_ref[...], ht)

    _peer_tile_weights(a_ref.at[g % 2], 0, cnt_ref, e1_ref, r2_ref, e2_ref, pa_ref)
    a_odd = _dot(ua_ref[...], ht)
    acc_ref[...] += _dot(vt_ref[0], pa_ref[...])
    _peer_tile_weights(a_odd, PK_ITILE, cnt_ref, e1_ref, r2_ref, e2_ref, pb_ref)
    acc_ref[...] += _dot(vt_ref[1], pb_ref[...])
    a_ref[(g + 1) % 2] = _dot(ub_ref[...], ht)

    @pl.when(g == pl.num_programs(1) - 1)
    def _():
        o_ref[...] = x_ref[...] + g2_ref[...] * acc_ref[...].T


def _peer_dense(x2, gate2, ht, u_bf, vt_bf, cnt, e1, r2, e2, seq):
    t, d = x2.shape
    tb = min(PK_DENSE_TB, seq)
    per_b = seq // tb
    te = PK_ITILE * PK_NKEYS
    ntile = u_bf.shape[0] // te
    u_spec = lambda f: pl.BlockSpec((te, d), lambda i, g: (f(g), 0))
    pair_i = pl.BlockSpec((PK_HEADS, 2 * PK_ITILE, tb), lambda i, g: (0, g, i))
    full_j = pl.BlockSpec((PK_HEADS, PK_NKEYS, tb), lambda i, g: (0, 0, i))
    return pl.pallas_call(
        _peer_dense_kernel,
        grid=(t // tb, ntile // 2),
        in_specs=[
            pl.BlockSpec((tb, d), lambda i, g: (i, 0)),
            pl.BlockSpec((None, 1, d), lambda i, g: (i // per_b, 0, 0)),
            pl.BlockSpec((d, tb), lambda i, g: (0, i)),
            u_spec(lambda g: 0), u_spec(lambda g: 2 * g + 1),
            u_spec(lambda g: jnp.minimum(2 * g + 2, ntile - 1)),
            pl.BlockSpec((2, d, te), lambda i, g: (g, 0, 0)),
            pair_i, pair_i, full_j, full_j,
        ],
        out_specs=pl.BlockSpec((tb, d), lambda i, g: (i, 0)),
        out_shape=jax.ShapeDtypeStruct((t, d), F32),
        scratch_shapes=[pltpu.VMEM((d, tb), F32), pltpu.VMEM((2, te, tb), F32),
                        pltpu.VMEM((te, tb), BF16), pltpu.VMEM((te, tb), BF16)],
        compiler_params=_params("parallel", "arbitrary"),
        name="peer_dense",
    )(x2, gate2, ht, u_bf, u_bf, u_bf, vt_bf, cnt, e1, r2, e2)


def _final_norm_kernel(x_ref, g_ref, o_ref):
    x = x_ref[...]
    ms = jnp.mean(x * x, axis=-1, keepdims=True)
    o_ref[...] = x * lax.rsqrt(ms + NORM_EPS) * g_ref[...]


def _final_norm(x2, g):
    t, d = x2.shape
    tm = min(1024, t)
    return pl.pallas_call(
        _final_norm_kernel,
        grid=(t // tm,),
        in_specs=[pl.BlockSpec((tm, d), lambda i: (i, 0)), pl.BlockSpec((1, d), lambda i: (0, 0))],
        out_specs=pl.BlockSpec((tm, d), lambda i: (i, 0)),
        out_shape=jax.ShapeDtypeStruct((t, d), F32),
        compiler_params=_params("parallel"),
        name="final_norm",
    )(x2, g.reshape(1, d))


def _permute_w_in(w_in):
    main = jnp.concatenate([w_in[..., 5640:8712], w_in[..., 3584:4608], w_in[..., 0:3584],
                            w_in[..., 4608:5632]], axis=-1).astype(BF16)
    gates = jnp.pad(w_in[..., 5632:5640], ((0, 0), (0, 0), (0, 120))).astype(BF16)
    return main, gates


def kernel(x, c, mod_w, mod_b, norm_mix_g, norm_ffn_g, w_in, ml_conv_w, ml_conv_b, ml_gate_b,
           hg_lb_logits, hg_norm_g, ml_norm_g, w_branch, w_out, pk_wq, pk_keys, pk_u, pk_v, final_g):
    bsz, seq, d = x.shape
    depth = w_in.shape[0]
    t = bsz * seq
    x2 = x.reshape(t, d)

    mod = _modulation(c, mod_w, mod_b).reshape(depth, bsz, 6, 1, d)
    w_main, w_gates = _permute_w_in(w_in)
    w_branch_bf = w_branch.astype(BF16)
    w_out_bf = w_out.astype(BF16)
    u_bf = pk_u.astype(BF16)
    te = PK_ITILE * PK_NKEYS
    vt_bf = jnp.swapaxes(pk_v.astype(BF16).reshape(depth, pk_v.shape[1] // te, te, d), 2, 3)

    for l in range(depth):
        shift1, scale1, gate1, shift2, scale2, gate2 = (mod[l, :, r] for r in range(6))
        proj, gates = _norm_proj(x2, norm_mix_g[l], scale1, shift1, w_main[l], w_gates[l], seq)
        ya = _sb_attention(proj, bsz, seq)
        yb = _hgrn(proj, hg_lb_logits, hg_norm_g[l], l, bsz, seq)
        yc = _mlstm(proj, gates, gates[:, 0:8].T, ml_conv_w[l], ml_conv_b[l], ml_gate_b[l],
                    ml_norm_g[l], bsz, seq)
        x2 = _merge(x2, ya, yb, yc, proj, w_branch_bf[l], w_out_bf[l], gate1, seq)
        wk_t = _fold_keys(pk_keys[l], pk_wq[l]).astype(BF16)
        ht, st = _peer_scores(x2, norm_ffn_g[l], scale2, shift2, wk_t, seq)
        cnt, e1, r2, e2 = _peer_stats(st)
        x2 = _peer_dense(x2, gate2, ht, u_bf[l], vt_bf[l], cnt, e1, r2, e2, seq)
    return _final_norm(x2, final_g).reshape(bsz, seq, d)
```

```python
import functools

import numpy as np
import jax
import jax.numpy as jnp
from jax import lax
from jax.experimental import pallas as pl
from jax.experimental.pallas import tpu as pltpu

F32 = jnp.float32
BF16 = jnp.bfloat16

D_MODEL = 1024
DEPTH = 4
NORM_EPS = 1e-6
MASK_NEG = -1e30
LB_FLOOR = 1e-30
SB_HEAD_DIM = 64
HEADS4 = 4
HEAD_DIM = 128
BRANCH_WIDTH = 512
ML_CONV = 4
PK_HEADS = 8
PK_NKEYS = 128
PK_HALF = 64
PK_TOPK = 16

PROJ_WIDTH = 8704
COL_BRG = 0
COL_MLQ = 3072
COL_MLK = 3584
COL_SBQ = 4096
COL_SBK = 4608
COL_SBV = 5120
COL_HGQ = 5632
COL_HGF = 6144
COL_HGI = 6656
COL_HGG = 7168
COL_MLV = 7680
COL_MLO = 8192

CHUNK = 128
VMEM_LIMIT = 56 * 1024 * 1024


def _dot(a, b):
    return jnp.dot(a, b, preferred_element_type=F32)


def _dot_nt(a, b):
    return lax.dot_general(a, b, (((1,), (1,)), ((), ())), preferred_element_type=F32)


def _dot_tn(a, b):
    return lax.dot_general(a, b, (((0,), (0,)), ((), ())), preferred_element_type=F32)


def _split3(x):
    hi = x.astype(BF16)
    r = x - hi.astype(F32)
    mid = r.astype(BF16)
    lo = (r - mid.astype(F32)).astype(BF16)
    return hi, mid, lo


def _sel_dot(m01, x):
    hi, mid, lo = _split3(x)
    return _dot(m01, hi) + _dot(m01, mid) + _dot(m01, lo)


def _dot_sel(x, m01):
    hi, mid, lo = _split3(x)
    return _dot(hi, m01) + _dot(mid, m01) + _dot(lo, m01)


def _sel_dot2(m01, x):
    hi = x.astype(BF16)
    lo = (x - hi.astype(F32)).astype(BF16)
    return _dot(m01, hi) + _dot(m01, lo)


def _sigmoid(x):
    return 1.0 / (1.0 + jnp.exp(-x))


def _silu(x):
    return x * _sigmoid(x)


def _log_sigmoid(x):
    return jnp.minimum(x, 0.0) - jnp.log(1.0 + jnp.exp(-jnp.abs(x)))


def _params(*sem):
    return pltpu.CompilerParams(dimension_semantics=sem, vmem_limit_bytes=VMEM_LIMIT)


def _mod_kernel(c_ref, w_ref, b_ref, o_ref):
    cond = _silu(c_ref[...])
    o_ref[...] = jnp.dot(cond, w_ref[...], preferred_element_type=F32,
                         precision=lax.Precision.HIGHEST) + b_ref[...]


def _modulation(c, mod_w, mod_b):
    depth, d, n = mod_w.shape
    bsz = c.shape[0]
    tn = d
    return pl.pallas_call(
        _mod_kernel,
        grid=(depth, n // tn),
        in_specs=[
            pl.BlockSpec((bsz, d), lambda l, j: (0, 0)),
            pl.BlockSpec((None, d, tn), lambda l, j: (l, 0, j)),
            pl.BlockSpec((None, 1, tn), lambda l, j: (l, 0, j)),
        ],
        out_specs=pl.BlockSpec((None, bsz, tn), lambda l, j: (l, 0, j)),
        out_shape=jax.ShapeDtypeStruct((depth, bsz, n), F32),
        compiler_params=_params("parallel", "parallel"),
        name="modulation",
    )(c, mod_w, mod_b.reshape(depth, 1, n))


def _rms_mod(x, g, scale, shift):
    ms = jnp.mean(x * x, axis=-1, keepdims=True)
    y = x * lax.rsqrt(ms + NORM_EPS) * g
    return y * (1.0 + scale) + shift


def _norm_proj_kernel(x_ref, g_ref, sc_ref, sh_ref, w_ref, wg_ref, proj_ref, gates_ref, h_ref):
    @pl.when(pl.program_id(1) == 0)
    def _():
        hb = _rms_mod(x_ref[...], g_ref[...], sc_ref[...], sh_ref[...]).astype(BF16)
        h_ref[...] = hb
        gates_ref[...] = _dot(hb, wg_ref[...])

    proj_ref[...] = _dot(h_ref[...], w_ref[...]).astype(BF16)


def _norm_proj(x2, g, scale, shift, w_main, w_gates, seq):
    t, d = x2.shape
    tm = min(1024, seq)
    tn = PROJ_WIDTH // 4
    per_b = seq // tm
    n = w_main.shape[1]
    return pl.pallas_call(
        _norm_proj_kernel,
        grid=(t // tm, n // tn),
        in_specs=[
            pl.BlockSpec((tm, d), lambda i, j: (i, 0)),
            pl.BlockSpec((1, d), lambda i, j: (0, 0)),
            pl.BlockSpec((None, 1, d), lambda i, j: (i // per_b, 0, 0)),
            pl.BlockSpec((None, 1, d), lambda i, j: (i // per_b, 0, 0)),
            pl.BlockSpec((d, tn), lambda i, j: (0, j)),
            pl.BlockSpec((d, 128), lambda i, j: (0, 0)),
        ],
        out_specs=[
            pl.BlockSpec((tm, tn), lambda i, j: (i, j)),
            pl.BlockSpec((tm, 128), lambda i, j: (i, 0)),
        ],
        out_shape=[jax.ShapeDtypeStruct((t, n), BF16), jax.ShapeDtypeStruct((t, 128), F32)],
        scratch_shapes=[pltpu.VMEM((tm, d), BF16)],
        compiler_params=_params("parallel", "arbitrary"),
        name="norm_proj",
    )(x2, g.reshape(1, d), scale, shift, w_main, w_gates)


SB_TQ = 512
SB_TK = 256
SB_PAIRS = 2
SB_UNROLL = 2
assert SB_UNROLL * SB_TK == SB_TQ
LOG2_E = 1.4426950408889634


def _sb_kernel(q_ref, k_ref, v_ref, o_ref):
    qi = pl.program_id(2)
    lane = lax.broadcasted_iota(jnp.int32, (1, 128), 1)
    first = lane < SB_HEAD_DIM
    qh = []
    for p in range(SB_PAIRS):
        q2 = q_ref[:, p * 128:(p + 1) * 128] * (SB_HEAD_DIM ** -0.5 * LOG2_E)
        zero = jnp.zeros_like(q2)
        qh.append((jnp.where(first, q2, zero), jnp.where(first, zero, q2)))
    t_idx = qi * SB_TQ + lax.broadcasted_iota(jnp.int32, (SB_TQ, 1), 0)
    rs = lax.broadcasted_iota(jnp.int32, (SB_TK, SB_TK), 0)
    cs = lax.broadcasted_iota(jnp.int32, (SB_TK, SB_TK), 1)
    tri = jnp.where(rs > cs, 1.0, 0.0).astype(BF16)
    tri_ext = jnp.concatenate([tri, jnp.ones_like(tri)], axis=1)
    nkb = (qi + 1) * (SB_TQ // SB_TK)

    def body(it, carry, masked):
        cc = list(carry[0])
        acc = list(carry[1])
        for sub in range(SB_UNROLL):
            kb = nkb - 1 - (it * SB_UNROLL + sub)
            r0 = pl.multiple_of(kb * SB_TK, SB_TK)
            past = (kb * SB_TK + lax.broadcasted_iota(jnp.int32, (1, SB_TK), 1)) < t_idx
            for p in range(SB_PAIRS):
                ks = k_ref[pl.ds(r0, SB_TK), p * 128:(p + 1) * 128]
                vs = v_ref[pl.ds(r0, SB_TK), p * 128:(p + 1) * 128]
                vzero = jnp.zeros_like(vs)
                vh = (jnp.where(first, vs, vzero), jnp.where(first, vzero, vs))
                for h in range(2):
                    z = _dot_nt(qh[p][h], ks)
                    ls = jnp.minimum(z, 0.0) - jnp.log2(1.0 + jnp.exp2(-jnp.abs(z)))
                    lk = ls - z
                    if masked:
                        lk = jnp.where(past, lk, 0.0)
                    sums = _dot(lk.astype(BF16), tri_ext)
                    w = jnp.exp2(ls + sums[:, 0:SB_TK] + cc[2 * p + h])
                    if masked:
                        w = jnp.where(past, w, 0.0)
                    acc[p] = acc[p] + _dot(w.astype(BF16), vh[h])
                    cc[2 * p + h] = cc[2 * p + h] + sums[:, SB_TK:]
        return tuple(cc), tuple(acc)

    init = (tuple(jnp.zeros((SB_TQ, SB_TK), F32) for _ in range(2 * SB_PAIRS)),
            tuple(jnp.zeros((SB_TQ, 128), F32) for _ in range(SB_PAIRS)))
    carry = body(0, init, True)
    _, acc = lax.fori_loop(1, nkb // SB_UNROLL, functools.partial(body, masked=False), carry)
    for p in range(SB_PAIRS):
        o_ref[:, p * 128:(p + 1) * 128] = acc[p].astype(o_ref.dtype)


def _sb_attention(proj, bsz, seq):
    t = proj.shape[0]
    nq = seq // SB_TQ
    wd = 128 * SB_PAIRS
    qb, kb, vb = COL_SBQ // wd, COL_SBK // wd, COL_SBV // wd
    return pl.pallas_call(
        _sb_kernel,
        grid=(bsz, BRANCH_WIDTH // wd, nq),
        in_specs=[
            pl.BlockSpec((SB_TQ, wd), lambda b, p, i: (b * nq + i, qb + p)),
            pl.BlockSpec((seq, wd), lambda b, p, i: (b, kb + p)),
            pl.BlockSpec((seq, wd), lambda b, p, i: (b, vb + p)),
        ],
        out_specs=pl.BlockSpec((SB_TQ, wd), lambda b, p, i: (b * nq + i, p)),
        out_shape=jax.ShapeDtypeStruct((t, BRANCH_WIDTH), BF16),
        compiler_params=_params("parallel", "parallel", "arbitrary"),
        name="sb_attention",
    )(proj, proj, proj)


def _hgrn_consts():
    n_l = CHUNK
    t = np.arange(n_l)[:, None]
    u = np.arange(n_l)[None, :]
    dq, dk, masks = [], [], []
    n = n_l
    while n >= 2:
        half = n // 2
        a_t = (t // n) * n
        mid = a_t + half - 1
        up_t = (t - a_t) >= half
        up_u = (u - (u // n) * n) >= half
        dq.append(up_t & (u > mid) & (u <= t))
        dk.append((~up_t) & (u > t) & (u <= mid))
        masks.append(((t // n) == (u // n)) & up_t & (~up_u))
        n //= 2
    masks.append(t == u)
    cm = np.concatenate(dq + dk + [u <= t, u > t], axis=0).astype(np.float32)
    return cm, np.stack(masks).astype(np.float32)


HG_LEVELS = 7


def _hgrn_kernel(lbl_ref, q_ref, f_ref, i_ref, g_ref, cm_ref, mask_ref, ng_ref, o_ref, st_ref,
                 ee_ref, *, layer, nchunk):
    @pl.when(pl.program_id(1) == 0)
    def _():
        st_ref[...] = jnp.zeros_like(st_ref)

    lg = lbl_ref[...]
    e = jnp.exp(lg - jnp.max(lg, axis=0, keepdims=True))
    sm = e / jnp.sum(e, axis=0, keepdims=True)
    lb = jnp.sum(sm[0:layer + 1], axis=0, keepdims=True) - sm[0:1]
    log_lb = jnp.log(jnp.maximum(lb, LB_FLOOR))
    log_1m = jnp.log(1.0 - lb)
    cm = cm_ref[...]
    nl = HG_LEVELS * CHUNK

    def chunk_body(c, carry):
        r0 = pl.multiple_of(c * CHUNK, CHUNK)
        rows = pl.ds(r0, CHUNK)
        z = f_ref[rows, :].astype(F32)
        bb = log_1m + _log_sigmoid(z)
        lf = jnp.maximum(log_lb, bb) + jnp.log(1.0 + jnp.exp(-jnp.abs(log_lb - bb)))
        kin_all = (1.0 - lb) * _sigmoid(-z)
        ee_ref[...] = jnp.exp(_sel_dot2(cm, lf))
        for h in range(HEADS4):
            hs = slice(h * HEAD_DIM, (h + 1) * HEAD_DIM)
            q = q_ref[rows, hs].astype(F32)
            iv = i_ref[rows, hs].astype(F32)
            gv = g_ref[rows, hs].astype(F32)
            kin = kin_all[:, hs]
            i_act = _silu(iv).astype(BF16)
            scores = mask_ref[HG_LEVELS] * _dot_nt(q.astype(BF16), kin.astype(BF16))
            for l in range(HG_LEVELS):
                ql = (q * ee_ref[l * CHUNK:(l + 1) * CHUNK, hs]).astype(BF16)
                kl = (kin * ee_ref[nl + l * CHUNK:nl + (l + 1) * CHUNK, hs]).astype(BF16)
                scores = scores + mask_ref[l] * _dot_nt(ql, kl)
            eb = ee_ref[2 * nl:2 * nl + CHUNK, hs]
            esuf = ee_ref[2 * nl + CHUNK:2 * nl + 2 * CHUNK, hs]
            st = st_ref[h]
            o = _dot(scores.astype(BF16), i_act) + _dot_nt((q * eb).astype(BF16), st.astype(BF16))
            st_ref[h] = st * eb[CHUNK - 1:CHUNK, :] + _dot_tn(i_act, (kin * esuf).astype(BF16))
            ms = jnp.mean(o * o, axis=-1, keepdims=True)
            y = o * lax.rsqrt(ms + NORM_EPS) * ng_ref[:, hs] * _silu(gv)
            o_ref[rows, hs] = y.astype(o_ref.dtype)
        return carry

    lax.fori_loop(0, nchunk, chunk_body, 0, unroll=2)


def _hgrn(proj, lb_logits, norm_g, layer, bsz, seq):
    t = proj.shape[0]
    ts = min(512, seq)
    nt = seq // ts
    cm, masks = _hgrn_consts()
    w = BRANCH_WIDTH
    col = lambda c: (lambda b, i: (b * nt + i, c // w))
    return pl.pallas_call(
        functools.partial(_hgrn_kernel, layer=layer, nchunk=ts // CHUNK),
        grid=(bsz, nt),
        in_specs=[
            pl.BlockSpec(lb_logits.shape, lambda b, i: (0, 0)),
            pl.BlockSpec((ts, w), col(COL_HGQ)),
            pl.BlockSpec((ts, w), col(COL_HGF)),
            pl.BlockSpec((ts, w), col(COL_HGI)),
            pl.BlockSpec((ts, w), col(COL_HGG)),
            pl.BlockSpec(cm.shape, lambda b, i: (0, 0)),
            pl.BlockSpec(masks.shape, lambda b, i: (0, 0, 0)),
            pl.BlockSpec((1, w), lambda b, i: (0, 0)),
        ],
        out_specs=pl.BlockSpec((ts, w), lambda b, i: (b * nt + i, 0)),
        out_shape=jax.ShapeDtypeStruct((t, w), BF16),
        scratch_shapes=[pltpu.VMEM((HEADS4, HEAD_DIM, HEAD_DIM), F32),
                        pltpu.VMEM(((2 * HG_LEVELS + 2) * CHUNK, w), F32)],
        compiler_params=_params("parallel", "arbitrary"),
        name="hgrn2",
    )(lb_logits, proj, proj, proj, proj, jnp.asarray(cm, BF16), jnp.asarray(masks, F32),
      norm_g.reshape(1, w))


def _mlstm_kernel(q_ref, k_ref, v_ref, og_ref, gc_ref, gr_ref, cw_ref, cb_ref, brow_ref, bcol_ref,
                  ng_ref, o_ref, qbuf, kbuf, qact, kact, ct_ref, m_ref, *, ts):
    nchunk = ts // CHUNK
    w = BRANCH_WIDTH

    @pl.when(pl.program_id(1) == 0)
    def _():
        qbuf[0:8, :] = jnp.zeros((8, w), F32)
        kbuf[0:8, :] = jnp.zeros((8, w), F32)
        ct_ref[...] = jnp.zeros_like(ct_ref)
        m_ref[...] = jnp.zeros_like(m_ref)

    qbuf[8:8 + ts, :] = q_ref[...].astype(F32)
    kbuf[8:8 + ts, :] = k_ref[...].astype(F32)

    rs = lax.broadcasted_iota(jnp.int32, (CHUNK, CHUNK), 0)
    cs = lax.broadcasted_iota(jnp.int32, (CHUNK, CHUNK), 1)
    causal = cs <= rs
    tri_c = jnp.where(causal, 1.0, 0.0).astype(BF16)
    tri_r = jnp.where(rs <= cs, 1.0, 0.0).astype(BF16)
    ones_v = jnp.ones((CHUNK, HEAD_DIM), F32)
    kscale = HEAD_DIM ** -0.5

    for cc in range(nchunk):
        yq = jnp.zeros((CHUNK, w), F32) + cb_ref[:, 0:w]
        yk = jnp.zeros((CHUNK, w), F32) + cb_ref[:, w:2 * w]
        for j in range(ML_CONV):
            lo = cc * CHUNK + 8 - (ML_CONV - 1 - j)
            yq = yq + cw_ref[j:j + 1, 0:w] * qbuf[lo:lo + CHUNK, :]
            yk = yk + cw_ref[j:j + 1, w:2 * w] * kbuf[lo:lo + CHUNK, :]
        qact[cc * CHUNK:(cc + 1) * CHUNK, :] = _silu(yq).astype(BF16)
        kact[cc * CHUNK:(cc + 1) * CHUNK, :] = (_silu(yk) * kscale).astype(BF16)

    def chunk_body(c, carry):
        r0 = pl.multiple_of(c * CHUNK, CHUNK)
        rows = pl.ds(r0, CHUNK)
        pre_c = gc_ref[rows, :] + brow_ref[...]
        pre_r = gr_ref[:, rows] + bcol_ref[...]
        b_c = _sel_dot(tri_c, _log_sigmoid(pre_c))
        b_r = _dot_sel(_log_sigmoid(pre_r), tri_r)
        for h in range(HEADS4):
            hs = slice(h * HEAD_DIM, (h + 1) * HEAD_DIM)
            qh = qact[rows, hs]
            kh = kact[rows, hs]
            vx = jnp.concatenate([v_ref[rows, hs].astype(F32), ones_v], axis=1)
            bcol = b_c[:, HEADS4 + h:HEADS4 + h + 1]
            icol = pre_c[:, h:h + 1]
            brow = b_r[HEADS4 + h:HEADS4 + h + 1, :]
            irow = pre_r[h:h + 1, :]
            m = m_ref[h:h + 1, 0:1]
            log_d = jnp.where(causal, bcol - brow + irow, MASK_NEG)
            log_inter = bcol + m
            m_t = jnp.maximum(log_inter, jnp.max(log_d, axis=-1, keepdims=True))
            dmat = jnp.where(causal, jnp.exp(log_d - m_t), 0.0)
            inter = jnp.exp(log_inter - m_t)
            qk = _dot_nt(qh, kh) * dmat
            ct = ct_ref[h]
            hx = _dot(qk.astype(BF16), vx.astype(BF16)) + inter * _dot(qh, ct.astype(BF16))
            hout = hx[:, 0:HEAD_DIM] / jnp.maximum(jnp.abs(hx[:, HEAD_DIM:]), jnp.exp(-m_t))
            b_last = bcol[CHUNK - 1:CHUNK, :]
            m_new = jnp.maximum(b_last + m, jnp.max(b_last - brow + irow, axis=-1, keepdims=True))
            w_col = jnp.exp(b_last - bcol + icol - m_new)
            dec = jnp.exp(b_last + m - m_new)
            ct_ref[h] = dec * ct + _dot_tn(kh, (vx * w_col).astype(BF16))
            m_ref[h:h + 1, :] = jnp.broadcast_to(m_new, (1, 128))
            ms = jnp.mean(hout * hout, axis=-1, keepdims=True)
            y = hout * lax.rsqrt(ms + NORM_EPS) * ng_ref[:, hs] * _sigmoid(og_ref[rows, hs].astype(F32))
            o_ref[rows, hs] = y.astype(o_ref.dtype)
        return carry

    lax.fori_loop(0, nchunk, chunk_body, 0, unroll=2)
    qbuf[0:8, :] = qbuf[ts:ts + 8, :]
    kbuf[0:8, :] = kbuf[ts:ts + 8, :]


def _mlstm(proj, gates, gates_t, conv_w, conv_b, gate_b, norm_g, bsz, seq):
    t = proj.shape[0]
    ts = min(512, seq)
    nt = seq // ts
    w = BRANCH_WIDTH
    col = lambda c: (lambda b, i: (b * nt + i, c // w))
    brow = jnp.zeros((1, 128), F32).at[0, 0:8].set(gate_b)
    bcol = gate_b.reshape(8, 1)
    return pl.pallas_call(
        functools.partial(_mlstm_kernel, ts=ts),
        grid=(bsz, nt),
        in_specs=[
            pl.BlockSpec((ts, w), col(COL_MLQ)),
            pl.BlockSpec((ts, w), col(COL_MLK)),
            pl.BlockSpec((ts, w), col(COL_MLV)),
            pl.BlockSpec((ts, w), col(COL_MLO)),
            pl.BlockSpec((ts, 128), lambda b, i: (b * nt + i, 0)),
            pl.BlockSpec((8, ts), lambda b, i: (0, b * nt + i)),
            pl.BlockSpec((ML_CONV, 2 * w), lambda b, i: (0, 0)),
            pl.BlockSpec((1, 2 * w), lambda b, i: (0, 0)),
            pl.BlockSpec((1, 128), lambda b, i: (0, 0)),
            pl.BlockSpec((8, 1), lambda b, i: (0, 0)),
            pl.BlockSpec((1, w), lambda b, i: (0, 0)),
        ],
        out_specs=pl.BlockSpec((ts, w), lambda b, i: (b * nt + i, 0)),
        out_shape=jax.ShapeDtypeStruct((t, w), BF16),
        scratch_shapes=[
            pltpu.VMEM((ts + 8, w), F32),
            pltpu.VMEM((ts + 8, w), F32),
            pltpu.VMEM((ts, w), BF16),
            pltpu.VMEM((ts, w), BF16),
            pltpu.VMEM((HEADS4, HEAD_DIM, 2 * HEAD_DIM), F32),
            pltpu.VMEM((8, 128), F32),
        ],
        compiler_params=_params("parallel", "arbitrary"),
        name="mlstm",
    )(proj, proj, proj, proj, gates, gates_t, conv_w, conv_b.reshape(1, 2 * w), brow, bcol,
      norm_g.reshape(1, w))


def _merge_kernel(x_ref, ya_ref, yb_ref, yc_ref, bg_ref, wb_ref, wo_ref, g1_ref, o_ref):
    d = D_MODEL
    merged = jnp.zeros(x_ref.shape, F32)
    for g, y_ref in enumerate((ya_ref, yb_ref, yc_ref)):
        up = _dot(y_ref[...], wb_ref[g])
        merged = merged + _sigmoid(bg_ref[:, g * d:(g + 1) * d].astype(F32)) * up
    y = _dot(merged.astype(BF16), wo_ref[...])
    o_ref[...] = x_ref[...] + g1_ref[...] * y


def _merge(x2, ya, yb, yc, proj, w_branch, w_out, gate1, seq):
    t, d = x2.shape
    tm = min(512, seq)
    per_b = seq // tm
    w = BRANCH_WIDTH
    row = lambda i: (i, 0)
    return pl.pallas_call(
        _merge_kernel,
        grid=(t // tm,),
        in_specs=[
            pl.BlockSpec((tm, d), row),
            pl.BlockSpec((tm, w), row),
            pl.BlockSpec((tm, w), row),
            pl.BlockSpec((tm, w), row),
            pl.BlockSpec((tm, 3 * d), lambda i: (i, COL_BRG // (3 * d))),
            pl.BlockSpec((3, w, d), lambda i: (0, 0, 0)),
            pl.BlockSpec((d, d), lambda i: (0, 0)),
            pl.BlockSpec((None, 1, d), lambda i: (i // per_b, 0, 0)),
        ],
        out_specs=pl.BlockSpec((tm, d), row),
        out_shape=jax.ShapeDtypeStruct((t, d), F32),
        compiler_params=_params("parallel"),
        name="merge",
    )(x2, ya, yb, yc, proj, w_branch, w_out, gate1)


def _fold_keys_kernel(keys_ref, wq_ref, o_ref):
    o_ref[...] = lax.dot_general(keys_ref[...], wq_ref[...], (((1,), (1,)), ((), ())),
                                 preferred_element_type=F32, precision=lax.Precision.HIGHEST)


def _fold_keys(pk_keys, pk_wq):
    d = pk_wq.shape[0]
    sets = 2 * PK_HEADS
    keys = pk_keys.reshape(sets, PK_NKEYS, PK_HALF)
    wq = pk_wq.reshape(d, sets, PK_HALF).transpose(1, 0, 2)
    out = pl.pallas_call(
        _fold_keys_kernel,
        grid=(sets,),
        in_specs=[
            pl.BlockSpec((None, PK_NKEYS, PK_HALF), lambda s: (s, 0, 0)),
            pl.BlockSpec((None, d, PK_HALF), lambda s: (s, 0, 0)),
        ],
        out_specs=pl.BlockSpec((None, PK_NKEYS, d), lambda s: (s, 0, 0)),
        out_shape=jax.ShapeDtypeStruct((sets, PK_NKEYS, d), F32),
        compiler_params=_params("parallel"),
        name="fold_keys",
    )(keys, wq)
    return out.reshape(sets * PK_NKEYS, d)


def _peer_scores_kernel(x_ref, g_ref, sc_ref, sh_ref, wk_ref, ht_ref, st_ref):
    h = _rms_mod(x_ref[...], g_ref[...], sc_ref[...], sh_ref[...])
    ht = h.T.astype(BF16)
    ht_ref[...] = ht
    st_ref[...] = _dot(wk_ref[...], ht)


def _peer_scores(x2, g, scale, shift, wk_t, seq):
    t, d = x2.shape
    tb = min(512, seq)
    per_b = seq // tb
    nk = wk_t.shape[0]
    return pl.pallas_call(
        _peer_scores_kernel,
        grid=(t // tb,),
        in_specs=[
            pl.BlockSpec((tb, d), lambda i: (i, 0)),
            pl.BlockSpec((1, d), lambda i: (0, 0)),
            pl.BlockSpec((None, 1, d), lambda i: (i // per_b, 0, 0)),
            pl.BlockSpec((None, 1, d), lambda i: (i // per_b, 0, 0)),
            pl.BlockSpec((nk, d), lambda i: (0, 0)),
        ],
        out_specs=[
            pl.BlockSpec((d, tb), lambda i: (0, i)),
            pl.BlockSpec((nk, tb), lambda i: (0, i)),
        ],
        out_shape=[jax.ShapeDtypeStruct((d, t), BF16), jax.ShapeDtypeStruct((nk, t), F32)],
        compiler_params=_params("parallel"),
        name="peer_scores",
    )(x2, g.reshape(1, d), scale, shift, wk_t)


NEG_INF = float("-inf")
PK_ROUNDS = PK_TOPK + 1
PK_RANK_NONE = 32.0


def _top_rows(x, rounds, want_rank=False):
    vals = []
    rank = jnp.full(x.shape, PK_RANK_NONE, F32)
    for r in range(rounds):
        m = jnp.max(x, axis=0, keepdims=True)
        vals.append(m)
        eq = x == m
        if want_rank:
            rank = jnp.where(eq, float(r), rank)
        if r + 1 < rounds:
            x = jnp.where(eq, NEG_INF, x)
    return (vals, rank) if want_rank else vals


def _count_ge(v, need):
    assert PK_TOPK == 16
    w = jnp.where
    c1 = v[7] >= need
    c2 = w(c1, v[11], v[3]) >= need
    c3 = w(c1, w(c2, v[13], v[9]), w(c2, v[5], v[1])) >= need
    c4 = w(c1, w(c2, w(c3, v[14], v[12]), w(c3, v[10], v[8])),
           w(c2, w(c3, v[6], v[4]), w(c3, v[2], v[0]))) >= need
    c5 = v[15] >= need
    return (w(c1, 8.0, 0.0) + w(c2, 4.0, 0.0) + w(c3, 2.0, 0.0) + w(c4, 1.0, 0.0) + w(c5, 1.0, 0.0))


def _peer_stats_kernel(st_ref, cnt_ref, e1_ref, r2_ref, e2_ref):
    tb = st_ref.shape[-1]
    for h in range(PK_HEADS):
        s1 = st_ref[h, 0]
        s2 = st_ref[h, 1]
        v1 = _top_rows(s1, PK_ROUNDS)
        v2, rank2 = _top_rows(s2, PK_ROUNDS, want_rank=True)
        v2_16 = jnp.concatenate(v2[0:16], axis=0)
        blocks = [v1[0] + v2_16]
        for a in range(1, 8):
            blocks.append(v1[a] + v2_16[0:8])
        blocks.append(jnp.concatenate(v1[8:16], axis=0) + v2[0])
        blocks.append(jnp.concatenate(
            [v1[0] + v2[16], v1[16] + v2[0], jnp.full((6, tb), NEG_INF, F32)], axis=0))
        cand = jnp.concatenate(blocks, axis=0)
        top = _top_rows(cand, PK_ROUNDS)
        thr = 0.5 * (top[PK_TOPK - 1] + top[PK_TOPK])
        zsum = jnp.zeros((1, tb), F32)
        for r in range(PK_TOPK):
            zsum = zsum + jnp.exp(top[r] - top[0])
        need = thr - s1
        cnt_ref[h] = _count_ge(v2, need)
        e1_ref[h] = jnp.exp(s1 - v1[0]) / zsum
        r2_ref[h] = rank2.astype(BF16)
        e2_ref[h] = jnp.exp(s2 - v2[0]).astype(BF16)


def _peer_stats(st):
    nk, t = st.shape
    tb = min(256, t)
    st4 = st.reshape(PK_HEADS, 2, PK_NKEYS, t)
    shp = lambda dt: jax.ShapeDtypeStruct((PK_HEADS, PK_NKEYS, t), dt)
    spec = pl.BlockSpec((PK_HEADS, PK_NKEYS, tb), lambda i: (0, 0, i))
    return pl.pallas_call(
        _peer_stats_kernel,
        grid=(t // tb,),
        in_specs=[pl.BlockSpec((PK_HEADS, 2, PK_NKEYS, tb), lambda i: (0, 0, 0, i))],
        out_specs=[spec, spec, spec, spec],
        out_shape=[shp(F32), shp(F32), shp(BF16), shp(BF16)],
        compiler_params=_params("parallel"),
        name="peer_stats",
    )(st4)


PK_ITILE = 8
PK_DENSE_TB = 512
GELU_C0 = 0.7978845608028654
GELU_C1 = 0.044715


def _gelu_tanh(a):
    return 0.5 * a * (1.0 + jnp.tanh(GELU_C0 * (a + GELU_C1 * (a * a * a))))


def _peer_tile_weights(a, row0, cnt_ref, e1_ref, r2_ref, e2_ref, p_ref):
    tb = a.shape[1]
    shape = (PK_NKEYS, tb)
    zero = jnp.zeros(shape, BF16)
    for il in range(PK_ITILE):
        g = _gelu_tanh(a[il * PK_NKEYS:(il + 1) * PK_NKEYS, :].astype(BF16))
        wsum = zero
        for h in range(PK_HEADS):
            cb = jnp.broadcast_to(cnt_ref[h, row0 + il:row0 + il + 1, :], shape).astype(BF16)
            eb = jnp.broadcast_to(e1_ref[h, row0 + il:row0 + il + 1, :], shape).astype(BF16)
            wsum = wsum + eb * jnp.where(cb > r2_ref[h], e2_ref[h], zero)
        p_ref[il * PK_NKEYS:(il + 1) * PK_NKEYS, :] = g * wsum


def _peer_dense_kernel(x_ref, g2_ref, ht_ref, u0_ref, ua_ref, ub_ref, vt_ref, cnt_ref, e1_ref,
                       r2_ref, e2_ref, o_ref, acc_ref, a_ref, pa_ref, pb_ref):
    g = pl.program_id(1)
    te = pa_ref.shape[0]
    ht = ht_ref[...]

    @pl.when(g == 0)
    def _():
        acc_ref[...] = jnp.zeros_like(acc_ref)
        a_ref[0] = _dot(u0_ref[...], ht)

    _peer_tile_weights(a_ref.at[g % 2], 0, cnt_ref, e1_ref, r2_ref, e2_ref, pa_ref)
    a_odd = _dot(ua_ref[...], ht)
    acc_ref[...] += _dot(vt_ref[0], pa_ref[...])
    _peer_tile_weights(a_odd, PK_ITILE, cnt_ref, e1_ref, r2_ref, e2_ref, pb_ref)
    acc_ref[...] += _dot(vt_ref[1], pb_ref[...])
    a_ref[(g + 1) % 2] = _dot(ub_ref[...], ht)

    @pl.when(g == pl.num_programs(1) - 1)
    def _():
        o_ref[...] = x_ref[...] + g2_ref[...] * acc_ref[...].T


def _peer_dense(x2, gate2, ht, u_bf, vt_bf, cnt, e1, r2, e2, seq):
    t, d = x2.shape
    tb = min(PK_DENSE_TB, seq)
    per_b = seq // tb
    te = PK_ITILE * PK_NKEYS
    ntile = u_bf.shape[0] // te
    u_spec = lambda f: pl.BlockSpec((te, d), lambda i, g: (f(g), 0))
    pair_i = pl.BlockSpec((PK_HEADS, 2 * PK_ITILE, tb), lambda i, g: (0, g, i))
    full_j = pl.BlockSpec((PK_HEADS, PK_NKEYS, tb), lambda i, g: (0, 0, i))
    return pl.pallas_call(
        _peer_dense_kernel,
        grid=(t // tb, ntile // 2),
        in_specs=[
            pl.BlockSpec((tb, d), lambda i, g: (i, 0)),
            pl.BlockSpec((None, 1, d), lambda i, g: (i // per_b, 0, 0)),
            pl.BlockSpec((d, tb), lambda i, g: (0, i)),
            u_spec(lambda g: 0), u_spec(lambda g: 2 * g + 1),
            u_spec(lambda g: jnp.minimum(2 * g + 2, ntile - 1)),
            pl.BlockSpec((2, d, te), lambda i, g: (g, 0, 0)),
            pair_i, pair_i, full_j, full_j,
        ],
        out_specs=pl.BlockSpec((tb, d), lambda i, g: (i, 0)),
        out_shape=jax.ShapeDtypeStruct((t, d), F32),
        scratch_shapes=[pltpu.VMEM((d, tb), F32), pltpu.VMEM((2, te, tb), F32),
                        pltpu.VMEM((te, tb), BF16), pltpu.VMEM((te, tb), BF16)],
        compiler_params=_params("parallel", "arbitrary"),
        name="peer_dense",
    )(x2, gate2, ht, u_bf, u_bf, u_bf, vt_bf, cnt, e1, r2, e2)


def _final_norm_kernel(x_ref, g_ref, o_ref):
    x = x_ref[...]
    ms = jnp.mean(x * x, axis=-1, keepdims=True)
    o_ref[...] = x * lax.rsqrt(ms + NORM_EPS) * g_ref[...]


def _final_norm(x2, g):
    t, d = x2.shape
    tm = min(1024, t)
    return pl.pallas_call(
        _final_norm_kernel,
        grid=(t // tm,),
        in_specs=[pl.BlockSpec((tm, d), lambda i: (i, 0)), pl.BlockSpec((1, d), lambda i: (0, 0))],
        out_specs=pl.BlockSpec((tm, d), lambda i: (i, 0)),
        out_shape=jax.ShapeDtypeStruct((t, d), F32),
        compiler_params=_params("parallel"),
        name="final_norm",
    )(x2, g.reshape(1, d))


def _permute_w_in(w_in):
    main = jnp.concatenate([w_in[..., 5640:8712], w_in[..., 3584:4608], w_in[..., 0:3584],
                            w_in[..., 4608:5632]], axis=-1).astype(BF16)
    gates = jnp.pad(w_in[..., 5632:5640], ((0, 0), (0, 0), (0, 120))).astype(BF16)
    return main, gates


def kernel(x, c, mod_w, mod_b, norm_mix_g, norm_ffn_g, w_in, ml_conv_w, ml_conv_b, ml_gate_b,
           hg_lb_logits, hg_norm_g, ml_norm_g, w_branch, w_out, pk_wq, pk_keys, pk_u, pk_v, final_g):
    bsz, seq, d = x.shape
    depth = w_in.shape[0]
    t = bsz * seq
    x2 = x.reshape(t, d)

    mod = _modulation(c, mod_w, mod_b).reshape(depth, bsz, 6, 1, d)
    w_main, w_gates = _permute_w_in(w_in)
    w_branch_bf = w_branch.astype(BF16)
    w_out_bf = w_out.astype(BF16)
    u_bf = pk_u.astype(BF16)
    te = PK_ITILE * PK_NKEYS
    vt_bf = jnp.swapaxes(pk_v.astype(BF16).reshape(depth, pk_v.shape[1] // te, te, d), 2, 3)

    for l in range(depth):
        shift1, scale1, gate1, shift2, scale2, gate2 = (mod[l, :, r] for r in range(6))
        proj, gates = _norm_proj(x2, norm_mix_g[l], scale1, shift1, w_main[l], w_gates[l], seq)
        ya = _sb_attention(proj, bsz, seq)
        yb = _hgrn(proj, hg_lb_logits, hg_norm_g[l], l, bsz, seq)
        yc = _mlstm(proj, gates, gates[:, 0:8].T, ml_conv_w[l], ml_conv_b[l], ml_gate_b[l],
                    ml_norm_g[l], bsz, seq)
        x2 = _merge(x2, ya, yb, yc, proj, w_branch_bf[l], w_out_bf[l], gate1, seq)
        wk_t = _fold_keys(pk_keys[l], pk_wq[l]).astype(BF16)
        ht, st = _peer_scores(x2, norm_ffn_g[l], scale2, shift2, wk_t, seq)
        cnt, e1, r2, e2 = _peer_stats(st)
        x2 = _peer_dense(x2, gate2, ht, u_bf[l], vt_bf[l], cnt, e1, r2, e2, seq)
    return _final_norm(x2, final_g).reshape(bsz, seq, d)
```

```python
import functools

import numpy as np
import jax
import jax.numpy as jnp
from jax import lax
from jax.experimental import pallas as pl
from jax.experimental.pallas import tpu as pltpu

F32 = jnp.float32
BF16 = jnp.bfloat16

D_MODEL = 1024
DEPTH = 4
NORM_EPS = 1e-6
MASK_NEG = -1e30
LB_FLOOR = 1e-30
SB_HEAD_DIM = 64
HEADS4 = 4
HEAD_DIM = 128
BRANCH_WIDTH = 512
ML_CONV = 4
PK_HEADS = 8
PK_NKEYS = 128
PK_HALF = 64
PK_TOPK = 16

PROJ_WIDTH = 8704
COL_BRG = 0
COL_MLQ = 3072
COL_MLK = 3584
COL_SBQ = 4096
COL_SBK = 4608
COL_SBV = 5120
COL_HGQ = 5632
COL_HGF = 6144
COL_HGI = 6656
COL_HGG = 7168
COL_MLV = 7680
COL_MLO = 8192

CHUNK = 128
VMEM_LIMIT = 56 * 1024 * 1024


def _dot(a, b):
    return jnp.dot(a, b, preferred_element_type=F32)


def _dot_nt(a, b):
    return lax.dot_general(a, b, (((1,), (1,)), ((), ())), preferred_element_type=F32)


def _dot_tn(a, b):
    return lax.dot_general(a, b, (((0,), (0,)), ((), ())), preferred_element_type=F32)


def _split3(x):
    hi = x.astype(BF16)
    r = x - hi.astype(F32)
    mid = r.astype(BF16)
    lo = (r - mid.astype(F32)).astype(BF16)
    return hi, mid, lo


def _sel_dot(m01, x):
    hi, mid, lo = _split3(x)
    return _dot(m01, hi) + _dot(m01, mid) + _dot(m01, lo)


def _dot_sel(x, m01):
    hi, mid, lo = _split3(x)
    return _dot(hi, m01) + _dot(mid, m01) + _dot(lo, m01)


def _sel_dot2(m01, x):
    hi = x.astype(BF16)
    lo = (x - hi.astype(F32)).astype(BF16)
    return _dot(m01, hi) + _dot(m01, lo)


def _sigmoid(x):
    return 1.0 / (1.0 + jnp.exp(-x))


def _silu(x):
    return x * _sigmoid(x)


def _log_sigmoid(x):
    return jnp.minimum(x, 0.0) - jnp.log(1.0 + jnp.exp(-jnp.abs(x)))


def _params(*sem):
    return pltpu.CompilerParams(dimension_semantics=sem, vmem_limit_bytes=VMEM_LIMIT)


def _mod_kernel(c_ref, w_ref, b_ref, o_ref):
    cond = _silu(c_ref[...])
    o_ref[...] = jnp.dot(cond, w_ref[...], preferred_element_type=F32,
                         precision=lax.Precision.HIGHEST) + b_ref[...]


def _modulation(c, mod_w, mod_b):
    depth, d, n = mod_w.shape
    bsz = c.shape[0]
    tn = d
    return pl.pallas_call(
        _mod_kernel,
        grid=(depth, n // tn),
        in_specs=[
            pl.BlockSpec((bsz, d), lambda l, j: (0, 0)),
            pl.BlockSpec((None, d, tn), lambda l, j: (l, 0, j)),
            pl.BlockSpec((None, 1, tn), lambda l, j: (l, 0, j)),
        ],
        out_specs=pl.BlockSpec((None, bsz, tn), lambda l, j: (l, 0, j)),
        out_shape=jax.ShapeDtypeStruct((depth, bsz, n), F32),
        compiler_params=_params("parallel", "parallel"),
        name="modulation",
    )(c, mod_w, mod_b.reshape(depth, 1, n))


def _rms_mod(x, g, scale, shift):
    ms = jnp.mean(x * x, axis=-1, keepdims=True)
    y = x * lax.rsqrt(ms + NORM_EPS) * g
    return y * (1.0 + scale) + shift


def _norm_proj_kernel(x_ref, g_ref, sc_ref, sh_ref, w_ref, wg_ref, proj_ref, gates_ref, h_ref):
    @pl.when(pl.program_id(1) == 0)
    def _():
        hb = _rms_mod(x_ref[...], g_ref[...], sc_ref[...], sh_ref[...]).astype(BF16)
        h_ref[...] = hb
        gates_ref[...] = _dot(hb, wg_ref[...])

    proj_ref[...] = _dot(h_ref[...], w_ref[...]).astype(BF16)


def _norm_proj(x2, g, scale, shift, w_main, w_gates, seq):
    t, d = x2.shape
    tm = min(1024, seq)
    tn = PROJ_WIDTH // 4
    per_b = seq // tm
    n = w_main.shape[1]
    return pl.pallas_call(
        _norm_proj_kernel,
        grid=(t // tm, n // tn),
        in_specs=[
            pl.BlockSpec((tm, d), lambda i, j: (i, 0)),
            pl.BlockSpec((1, d), lambda i, j: (0, 0)),
            pl.BlockSpec((None, 1, d), lambda i, j: (i // per_b, 0, 0)),
            pl.BlockSpec((None, 1, d), lambda i, j: (i // per_b, 0, 0)),
            pl.BlockSpec((d, tn), lambda i, j: (0, j)),
            pl.BlockSpec((d, 128), lambda i, j: (0, 0)),
        ],
        out_specs=[
            pl.BlockSpec((tm, tn), lambda i, j: (i, j)),
            pl.BlockSpec((tm, 128), lambda i, j: (i, 0)),
        ],
        out_shape=[jax.ShapeDtypeStruct((t, n), BF16), jax.ShapeDtypeStruct((t, 128), F32)],
        scratch_shapes=[pltpu.VMEM((tm, d), BF16)],
        compiler_params=_params("parallel", "arbitrary"),
        name="norm_proj",
    )(x2, g.reshape(1, d), scale, shift, w_main, w_gates)


SB_TQ = 512
SB_TK = 256
SB_PAIRS = 2
SB_UNROLL = 2
assert SB_UNROLL * SB_TK == SB_TQ
LOG2_E = 1.4426950408889634


def _sb_kernel(q_ref, k_ref, v_ref, o_ref):
    qi = pl.program_id(2)
    lane = lax.broadcasted_iota(jnp.int32, (1, 128), 1)
    first = lane < SB_HEAD_DIM
    qh = []
    for p in range(SB_PAIRS):
        q2 = q_ref[:, p * 128:(p + 1) * 128] * (SB_HEAD_DIM ** -0.5 * LOG2_E)
        zero = jnp.zeros_like(q2)
        qh.append((jnp.where(first, q2, zero), jnp.where(first, zero, q2)))
    t_idx = qi * SB_TQ + lax.broadcasted_iota(jnp.int32, (SB_TQ, 1), 0)
    rs = lax.broadcasted_iota(jnp.int32, (SB_TK, SB_TK), 0)
    cs = lax.broadcasted_iota(jnp.int32, (SB_TK, SB_TK), 1)
    tri = jnp.where(rs > cs, 1.0, 0.0).astype(BF16)
    tri_ext = jnp.concatenate([tri, jnp.ones_like(tri)], axis=1)
    nkb = (qi + 1) * (SB_TQ // SB_TK)

    def body(it, carry, masked):
        cc = list(carry[0])
        acc = list(carry[1])
        for sub in range(SB_UNROLL):
            kb = nkb - 1 - (it * SB_UNROLL + sub)
            r0 = pl.multiple_of(kb * SB_TK, SB_TK)
            past = (kb * SB_TK + lax.broadcasted_iota(jnp.int32, (1, SB_TK), 1)) < t_idx
            for p in range(SB_PAIRS):
                ks = k_ref[pl.ds(r0, SB_TK), p * 128:(p + 1) * 128]
                vs = v_ref[pl.ds(r0, SB_TK), p * 128:(p + 1) * 128]
                vzero = jnp.zeros_like(vs)
                vh = (jnp.where(first, vs, vzero), jnp.where(first, vzero, vs))
                for h in range(2):
                    z = _dot_nt(qh[p][h], ks)
                    ls = jnp.minimum(z, 0.0) - jnp.log2(1.0 + jnp.exp2(-jnp.abs(z)))
                    lk = ls - z
                    if masked:
                        lk = jnp.where(past, lk, 0.0)
                    sums = _dot(lk.astype(BF16), tri_ext)
                    w = jnp.exp2(ls + sums[:, 0:SB_TK] + cc[2 * p + h])
                    if masked:
                        w = jnp.where(past, w, 0.0)
                    acc[p] = acc[p] + _dot(w.astype(BF16), vh[h])
                    cc[2 * p + h] = cc[2 * p + h] + sums[:, SB_TK:]
        return tuple(cc), tuple(acc)

    init = (tuple(jnp.zeros((SB_TQ, SB_TK), F32) for _ in range(2 * SB_PAIRS)),
            tuple(jnp.zeros((SB_TQ, 128), F32) for _ in range(SB_PAIRS)))
    carry = body(0, init, True)
    _, acc = lax.fori_loop(1, nkb // SB_UNROLL, functools.partial(body, masked=False), carry)
    for p in range(SB_PAIRS):
        o_ref[:, p * 128:(p + 1) * 128] = acc[p].astype(o_ref.dtype)


def _sb_attention(proj, bsz, seq):
    t = proj.shape[0]
    nq = seq // SB_TQ
    wd = 128 * SB_PAIRS
    qb, kb, vb = COL_SBQ // wd, COL_SBK // wd, COL_SBV // wd
    return pl.pallas_call(
        _sb_kernel,
        grid=(bsz, BRANCH_WIDTH // wd, nq),
        in_specs=[
            pl.BlockSpec((SB_TQ, wd), lambda b, p, i: (b * nq + i, qb + p)),
            pl.BlockSpec((seq, wd), lambda b, p, i: (b, kb + p)),
            pl.BlockSpec((seq, wd), lambda b, p, i: (b, vb + p)),
        ],
        out_specs=pl.BlockSpec((SB_TQ, wd), lambda b, p, i: (b * nq + i, p)),
        out_shape=jax.ShapeDtypeStruct((t, BRANCH_WIDTH), BF16),
        compiler_params=_params("parallel", "parallel", "arbitrary"),
        name="sb_attention",
    )(proj, proj, proj)


def _hgrn_consts():
    n_l = CHUNK
    t = np.arange(n_l)[:, None]
    u = np.arange(n_l)[None, :]
    dq, dk, masks = [], [], []
    n = n_l
    while n >= 2:
        half = n // 2
        a_t = (t // n) * n
        mid = a_t + half - 1
        up_t = (t - a_t) >= half
        up_u = (u - (u // n) * n) >= half
        dq.append(up_t & (u > mid) & (u <= t))
        dk.append((~up_t) & (u > t) & (u <= mid))
        masks.append(((t // n) == (u // n)) & up_t & (~up_u))
        n //= 2
    masks.append(t == u)
    cm = np.concatenate(dq + dk + [u <= t, u > t], axis=0).astype(np.float32)
    return cm, np.stack(masks).astype(np.float32)


HG_LEVELS = 7


def _hgrn_kernel(lbl_ref, q_ref, f_ref, i_ref, g_ref, cm_ref, mask_ref, ng_ref, o_ref, st_ref,
                 ee_ref, *, layer, nchunk):
    @pl.when(pl.program_id(1) == 0)
    def _():
        st_ref[...] = jnp.zeros_like(st_ref)

    lg = lbl_ref[...]
    e = jnp.exp(lg - jnp.max(lg, axis=0, keepdims=True))
    sm = e / jnp.sum(e, axis=0, keepdims=True)
    lb = jnp.sum(sm[0:layer + 1], axis=0, keepdims=True) - sm[0:1]
    log_lb = jnp.log(jnp.maximum(lb, LB_FLOOR))
    log_1m = jnp.log(1.0 - lb)
    cm = cm_ref[...]
    nl = HG_LEVELS * CHUNK

    def chunk_body(c, carry):
        r0 = pl.multiple_of(c * CHUNK, CHUNK)
        rows = pl.ds(r0, CHUNK)
        z = f_ref[rows, :].astype(F32)
        bb = log_1m + _log_sigmoid(z)
        lf = jnp.maximum(log_lb, bb) + jnp.log(1.0 + jnp.exp(-jnp.abs(log_lb - bb)))
        kin_all = (1.0 - lb) * _sigmoid(-z)
        ee_ref[...] = jnp.exp(_sel_dot2(cm, lf))
        for h in range(HEADS4):
            hs = slice(h * HEAD_DIM, (h + 1) * HEAD_DIM)
            q = q_ref[rows, hs].astype(F32)
            iv = i_ref[rows, hs].astype(F32)
            gv = g_ref[rows, hs].astype(F32)
            kin = kin_all[:, hs]
            i_act = _silu(iv).astype(BF16)
            scores = mask_ref[HG_LEVELS] * _dot_nt(q.astype(BF16), kin.astype(BF16))
            for l in range(HG_LEVELS):
                ql = (q * ee_ref[l * CHUNK:(l + 1) * CHUNK, hs]).astype(BF16)
                kl = (kin * ee_ref[nl + l * CHUNK:nl + (l + 1) * CHUNK, hs]).astype(BF16)
                scores = scores + mask_ref[l] * _dot_nt(ql, kl)
            eb = ee_ref[2 * nl:2 * nl + CHUNK, hs]
            esuf = ee_ref[2 * nl + CHUNK:2 * nl + 2 * CHUNK, hs]
            st = st_ref[h]
            o = _dot(scores.astype(BF16), i_act) + _dot_nt((q * eb).astype(BF16), st.astype(BF16))
            st_ref[h] = st * eb[CHUNK - 1:CHUNK, :] + _dot_tn(i_act, (kin * esuf).astype(BF16))
            ms = jnp.mean(o * o, axis=-1, keepdims=True)
            y = o * lax.rsqrt(ms + NORM_EPS) * ng_ref[:, hs] * _silu(gv)
            o_ref[rows, hs] = y.astype(o_ref.dtype)
        return carry

    lax.fori_loop(0, nchunk, chunk_body, 0, unroll=2)


def _hgrn(proj, lb_logits, norm_g, layer, bsz, seq):
    t = proj.shape[0]
    ts = min(512, seq)
    nt = seq // ts
    cm, masks = _hgrn_consts()
    w = BRANCH_WIDTH
    col = lambda c: (lambda b, i: (b * nt + i, c // w))
    return pl.pallas_call(
        functools.partial(_hgrn_kernel, layer=layer, nchunk=ts // CHUNK),
        grid=(bsz, nt),
        in_specs=[
            pl.BlockSpec(lb_logits.shape, lambda b, i: (0, 0)),
            pl.BlockSpec((ts, w), col(COL_HGQ)),
            pl.BlockSpec((ts, w), col(COL_HGF)),
            pl.BlockSpec((ts, w), col(COL_HGI)),
            pl.BlockSpec((ts, w), col(COL_HGG)),
            pl.BlockSpec(cm.shape, lambda b, i: (0, 0)),
            pl.BlockSpec(masks.shape, lambda b, i: (0, 0, 0)),
            pl.BlockSpec((1, w), lambda b, i: (0, 0)),
        ],
        out_specs=pl.BlockSpec((ts, w), lambda b, i: (b * nt + i, 0)),
        out_shape=jax.ShapeDtypeStruct((t, w), BF16),
        scratch_shapes=[pltpu.VMEM((HEADS4, HEAD_DIM, HEAD_DIM), F32),
                        pltpu.VMEM(((2 * HG_LEVELS + 2) * CHUNK, w), F32)],
        compiler_params=_params("parallel", "arbitrary"),
        name="hgrn2",
    )(lb_logits, proj, proj, proj, proj, jnp.asarray(cm, BF16), jnp.asarray(masks, F32),
      norm_g.reshape(1, w))


def _mlstm_kernel(q_ref, k_ref, v_ref, og_ref, gc_ref, gr_ref, cw_ref, cb_ref, brow_ref, bcol_ref,
                  ng_ref, o_ref, qbuf, kbuf, qact, kact, ct_ref, m_ref, *, ts):
    nchunk = ts // CHUNK
    w = BRANCH_WIDTH

    @pl.when(pl.program_id(1) == 0)
    def _():
        qbuf[0:8, :] = jnp.zeros((8, w), F32)
        kbuf[0:8, :] = jnp.zeros((8, w), F32)
        ct_ref[...] = jnp.zeros_like(ct_ref)
        m_ref[...] = jnp.zeros_like(m_ref)

    qbuf[8:8 + ts, :] = q_ref[...].astype(F32)
    kbuf[8:8 + ts, :] = k_ref[...].astype(F32)

    rs = lax.broadcasted_iota(jnp.int32, (CHUNK, CHUNK), 0)
    cs = lax.broadcasted_iota(jnp.int32, (CHUNK, CHUNK), 1)
    causal = cs <= rs
    tri_c = jnp.where(causal, 1.0, 0.0).astype(BF16)
    tri_r = jnp.where(rs <= cs, 1.0, 0.0).astype(BF16)
    ones_v = jnp.ones((CHUNK, HEAD_DIM), F32)
    kscale = HEAD_DIM ** -0.5

    for cc in range(nchunk):
        yq = jnp.zeros((CHUNK, w), F32) + cb_ref[:, 0:w]
        yk = jnp.zeros((CHUNK, w), F32) + cb_ref[:, w:2 * w]
        for j in range(ML_CONV):
            lo = cc * CHUNK + 8 - (ML_CONV - 1 - j)
            yq = yq + cw_ref[j:j + 1, 0:w] * qbuf[lo:lo + CHUNK, :]
            yk = yk + cw_ref[j:j + 1, w:2 * w] * kbuf[lo:lo + CHUNK, :]
        qact[cc * CHUNK:(cc + 1) * CHUNK, :] = _silu(yq).astype(BF16)
        kact[cc * CHUNK:(cc + 1) * CHUNK, :] = (_silu(yk) * kscale).astype(BF16)

    def chunk_body(c, carry):
        r0 = pl.multiple_of(c * CHUNK, CHUNK)
        rows = pl.ds(r0, CHUNK)
        pre_c = gc_ref[rows, :] + brow_ref[...]
        pre_r = gr_ref[:, rows] + bcol_ref[...]
        b_c = _sel_dot(tri_c, _log_sigmoid(pre_c))
        b_r = _dot_sel(_log_sigmoid(pre_r), tri_r)
        for h in range(HEADS4):
            hs = slice(h * HEAD_DIM, (h + 1) * HEAD_DIM)
            qh = qact[rows, hs]
            kh = kact[rows, hs]
            vx = jnp.concatenate([v_ref[rows, hs].astype(F32), ones_v], axis=1)
            bcol = b_c[:, HEADS4 + h:HEADS4 + h + 1]
            icol = pre_c[:, h:h + 1]
            brow = b_r[HEADS4 + h:HEADS4 + h + 1, :]
            irow = pre_r[h:h + 1, :]
            m = m_ref[h:h + 1, 0:1]
            log_d = jnp.where(causal, bcol - brow + irow, MASK_NEG)
            log_inter = bcol + m
            m_t = jnp.maximum(log_inter, jnp.max(log_d, axis=-1, keepdims=True))
            dmat = jnp.where(causal, jnp.exp(log_d - m_t), 0.0)
            inter = jnp.exp(log_inter - m_t)
            qk = _dot_nt(qh, kh) * dmat
            ct = ct_ref[h]
            hx = _dot(qk.astype(BF16), vx.astype(BF16)) + inter * _dot(qh, ct.astype(BF16))
            hout = hx[:, 0:HEAD_DIM] / jnp.maximum(jnp.abs(hx[:, HEAD_DIM:]), jnp.exp(-m_t))
            b_last = bcol[CHUNK - 1:CHUNK, :]
            m_new = jnp.maximum(b_last + m, jnp.max(b_last - brow + irow, axis=-1, keepdims=True))
            w_col = jnp.exp(b_last - bcol + icol - m_new)
            dec = jnp.exp(b_last + m - m_new)
            ct_ref[h] = dec * ct + _dot_tn(kh, (vx * w_col).astype(BF16))
            m_ref[h:h + 1, :] = jnp.broadcast_to(m_new, (1, 128))
            ms = jnp.mean(hout * hout, axis=-1, keepdims=True)
            y = hout * lax.rsqrt(ms + NORM_EPS) * ng_ref[:, hs] * _sigmoid(og_ref[rows, hs].astype(F32))
            o_ref[rows, hs] = y.astype(o_ref.dtype)
        return carry

    lax.fori_loop(0, nchunk, chunk_body, 0, unroll=2)
    qbuf[0:8, :] = qbuf[ts:ts + 8, :]
    kbuf[0:8, :] = kbuf[ts:ts + 8, :]


def _mlstm(proj, gates, gates_t, conv_w, conv_b, gate_b, norm_g, bsz, seq):
    t = proj.shape[0]
    ts = min(512, seq)
    nt = seq // ts
    w = BRANCH_WIDTH
    col = lambda c: (lambda b, i: (b * nt + i, c // w))
    brow = jnp.zeros((1, 128), F32).at[0, 0:8].set(gate_b)
    bcol = gate_b.reshape(8, 1)
    return pl.pallas_call(
        functools.partial(_mlstm_kernel, ts=ts),
        grid=(bsz, nt),
        in_specs=[
            pl.BlockSpec((ts, w), col(COL_MLQ)),
            pl.BlockSpec((ts, w), col(COL_MLK)),
            pl.BlockSpec((ts, w), col(COL_MLV)),
            pl.BlockSpec((ts, w), col(COL_MLO)),
            pl.BlockSpec((ts, 128), lambda b, i: (b * nt + i, 0)),
            pl.BlockSpec((8, ts), lambda b, i: (0, b * nt + i)),
            pl.BlockSpec((ML_CONV, 2 * w), lambda b, i: (0, 0)),
            pl.BlockSpec((1, 2 * w), lambda b, i: (0, 0)),
            pl.BlockSpec((1, 128), lambda b, i: (0, 0)),
            pl.BlockSpec((8, 1), lambda b, i: (0, 0)),
            pl.BlockSpec((1, w), lambda b, i: (0, 0)),
        ],
        out_specs=pl.BlockSpec((ts, w), lambda b, i: (b * nt + i, 0)),
        out_shape=jax.ShapeDtypeStruct((t, w), BF16),
        scratch_shapes=[
            pltpu.VMEM((ts + 8, w), F32),
            pltpu.VMEM((ts + 8, w), F32),
            pltpu.VMEM((ts, w), BF16),
            pltpu.VMEM((ts, w), BF16),
            pltpu.VMEM((HEADS4, HEAD_DIM, 2 * HEAD_DIM), F32),
            pltpu.VMEM((8, 128), F32),
        ],
        compiler_params=_params("parallel", "arbitrary"),
        name="mlstm",
    )(proj, proj, proj, proj, gates, gates_t, conv_w, conv_b.reshape(1, 2 * w), brow, bcol,
      norm_g.reshape(1, w))


def _merge_kernel(x_ref, ya_ref, yb_ref, yc_ref, bg_ref, wb_ref, wo_ref, g1_ref, o_ref):
    d = D_MODEL
    merged = jnp.zeros(x_ref.shape, F32)
    for g, y_ref in enumerate((ya_ref, yb_ref, yc_ref)):
        up = _dot(y_ref[...], wb_ref[g])
        merged = merged + _sigmoid(bg_ref[:, g * d:(g + 1) * d].astype(F32)) * up
    y = _dot(merged.astype(BF16), wo_ref[...])
    o_ref[...] = x_ref[...] + g1_ref[...] * y


def _merge(x2, ya, yb, yc, proj, w_branch, w_out, gate1, seq):
    t, d = x2.shape
    tm = min(512, seq)
    per_b = seq // tm
    w = BRANCH_WIDTH
    row = lambda i: (i, 0)
    return pl.pallas_call(
        _merge_kernel,
        grid=(t // tm,),
        in_specs=[
            pl.BlockSpec((tm, d), row),
            pl.BlockSpec((tm, w), row),
            pl.BlockSpec((tm, w), row),
            pl.BlockSpec((tm, w), row),
            pl.BlockSpec((tm, 3 * d), lambda i: (i, COL_BRG // (3 * d))),
            pl.BlockSpec((3, w, d), lambda i: (0, 0, 0)),
            pl.BlockSpec((d, d), lambda i: (0, 0)),
            pl.BlockSpec((None, 1, d), lambda i: (i // per_b, 0, 0)),
        ],
        out_specs=pl.BlockSpec((tm, d), row),
        out_shape=jax.ShapeDtypeStruct((t, d), F32),
        compiler_params=_params("parallel"),
        name="merge",
    )(x2, ya, yb, yc, proj, w_branch, w_out, gate1)


def _fold_keys_kernel(keys_ref, wq_ref, o_ref):
    o_ref[...] = lax.dot_general(keys_ref[...], wq_ref[...], (((1,), (1,)), ((), ())),
                                 preferred_element_type=F32, precision=lax.Precision.HIGHEST)


def _fold_keys(pk_keys, pk_wq):
    d = pk_wq.shape[0]
    sets = 2 * PK_HEADS
    keys = pk_keys.reshape(sets, PK_NKEYS, PK_HALF)
    wq = pk_wq.reshape(d, sets, PK_HALF).transpose(1, 0, 2)
    out = pl.pallas_call(
        _fold_keys_kernel,
        grid=(sets,),
        in_specs=[
            pl.BlockSpec((None, PK_NKEYS, PK_HALF), lambda s: (s, 0, 0)),
            pl.BlockSpec((None, d, PK_HALF), lambda s: (s, 0, 0)),
        ],
        out_specs=pl.BlockSpec((None, PK_NKEYS, d), lambda s: (s, 0, 0)),
        out_shape=jax.ShapeDtypeStruct((sets, PK_NKEYS, d), F32),
        compiler_params=_params("parallel"),
        name="fold_keys",
    )(keys, wq)
    return out.reshape(sets * PK_NKEYS, d)


def _peer_scores_kernel(x_ref, g_ref, sc_ref, sh_ref, wk_ref, ht_ref, st_ref):
    h = _rms_mod(x_ref[...], g_ref[...], sc_ref[...], sh_ref[...])
    ht = h.T.astype(BF16)
    ht_ref[...] = ht
    st_ref[...] = _dot(wk_ref[...], ht)


def _peer_scores(x2, g, scale, shift, wk_t, seq):
    t, d = x2.shape
    tb = min(512, seq)
    per_b = seq // tb
    nk = wk_t.shape[0]
    return pl.pallas_call(
        _peer_scores_kernel,
        grid=(t // tb,),
        in_specs=[
            pl.BlockSpec((tb, d), lambda i: (i, 0)),
            pl.BlockSpec((1, d), lambda i: (0, 0)),
            pl.BlockSpec((None, 1, d), lambda i: (i // per_b, 0, 0)),
            pl.BlockSpec((None, 1, d), lambda i: (i // per_b, 0, 0)),
            pl.BlockSpec((nk, d), lambda i: (0, 0)),
        ],
        out_specs=[
            pl.BlockSpec((d, tb), lambda i: (0, i)),
            pl.BlockSpec((nk, tb), lambda i: (0, i)),
        ],
        out_shape=[jax.ShapeDtypeStruct((d, t), BF16), jax.ShapeDtypeStruct((nk, t), F32)],
        compiler_params=_params("parallel"),
        name="peer_scores",
    )(x2, g.reshape(1, d), scale, shift, wk_t)


NEG_INF = float("-inf")
PK_ROUNDS = PK_TOPK + 1
PK_RANK_NONE = 32.0


def _top_rows(x, rounds, want_rank=False):
    vals = []
    rank = jnp.full(x.shape, PK_RANK_NONE, F32)
    for r in range(rounds):
        m = jnp.max(x, axis=0, keepdims=True)
        vals.append(m)
        eq = x == m
        if want_rank:
            rank = jnp.where(eq, float(r), rank)
        if r + 1 < rounds:
            x = jnp.where(eq, NEG_INF, x)
    return (vals, rank) if want_rank else vals


def _count_ge(v, need):
    assert PK_TOPK == 16
    w = jnp.where
    c1 = v[7] >= need
    c2 = w(c1, v[11], v[3]) >= need
    c3 = w(c1, w(c2, v[13], v[9]), w(c2, v[5], v[1])) >= need
    c4 = w(c1, w(c2, w(c3, v[14], v[12]), w(c3, v[10], v[8])),
           w(c2, w(c3, v[6], v[4]), w(c3, v[2], v[0]))) >= need
    c5 = v[15] >= need
    return (w(c1, 8.0, 0.0) + w(c2, 4.0, 0.0) + w(c3, 2.0, 0.0) + w(c4, 1.0, 0.0) + w(c5, 1.0, 0.0))


def _peer_stats_kernel(st_ref, cnt_ref, e1_ref, r2_ref, e2_ref):
    tb = st_ref.shape[-1]
    for h in range(PK_HEADS):
        s1 = st_ref[h, 0]
        s2 = st_ref[h, 1]
        v1 = _top_rows(s1, PK_ROUNDS)
        v2, rank2 = _top_rows(s2, PK_ROUNDS, want_rank=True)
        v2_16 = jnp.concatenate(v2[0:16], axis=0)
        blocks = [v1[0] + v2_16]
        for a in range(1, 8):
            blocks.append(v1[a] + v2_16[0:8])
        blocks.append(jnp.concatenate(v1[8:16], axis=0) + v2[0])
        blocks.append(jnp.concatenate(
            [v1[0] + v2[16], v1[16] + v2[0], jnp.full((6, tb), NEG_INF, F32)], axis=0))
        cand = jnp.concatenate(blocks, axis=0)
        top = _top_rows(cand, PK_ROUNDS)
        thr = 0.5 * (top[PK_TOPK - 1] + top[PK_TOPK])
        zsum = jnp.zeros((1, tb), F32)
        for r in range(PK_TOPK):
            zsum = zsum + jnp.exp(top[r] - top[0])
        need = thr - s1
        cnt_ref[h] = _count_ge(v2, need)
        e1_ref[h] = jnp.exp(s1 - v1[0]) / zsum
        r2_ref[h] = rank2.astype(BF16)
        e2_ref[h] = jnp.exp(s2 - v2[0]).astype(BF16)


def _peer_stats(st):
    nk, t = st.shape
    tb = min(256, t)
    st4 = st.reshape(PK_HEADS, 2, PK_NKEYS, t)
    shp = lambda dt: jax.ShapeDtypeStruct((PK_HEADS, PK_NKEYS, t), dt)
    spec = pl.BlockSpec((PK_HEADS, PK_NKEYS, tb), lambda i: (0, 0, i))
    return pl.pallas_call(
        _peer_stats_kernel,
        grid=(t // tb,),
        in_specs=[pl.BlockSpec((PK_HEADS, 2, PK_NKEYS, tb), lambda i: (0, 0, 0, i))],
        out_specs=[spec, spec, spec, spec],
        out_shape=[shp(F32), shp(F32), shp(BF16), shp(BF16)],
        compiler_params=_params("parallel"),
        name="peer_stats",
    )(st4)


PK_ITILE = 8
PK_DENSE_TB = 512
GELU_C0 = 0.7978845608028654
GELU_C1 = 0.044715


def _gelu_tanh(a):
    return 0.5 * a * (1.0 + jnp.tanh(GELU_C0 * (a + GELU_C1 * (a * a * a))))


def _peer_tile_weights(a, row0, cnt_ref, e1_ref, r2_ref, e2_ref, p_ref):
    tb = a.shape[1]
    shape = (PK_NKEYS, tb)
    zero = jnp.zeros(shape, BF16)
    for il in range(PK_ITILE):
        g = _gelu_tanh(a[il * PK_NKEYS:(il + 1) * PK_NKEYS, :].astype(BF16))
        wsum = zero
        for h in range(PK_HEADS):
            cb = jnp.broadcast_to(cnt_ref[h, row0 + il:row0 + il + 1, :], shape).astype(BF16)
            eb = jnp.broadcast_to(e1_ref[h, row0 + il:row0 + il + 1, :], shape).astype(BF16)
            wsum = wsum + eb * jnp.where(cb > r2_ref[h], e2_ref[h], zero)
        p_ref[il * PK_NKEYS:(il + 1) * PK_NKEYS, :] = g * wsum


def _peer_dense_kernel(x_ref, g2_ref, fg_ref, ht_ref, u0_ref, ua_ref, ub_ref, vt_ref, cnt_ref, e1_ref,
                       r2_ref, e2_ref, o_ref, acc_ref, a_ref, pa_ref, pb_ref, *, final_norm):
    g = pl.program_id(1)
    te = pa_ref.shape[0]
    ht = ht_ref[...]

    @pl.when(g == 0)
    def _():
        acc_ref[...] = jnp.zeros_like(acc_ref)
        a_ref[0] = _dot(u0_ref[...], ht)

    _peer_tile_weights(a_ref.at[g % 2], 0, cnt_ref, e1_ref, r2_ref, e2_ref, pa_ref)
    a_odd = _dot(ua_ref[...], ht)
    acc_ref[...] += _dot(vt_ref[0], pa_ref[...])
    _peer_tile_weights(a_odd, PK_ITILE, cnt_ref, e1_ref, r2_ref, e2_ref, pb_ref)
    acc_ref[...] += _dot(vt_ref[1], pb_ref[...])
    a_ref[(g + 1) % 2] = _dot(ub_ref[...], ht)

    @pl.when(g == pl.num_programs(1) - 1)
    def _():
        y = x_ref[...] + g2_ref[...] * acc_ref[...].T
        if final_norm:
            ms = jnp.mean(y * y, axis=-1, keepdims=True)
            y = y * lax.rsqrt(ms + NORM_EPS) * fg_ref[...]
        o_ref[...] = y


def _peer_dense(x2, gate2, final_g, ht, u_bf, vt_bf, cnt, e1, r2, e2, seq, final_norm):
    t, d = x2.shape
    tb = min(PK_DENSE_TB, seq)
    per_b = seq // tb
    te = PK_ITILE * PK_NKEYS
    ntile = u_bf.shape[0] // te
    u_spec = lambda f: pl.BlockSpec((te, d), lambda i, g: (f(g), 0))
    pair_i = pl.BlockSpec((PK_HEADS, 2 * PK_ITILE, tb), lambda i, g: (0, g, i))
    full_j = pl.BlockSpec((PK_HEADS, PK_NKEYS, tb), lambda i, g: (0, 0, i))
    return pl.pallas_call(
        functools.partial(_peer_dense_kernel, final_norm=final_norm),
        grid=(t // tb, ntile // 2),
        in_specs=[
            pl.BlockSpec((tb, d), lambda i, g: (i, 0)),
            pl.BlockSpec((None, 1, d), lambda i, g: (i // per_b, 0, 0)),
            pl.BlockSpec((1, d), lambda i, g: (0, 0)),
            pl.BlockSpec((d, tb), lambda i, g: (0, i)),
            u_spec(lambda g: 0), u_spec(lambda g: 2 * g + 1),
            u_spec(lambda g: jnp.minimum(2 * g + 2, ntile - 1)),
            pl.BlockSpec((2, d, te), lambda i, g: (g, 0, 0)),
            pair_i, pair_i, full_j, full_j,
        ],
        out_specs=pl.BlockSpec((tb, d), lambda i, g: (i, 0)),
        out_shape=jax.ShapeDtypeStruct((t, d), F32),
        scratch_shapes=[pltpu.VMEM((d, tb), F32), pltpu.VMEM((2, te, tb), F32),
                        pltpu.VMEM((te, tb), BF16), pltpu.VMEM((te, tb), BF16)],
        compiler_params=_params("parallel", "arbitrary"),
        name="peer_dense",
    )(x2, gate2, final_g.reshape(1, d), ht, u_bf, u_bf, u_bf, vt_bf, cnt, e1, r2, e2)


def _permute_w_in(w_in):
    main = jnp.concatenate([w_in[..., 5640:8712], w_in[..., 3584:4608], w_in[..., 0:3584],
                            w_in[..., 4608:5632]], axis=-1).astype(BF16)
    gates = jnp.pad(w_in[..., 5632:5640], ((0, 0), (0, 0), (0, 120))).astype(BF16)
    return main, gates


def kernel(x, c, mod_w, mod_b, norm_mix_g, norm_ffn_g, w_in, ml_conv_w, ml_conv_b, ml_gate_b,
           hg_lb_logits, hg_norm_g, ml_norm_g, w_branch, w_out, pk_wq, pk_keys, pk_u, pk_v, final_g):
    bsz, seq, d = x.shape
    depth = w_in.shape[0]
    t = bsz * seq
    x2 = x.reshape(t, d)

    mod = _modulation(c, mod_w, mod_b).reshape(depth, bsz, 6, 1, d)
    w_main, w_gates = _permute_w_in(w_in)
    w_branch_bf = w_branch.astype(BF16)
    w_out_bf = w_out.astype(BF16)
    u_bf = pk_u.astype(BF16)
    te = PK_ITILE * PK_NKEYS
    vt_bf = jnp.swapaxes(pk_v.astype(BF16).reshape(depth, pk_v.shape[1] // te, te, d), 2, 3)

    for l in range(depth):
        shift1, scale1, gate1, shift2, scale2, gate2 = (mod[l, :, r] for r in range(6))
        proj, gates = _norm_proj(x2, norm_mix_g[l], scale1, shift1, w_main[l], w_gates[l], seq)
        ya = _sb_attention(proj, bsz, seq)
        yb = _hgrn(proj, hg_lb_logits, hg_norm_g[l], l, bsz, seq)
        yc = _mlstm(proj, gates, gates[:, 0:8].T, ml_conv_w[l], ml_conv_b[l], ml_gate_b[l],
                    ml_norm_g[l], bsz, seq)
        x2 = _merge(x2, ya, yb, yc, proj, w_branch_bf[l], w_out_bf[l], gate1, seq)
        wk_t = _fold_keys(pk_keys[l], pk_wq[l]).astype(BF16)
        ht, st = _peer_scores(x2, norm_ffn_g[l], scale2, shift2, wk_t, seq)
        cnt, e1, r2, e2 = _peer_stats(st)
        x2 = _peer_dense(x2, gate2, final_g, ht, u_bf[l], vt_bf[l], cnt, e1, r2, e2, seq,
                         final_norm=(l == depth - 1))
    return x2.reshape(bsz, seq, d)
```
